```python
import math
import jax
import jax.numpy as jnp
from jax import lax
import numpy as np

D_MODEL = 1024
BATCH = 2
SEQ = 8192
DEPTH = 2

CHUNK = 64
Q_BLOCK = 128
EPS = 1e-6
ROPE_THETA = 10000.0

A_HEADS = D_MODEL // 128
A_DK = 64
A_DV = 64
B_HEADS = D_MODEL // 128
B_DH = 64
C_DH = 64
C_DV = 2 * C_DH
C_HEADS = D_MODEL // C_DV
N_GROUPS = 4
EXPERTS_PER_GROUP = 8
TOP_K_IN_GROUP = 2
D_EXPERT = D_MODEL // 2

N_EVEN = (DEPTH + 1) // 2
N_ODD = DEPTH // 2

_EVEN_COLS = (A_HEADS * A_DK, A_HEADS * A_DK, A_HEADS * A_DV, A_HEADS * A_DV,
              B_HEADS * B_DH, B_HEADS * B_DH, B_HEADS * B_DH, B_HEADS)
EVEN_IN = sum(_EVEN_COLS)
EVEN_SPLITS = tuple(int(c) for c in np.cumsum(_EVEN_COLS)[:-1])
EVEN_OUT = A_HEADS * A_DV + B_HEADS * B_DH
ODD_IN = C_HEADS * (2 * C_DH + 2 * C_DH + C_DV)
ODD_OUT = C_HEADS * C_DV

kernel_name = 'hybrid_hgrn2_fox_diffattn_hmoe'


def rms_norm(x, gain):
    xf = x.astype(jnp.float32)
    y = xf * lax.rsqrt(jnp.mean(jnp.square(xf), axis=-1, keepdims=True) + EPS)
    return (y * gain.astype(jnp.float32)).astype(x.dtype)


def apply_rope(x, positions):
    half = x.shape[-1] // 2
    inv_freq = ROPE_THETA ** (-jnp.arange(half, dtype=jnp.float32) / half)
    ang = positions.astype(jnp.float32)[..., None] * inv_freq
    ang = ang.reshape(ang.shape[:2] + (1,) * (x.ndim - 3) + (half,))
    cos, sin = jnp.cos(ang), jnp.sin(ang)
    xf = x.astype(jnp.float32)
    x1, x2 = xf[..., :half], xf[..., half:]
    return jnp.concatenate([x1 * cos - x2 * sin, x2 * cos + x1 * sin], axis=-1).astype(x.dtype)


def hgrn2_recurrence(q, f_logit, i, g, lb, out_norm):
    B_, S_, H, DK = q.shape
    DV = i.shape[-1]
    nc = S_ // CHUNK
    qf = jax.nn.silu(q.astype(jnp.float32))
    f = lb + (1.0 - lb) * jax.nn.sigmoid(f_logit.astype(jnp.float32))
    log_f = jnp.log(f)
    k = 1.0 - f

    def to_chunks(t):
        return t.reshape(B_, nc, CHUNK, H, t.shape[-1]).transpose(1, 0, 3, 2, 4)

    qc, kc, vc, lfc = (to_chunks(t) for t in (qf, k, i.astype(jnp.float32), log_f))
    causal = jnp.tril(jnp.ones((CHUNK, CHUNK), dtype=bool))[:, :, None]

    def step(state, inp):
        qb, kb, vb, lf = inp
        cum = jnp.cumsum(lf, axis=2)
        rel = cum[:, :, :, None, :] - cum[:, :, None, :, :]
        decay = jnp.exp(jnp.where(causal, rel, -jnp.inf))
        scores = jnp.einsum('bhtsk,bhsk->bhts', qb[:, :, :, None, :] * decay, kb)
        o = (jnp.einsum('bhts,bhsv->bhtv', scores, vb)
             + jnp.einsum('bhtk,bhkv->bhtv', qb * jnp.exp(cum), state))
        last = cum[:, :, -1:, :]
        state = (jnp.exp(last[:, :, 0, :])[..., None] * state
                 + jnp.einsum('bhsk,bhsv->bhkv', kb * jnp.exp(last - cum), vb))
        return state, o

    state0 = jnp.zeros((B_, H, DK, DV), jnp.float32)
    _, o = lax.scan(step, state0, (qc, kc, vc, lfc))
    o = o.transpose(1, 0, 3, 2, 4).reshape(B_, S_, H, DV)
    o = rms_norm(o, out_norm) * jax.nn.silu(g.astype(jnp.float32))
    return o.astype(q.dtype)


def forgetting_attention(q, k, v, f_logit):
    B_, S_, H, Dh = q.shape
    scale = Dh ** -0.5
    log_f = jax.nn.log_sigmoid(f_logit.astype(jnp.float32))
    cum = jnp.cumsum(log_f, axis=1).transpose(0, 2, 1)
    qh, kh, vh = (t.transpose(0, 2, 1, 3) for t in (q, k, v))
    outs = []
    for q0 in range(0, S_, Q_BLOCK):
        q1 = q0 + Q_BLOCK
        logits = jnp.einsum('bhtd,bhsd->bhts', qh[:, :, q0:q1], kh[:, :, :q1]).astype(jnp.float32) * scale
        logits = logits + cum[:, :, q0:q1, None] - cum[:, :, None, :q1]
        t_pos = jnp.arange(q0, q1)[:, None]
        s_pos = jnp.arange(q1)[None, :]
        logits = jnp.where(s_pos <= t_pos, logits, -jnp.inf)
        p = jax.nn.softmax(logits, axis=-1).astype(v.dtype)
        outs.append(jnp.einsum('bhts,bhsd->bhtd', p, vh[:, :, :q1]))
    return jnp.concatenate(outs, axis=2).transpose(0, 2, 1, 3)


def differential_attention(q, k, v, lam, lambda_init, subln):
    B_, S_, H, _, Dh = q.shape
    scale = Dh ** -0.5
    qh = q.transpose(0, 2, 3, 1, 4)
    kh = k.transpose(0, 2, 3, 1, 4)
    vh = v.transpose(0, 2, 1, 3)
    outs = []
    for q0 in range(0, S_, Q_BLOCK):
        q1 = q0 + Q_BLOCK
        logits = jnp.einsum('bhmtd,bhmsd->bhmts', qh[:, :, :, q0:q1], kh[:, :, :, :q1]).astype(jnp.float32) * scale
        t_chunk = (jnp.arange(q0, q1) // CHUNK)[:, None]
        s_chunk = (jnp.arange(q1) // CHUNK)[None, :]
        logits = jnp.where(s_chunk <= t_chunk, logits, -jnp.inf)
        p = jax.nn.softmax(logits, axis=-1)
        p_diff = (p[:, :, 0] - lam * p[:, :, 1]).astype(v.dtype)
        outs.append(jnp.einsum('bhts,bhsv->bhtv', p_diff, vh[:, :, :q1]))
    o = jnp.concatenate(outs, axis=2).transpose(0, 2, 1, 3)
    return rms_norm(o, subln) * (1.0 - lambda_init)


def even_mixer(u, w_in, w_out, lb, fox_f_bias, hgrn_out_norm, fox_q_norm, fox_k_norm):
    B_, S_, _ = u.shape
    proj = u @ w_in
    a_q, a_f, a_i, a_g, b_q, b_k, b_v, b_f = jnp.split(proj, EVEN_SPLITS, axis=-1)
    heads = lambda t, h: t.reshape(B_, S_, h, t.shape[-1] // h)
    o_a = hgrn2_recurrence(heads(a_q, A_HEADS), heads(a_f, A_HEADS), heads(a_i, A_HEADS),
                           heads(a_g, A_HEADS), lb, hgrn_out_norm)
    o_b = forgetting_attention(rms_norm(heads(b_q, B_HEADS), fox_q_norm),
                               rms_norm(heads(b_k, B_HEADS), fox_k_norm),
                               heads(b_v, B_HEADS), b_f + fox_f_bias)
    o = jnp.concatenate([o_a.reshape(B_, S_, -1), o_b.reshape(B_, S_, -1)], axis=-1)
    return o @ w_out


def odd_mixer(u, positions, w_in, w_out, q_norm, k_norm, lq1, lk1, lq2, lk2, subln, lambda_init):
    B_, S_, _ = u.shape
    q, k, v = jnp.split(u @ w_in, 3, axis=-1)
    q = apply_rope(rms_norm(q.reshape(B_, S_, C_HEADS, 2, C_DH), q_norm), positions)
    k = apply_rope(rms_norm(k.reshape(B_, S_, C_HEADS, 2, C_DH), k_norm), positions)
    v = v.reshape(B_, S_, C_HEADS, C_DV)
    f32 = jnp.float32
    lam = (jnp.exp(jnp.sum(lq1.astype(f32) * lk1.astype(f32)))
           - jnp.exp(jnp.sum(lq2.astype(f32) * lk2.astype(f32))) + lambda_init)
    o = differential_attention(q, k, v, lam, lambda_init, subln)
    return o.reshape(B_, S_, ODD_OUT) @ w_out


def hierarchical_moe(u, w_router_group, w_router_expert, w_gate, w_up, w_down):
    B_, S_, D = u.shape
    x = u.reshape(B_ * S_, D)
    grp_logits = (x @ w_router_group).astype(jnp.float32)
    p_grp = jax.nn.softmax(grp_logits, axis=-1)
    g_onehot = jax.nn.one_hot(jnp.argmax(grp_logits, axis=-1), N_GROUPS, dtype=jnp.float32)
    p_g = jnp.sum(p_grp * g_onehot, axis=-1, keepdims=True)
    exp_logits = jnp.einsum('nd,gde->nge', x, w_router_expert).astype(jnp.float32)
    sel_logits = jnp.einsum('nge,ng->ne', exp_logits, g_onehot)
    p_exp = jax.nn.softmax(sel_logits, axis=-1)
    top_w, top_i = lax.top_k(p_exp, TOP_K_IN_GROUP)
    top_w = top_w / jnp.sum(top_w, axis=-1, keepdims=True)
    w_in_group = jnp.sum(jax.nn.one_hot(top_i, EXPERTS_PER_GROUP, dtype=jnp.float32) * top_w[..., None], axis=1)
    combine = (g_onehot[:, :, None] * (p_g[:, :, None] * w_in_group[:, None, :])).astype(x.dtype)
    y = jnp.zeros_like(x)
    for g in range(N_GROUPS):
        hid = jax.nn.silu(jnp.einsum('nd,edf->nef', x, w_gate[g])) * jnp.einsum('nd,edf->nef', x, w_up[g])
        y = y + jnp.einsum('nef,efd->nd', hid * combine[:, g, :, None], w_down[g])
    return y.reshape(B_, S_, D)


def setup_inputs(seed: int = 0) -> dict:
    key = jax.random.key(seed)
    ks = jax.random.split(key, 32)
    f32 = jnp.float32

    def nrm(k, shape, scale):
        return jax.random.normal(k, shape, f32) * scale

    def gain(k, shape):
        return 1.0 + 0.05 * jax.random.normal(k, shape, f32)

    res_scale = (2 * DEPTH) ** -0.5
    x = jax.random.normal(ks[0], (BATCH, SEQ, D_MODEL), f32)
    start = jax.random.randint(ks[1], (BATCH, 1), 0, 4096, dtype=jnp.int32)
    positions = start + jnp.arange(SEQ, dtype=jnp.int32)[None, :]
    return {
        'x': x,
        'positions': positions,
        'hgrn_lb_logits': nrm(ks[2], (N_EVEN + 1, A_HEADS * A_DK), 0.5),
        'norm_mix': gain(ks[3], (DEPTH, D_MODEL)),
        'norm_ffn': gain(ks[4], (DEPTH, D_MODEL)),
        'even_w_in': nrm(ks[5], (N_EVEN, D_MODEL, EVEN_IN), D_MODEL ** -0.5),
        'even_w_out': nrm(ks[6], (N_EVEN, EVEN_OUT, D_MODEL), EVEN_OUT ** -0.5 * res_scale),
        'fox_f_bias': jnp.linspace(1.0, 4.0, B_HEADS, dtype=f32)[None, :] + nrm(ks[7], (N_EVEN, B_HEADS), 0.1),
        'hgrn_out_norm': gain(ks[8], (N_EVEN, A_DV)),
        'fox_q_norm': gain(ks[9], (N_EVEN, B_DH)),
        'fox_k_norm': gain(ks[10], (N_EVEN, B_DH)),
        'odd_w_in': nrm(ks[11], (N_ODD, D_MODEL, ODD_IN), D_MODEL ** -0.5),
        'odd_w_out': nrm(ks[12], (N_ODD, ODD_OUT, D_MODEL), ODD_OUT ** -0.5 * res_scale),
        'diff_q_norm': gain(ks[13], (N_ODD, C_DH)),
        'diff_k_norm': gain(ks[14], (N_ODD, C_DH)),
        'diff_lambda_q1': nrm(ks[15], (N_ODD, C_DH), 0.1),
        'diff_lambda_k1': nrm(ks[16], (N_ODD, C_DH), 0.1),
        'diff_lambda_q2': nrm(ks[17], (N_ODD, C_DH), 0.1),
        'diff_lambda_k2': nrm(ks[18], (N_ODD, C_DH), 0.1),
        'diff_subln': gain(ks[19], (N_ODD, C_DV)),
        'moe_router_group': nrm(ks[20], (DEPTH, D_MODEL, N_GROUPS), D_MODEL ** -0.5),
        'moe_router_expert': nrm(ks[21], (DEPTH, N_GROUPS, D_MODEL, EXPERTS_PER_GROUP), D_MODEL ** -0.5),
        'moe_w_gate': nrm(ks[22], (DEPTH, N_GROUPS, EXPERTS_PER_GROUP, D_MODEL, D_EXPERT), D_MODEL ** -0.5),
        'moe_w_up': nrm(ks[23], (DEPTH, N_GROUPS, EXPERTS_PER_GROUP, D_MODEL, D_EXPERT), D_MODEL ** -0.5),
        'moe_w_down': nrm(ks[24], (DEPTH, N_GROUPS, EXPERTS_PER_GROUP, D_EXPERT, D_MODEL), D_EXPERT ** -0.5 * res_scale),
    }


def reference(x, positions, hgrn_lb_logits, norm_mix, norm_ffn, even_w_in, even_w_out, fox_f_bias,
              hgrn_out_norm, fox_q_norm, fox_k_norm, odd_w_in, odd_w_out, diff_q_norm, diff_k_norm,
              diff_lambda_q1, diff_lambda_k1, diff_lambda_q2, diff_lambda_k2, diff_subln,
              moe_router_group, moe_router_expert, moe_w_gate, moe_w_up, moe_w_down):
    lower_bounds = jnp.cumsum(jax.nn.softmax(hgrn_lb_logits.astype(jnp.float32), axis=0), axis=0)
    h = x
    for layer in range(DEPTH):
        j = layer // 2
        u = rms_norm(h, norm_mix[layer])
        if layer % 2 == 0:
            mix = even_mixer(u, even_w_in[j], even_w_out[j], lower_bounds[j].reshape(A_HEADS, A_DK),
                             fox_f_bias[j], hgrn_out_norm[j], fox_q_norm[j], fox_k_norm[j])
        else:
            lambda_init = 0.8 - 0.6 * math.exp(-0.3 * layer)
            mix = odd_mixer(u, positions, odd_w_in[j], odd_w_out[j], diff_q_norm[j], diff_k_norm[j],
                            diff_lambda_q1[j], diff_lambda_k1[j], diff_lambda_q2[j], diff_lambda_k2[j],
                            diff_subln[j], lambda_init)
        h = h + mix
        h = h + hierarchical_moe(rms_norm(h, norm_ffn[layer]), moe_router_group[layer], moe_router_expert[layer],
                                 moe_w_gate[layer], moe_w_up[layer], moe_w_down[layer])
    return h
```

```python
import functools
import math

import numpy as np
import jax
import jax.numpy as jnp
from jax import lax
from jax.experimental import pallas as pl
from jax.experimental.pallas import tpu as pltpu

F32 = jnp.float32
BF16 = jnp.bfloat16

EPS = 1e-6
ROPE_THETA = 10000.0
CHUNK = 64
HEAD_DIM = 64
N_GROUPS = 4
EXPERTS_PER_GROUP = 8
N_EXPERTS = N_GROUPS * EXPERTS_PER_GROUP
LANES = 128

HGRN_CHUNK = 64
HGRN_ROWS = 256
ATTN_TILE = 512
PREP_ROWS = 256
PROJ_TM = 512
PROJ_TN = 512
ROUTER_ROWS = 256
MOE_TM = 256
COMBINE_ROWS = 256
VMEM_LIMIT = 56 * 1024 * 1024


def _split3(x):
    hi = x.astype(BF16)
    r1 = x - hi.astype(F32)
    mid = r1.astype(BF16)
    lo = (r1 - mid.astype(F32)).astype(BF16)
    return hi, mid, lo


def _dot3(const_bf16, x):
    hi, mid, lo = _split3(x)
    d = lambda b: jnp.dot(const_bf16, b, preferred_element_type=F32)
    return d(hi) + d(mid) + d(lo)


def _dot3_rhs(x, const_bf16):
    hi, mid, lo = _split3(x)
    d = lambda a: jnp.dot(a, const_bf16, preferred_element_type=F32)
    return d(hi) + d(mid) + d(lo)


def _dot_nt(a, b):
    return lax.dot_general(a, b, (((1,), (1,)), ((), ())), preferred_element_type=F32)


def _dot_tn(a, b):
    return lax.dot_general(a, b, (((0,), (0,)), ((), ())), preferred_element_type=F32)


def _sigmoid(x):
    return 1.0 / (1.0 + jnp.exp(-x))


def _norm_proj_kernel(has_aux, x_ref, g_ref, w_ref, *rest):
    if has_aux:
        waux_ref, o_ref, oaux_ref, xn_ref = rest
    else:
        o_ref, xn_ref = rest
    j = pl.program_id(1)

    @pl.when(j == 0)
    def _():
        x = x_ref[...]
        ms = jnp.mean(x * x, axis=-1, keepdims=True)
        xn = x * lax.rsqrt(ms + EPS) * g_ref[...]
        xn_ref[...] = xn.astype(BF16)
        if has_aux:
            oaux_ref[...] = jnp.dot(xn, waux_ref[...], precision=lax.Precision.HIGHEST,
                                    preferred_element_type=F32)

    o_ref[...] = jnp.dot(xn_ref[...], w_ref[...], preferred_element_type=F32).astype(o_ref.dtype)


def _norm_proj(x, gain, w, w_aux=None):
    n, d = x.shape
    m = w.shape[1]
    tm, tn = PROJ_TM, PROJ_TN
    has_aux = w_aux is not None
    in_specs = [pl.BlockSpec((tm, d), lambda i, j: (i, 0)),
                pl.BlockSpec((1, d), lambda i, j: (0, 0)),
                pl.BlockSpec((d, tn), lambda i, j: (0, j))]
    out_specs = [pl.BlockSpec((tm, tn), lambda i, j: (i, j))]
    out_shape = [jax.ShapeDtypeStruct((n, m), F32)]
    args = [x, gain.reshape(1, d), w]
    if has_aux:
        in_specs.append(pl.BlockSpec((d, LANES), lambda i, j: (0, 0)))
        out_specs.append(pl.BlockSpec((tm, LANES), lambda i, j: (i, 0)))
        out_shape.append(jax.ShapeDtypeStruct((n, LANES), F32))
        args.append(w_aux)
    res = pl.pallas_call(
        functools.partial(_norm_proj_kernel, has_aux),
        grid=(n // tm, m // tn),
        in_specs=in_specs, out_specs=out_specs, out_shape=out_shape,
        scratch_shapes=[pltpu.VMEM((tm, d), BF16)],
        compiler_params=pltpu.CompilerParams(
            dimension_semantics=("parallel", "arbitrary"), vmem_limit_bytes=VMEM_LIMIT),
    )(*args)
    return res if has_aux else res[0]


def _proj_res_kernel(n_in, *refs):
    h_ref = refs[2 * n_in]
    o_ref = refs[2 * n_in + 1]
    acc = h_ref[...]
    for t in range(n_in):
        acc = acc + jnp.dot(refs[2 * t][...], refs[2 * t + 1][...], preferred_element_type=F32)
    o_ref[...] = acc


def _proj_residual(pairs, h):
    n, d = h.shape
    tm, tn = PROJ_TM, PROJ_TN
    in_specs, args = [], []
    for a, w in pairs:
        k = a.shape[1]
        in_specs += [pl.BlockSpec((tm, k), lambda i, j: (i, 0)),
                     pl.BlockSpec((k, tn), lambda i, j: (0, j))]
        args += [a, w]
    in_specs.append(pl.BlockSpec((tm, tn), lambda i, j: (i, j)))
    args.append(h)
    return pl.pallas_call(
        functools.partial(_proj_res_kernel, len(pairs)),
        grid=(n // tm, d // tn),
        in_specs=in_specs,
        out_specs=pl.BlockSpec((tm, tn), lambda i, j: (i, j)),
        out_shape=jax.ShapeDtypeStruct((n, d), F32),
        compiler_params=pltpu.CompilerParams(
            dimension_semantics=("parallel", "parallel"), vmem_limit_bytes=VMEM_LIMIT),
    )(*args)


_HGRN_LEVELS = (64, 32, 16)
_HGRN_DIAG = 8


def _hgrn_constants():
    c = HGRN_CHUNK
    idx = np.arange(c)
    low = (idx[None, :] <= idx[:, None]).astype(np.float64)

    def ref_rows(r):
        return (idx[None, :] <= r[:, None]).astype(np.float64)

    blocks = [low, ref_rows(np.full(c, c - 1)) - low]
    masks = []
    for b in _HGRN_LEVELS:
        start = (idx // b) * b
        upper = (idx - start) >= b // 2
        ref = start + b // 2 - 1
        blocks.append(low - ref_rows(np.where(upper, ref, idx)))
        blocks.append(ref_rows(np.where(upper, idx, ref)) - low)
        same = (idx[:, None] // b) == (idx[None, :] // b)
        masks.append(same & upper[:, None] & ~upper[None, :])
    ref = (idx // _HGRN_DIAG) * _HGRN_DIAG + _HGRN_DIAG // 2 - 1
    blocks.append(low - ref_rows(ref))
    blocks.append(ref_rows(ref) - low)
    same = (idx[:, None] // _HGRN_DIAG) == (idx[None, :] // _HGRN_DIAG)
    masks.append(same & (idx[None, :] <= idx[:, None]))
    dst = np.concatenate(blocks, axis=0)
    return dst.astype(np.float32), np.stack(masks).astype(np.float32)


def _hgrn_kernel(q_ref, f_ref, i_ref, g_ref, lb_ref, gn_ref, dst_ref, mask_ref, o_ref, st_ref):
    c = HGRN_CHUNK
    n_heads = st_ref.shape[0]
    n_lvl = mask_ref.shape[0]

    @pl.when(pl.program_id(1) == 0)
    def _():
        st_ref[...] = jnp.zeros_like(st_ref)

    lb = lb_ref[...]
    gn = gn_ref[...]
    dst = dst_ref[...]
    for ch in range(q_ref.shape[0] // c):
        rows = pl.ds(ch * c, c)
        q = q_ref[rows, :]
        qf = q * _sigmoid(q)
        f = lb + (1.0 - lb) * _sigmoid(f_ref[rows, :])
        kk = 1.0 - f
        ex = jnp.exp(_dot3(dst, jnp.log(f)))
        v = i_ref[rows, :].astype(BF16)
        g = g_ref[rows, :]
        gate = g * _sigmoid(g)
        q_in = (qf * ex[0:c]).astype(BF16)
        k_st = (kk * ex[c:2 * c]).astype(BF16)
        dec = ex[c - 1:c]
        q_l = [(qf * ex[(2 + 2 * l) * c:(3 + 2 * l) * c]).astype(BF16) for l in range(n_lvl)]
        k_l = [(kk * ex[(3 + 2 * l) * c:(4 + 2 * l) * c]).astype(BF16) for l in range(n_lvl)]
        outs = []
        for h in range(n_heads):
            hs = slice(h * HEAD_DIM, (h + 1) * HEAD_DIM)
            scores = mask_ref[0] * _dot_nt(q_l[0][:, hs], k_l[0][:, hs])
            for l in range(1, n_lvl):
                scores = scores + mask_ref[l] * _dot_nt(q_l[l][:, hs], k_l[l][:, hs])
            st = st_ref[h]
            o = (jnp.dot(scores.astype(BF16), v[:, hs], preferred_element_type=F32)
                 + _dot_nt(q_in[:, hs], st.astype(BF16)))
            st_ref[h] = st * dec[:, hs] + _dot_tn(v[:, hs], k_st[:, hs])
            ms = jnp.mean(o * o, axis=-1, keepdims=True)
            outs.append(o * lax.rsqrt(ms + EPS) * gn)
        o_ref[rows, :] = (jnp.concatenate(outs, axis=-1) * gate).astype(o_ref.dtype)


def _hgrn2(proj, lb, out_norm, batch, seq):
    n = proj.shape[0]
    width = lb.shape[0]
    n_heads = width // HEAD_DIM
    rb = HGRN_ROWS
    spb = seq // rb
    dst, masks = _hgrn_constants()
    col = lambda j: pl.BlockSpec((rb, width), lambda b, s, j=j: (b * spb + s, j))
    full = lambda a: pl.BlockSpec(a.shape, lambda b, s: (0,) * a.ndim)
    lb2 = lb.reshape(1, width)
    gn = out_norm.reshape(1, HEAD_DIM)
    dst = jnp.asarray(dst, BF16)
    masks = jnp.asarray(masks, F32)
    return pl.pallas_call(
        _hgrn_kernel,
        grid=(batch, spb),
        in_specs=[col(0), col(1), col(2), col(3), full(lb2), full(gn), full(dst), full(masks)],
        out_specs=pl.BlockSpec((rb, width), lambda b, s: (b * spb + s, 0)),
        out_shape=jax.ShapeDtypeStruct((n, width), BF16),
        scratch_shapes=[pltpu.VMEM((n_heads, HEAD_DIM, HEAD_DIM), F32)],
        compiler_params=pltpu.CompilerParams(
            dimension_semantics=("parallel", "arbitrary"), vmem_limit_bytes=VMEM_LIMIT),
    )(proj, proj, proj, proj, lb2, gn, dst, masks)


def _fox_prep_kernel(q_ref, k_ref, v_ref, gate_ref, bias_ref, gq_ref, gk_ref, tril_ref,
                     qa_ref, ka_ref, va_ref, carry_ref):
    n_heads = qa_ref.shape[1]
    tm = q_ref.shape[0]

    @pl.when(pl.program_id(1) == 0)
    def _():
        carry_ref[...] = jnp.zeros_like(carry_ref)

    z = gate_ref[...] + bias_ref[...]
    ls = -(jnp.maximum(-z, 0.0) + jnp.log(1.0 + jnp.exp(-jnp.abs(z))))
    cum = _dot3(tril_ref[...], ls) + carry_ref[...]
    carry_ref[...] = cum[tm - 1:tm]

    lane = lax.broadcasted_iota(jnp.int32, (tm, HEAD_DIM), 1)
    one = jnp.ones((tm, HEAD_DIM), F32)
    zero = jnp.zeros((tm, HEAD_DIM), F32)
    scale = HEAD_DIM ** -0.5
    for h in range(n_heads):
        hs = slice(h * HEAD_DIM, (h + 1) * HEAD_DIM)
        qh = q_ref[:, hs]
        kh = k_ref[:, hs]
        qn = qh * lax.rsqrt(jnp.mean(qh * qh, axis=-1, keepdims=True) + EPS) * gq_ref[...] * scale
        kn = kh * lax.rsqrt(jnp.mean(kh * kh, axis=-1, keepdims=True) + EPS) * gk_ref[...]
        c_hi, c_mid, c_lo = (t.astype(F32) for t in _split3(cum[:, h:h + 1]))
        exq = jnp.where(lane == 0, c_hi, jnp.where(lane == 1, c_mid, jnp.where(lane == 2, c_lo,
                        jnp.where(lane < 6, one, zero))))
        exk = jnp.where(lane < 3, one, jnp.where(lane == 3, -c_hi, jnp.where(lane == 4, -c_mid,
                        jnp.where(lane == 5, -c_lo, zero))))
        exv = jnp.where(lane == 0, one, zero)
        qa_ref[0, h] = jnp.concatenate([qn, exq], axis=-1).astype(BF16)
        ka_ref[0, h] = jnp.concatenate([kn, exk], axis=-1).astype(BF16)
        va_ref[0, h] = jnp.concatenate([v_ref[:, hs], exv], axis=-1).astype(BF16)


def _fox_prep(proj, gates, f_bias, q_norm, k_norm, batch, seq, col0):
    width = 512
    n_heads = width // HEAD_DIM
    tm = PREP_ROWS
    spb = seq // tm
    col = lambda j: pl.BlockSpec((tm, width), lambda b, s, j=j: (b * spb + s, col0 + j))
    full = lambda a: pl.BlockSpec(a.shape, lambda b, s: (0,) * a.ndim)
    bias = jnp.zeros((1, LANES), F32).at[0, :n_heads].set(f_bias)
    gq = q_norm.reshape(1, HEAD_DIM)
    gk = k_norm.reshape(1, HEAD_DIM)
    tril = jnp.asarray(np.tril(np.ones((tm, tm), np.float32)), BF16)
    out = jax.ShapeDtypeStruct((batch, n_heads, seq, LANES), BF16)
    ospec = pl.BlockSpec((1, n_heads, tm, LANES), lambda b, s: (b, 0, s, 0))
    return pl.pallas_call(
        _fox_prep_kernel,
        grid=(batch, spb),
        in_specs=[col(0), col(1), col(2),
                  pl.BlockSpec((tm, LANES), lambda b, s: (b * spb + s, 0)),
                  full(bias), full(gq), full(gk), full(tril)],
        out_specs=[ospec, ospec, ospec],
        out_shape=[out, out, out],
        scratch_shapes=[pltpu.VMEM((1, LANES), F32)],
        compiler_params=pltpu.CompilerParams(
            dimension_semantics=("parallel", "arbitrary"), vmem_limit_bytes=VMEM_LIMIT),
    )(proj, proj, proj, gates, bias, gq, gk, tril)


def _tri_tables(nq):
    qi = [q for q in range(nq) for _ in range(q + 1)]
    ki = [k for q in range(nq) for k in range(q + 1)]
    return jnp.asarray(qi, jnp.int32), jnp.asarray(ki, jnp.int32)


def _fox_attn_kernel(qt_ref, kt_ref, q_ref, k_ref, v_ref, o_ref, m_ref, acc_ref):
    p_idx = pl.program_id(2)
    qi = qt_ref[p_idx]
    ki = kt_ref[p_idx]
    hp = q_ref.shape[1]
    t = q_ref.shape[2]

    @pl.when(ki == 0)
    def _():
        m_ref[...] = jnp.full_like(m_ref, -jnp.inf)
        acc_ref[...] = jnp.zeros_like(acc_ref)

    def step(masked):
        for h in range(hp):
            s = _dot_nt(q_ref[0, h], k_ref[0, h])
            if masked:
                row = lax.broadcasted_iota(jnp.int32, (t, t), 0)
                colm = lax.broadcasted_iota(jnp.int32, (t, t), 1)
                s = jnp.where(colm <= row, s, -jnp.inf)
            m_old = m_ref[h]
            m_new = jnp.maximum(m_old, jnp.max(s, axis=-1, keepdims=True))
            p = jnp.exp(s - m_new)
            acc_ref[h] = (jnp.exp(m_old - m_new) * acc_ref[h]
                          + jnp.dot(p.astype(BF16), v_ref[0, h], preferred_element_type=F32))
            m_ref[h] = m_new

    @pl.when(ki < qi)
    def _():
        step(False)

    @pl.when(ki == qi)
    def _():
        step(True)
        outs = []
        for h in range(hp):
            acc = acc_ref[h]
            outs.append(acc[:, :HEAD_DIM] / acc[:, HEAD_DIM:HEAD_DIM + 1])
        o_ref[0] = jnp.concatenate(outs, axis=-1).astype(o_ref.dtype)


def _fox_attention(qa, ka, va):
    batch, n_heads, seq, _ = qa.shape
    t = ATTN_TILE
    hp = 2
    nq = seq // t
    qt, kt = _tri_tables(nq)
    qspec = pl.BlockSpec((1, hp, t, LANES), lambda b, g, p, qt, kt: (b, g, qt[p], 0))
    kspec = pl.BlockSpec((1, hp, t, LANES), lambda b, g, p, qt, kt: (b, g, kt[p], 0))
    return pl.pallas_call(
        _fox_attn_kernel,
        grid_spec=pltpu.PrefetchScalarGridSpec(
            num_scalar_prefetch=2,
            grid=(batch, n_heads // hp, int(qt.shape[0])),
            in_specs=[qspec, kspec, kspec],
            out_specs=pl.BlockSpec((1, t, hp * HEAD_DIM), lambda b, g, p, qt, kt: (b, qt[p], g)),
            scratch_shapes=[pltpu.VMEM((hp, t, 1), F32), pltpu.VMEM((hp, t, LANES), F32)]),
        out_shape=jax.ShapeDtypeStruct((batch, seq, n_heads * HEAD_DIM), BF16),
        compiler_params=pltpu.CompilerParams(
            dimension_semantics=("parallel", "parallel", "arbitrary"), vmem_limit_bytes=VMEM_LIMIT),
    )(qt, kt, qa, ka, va)


def _diff_prep_kernel(q_ref, k_ref, v_ref, pos_ref, invf_ref, gq_ref, gk_ref, grp_ref,
                      qm_ref, k2_ref, va_ref):
    n_heads = k2_ref.shape[1]
    tm = q_ref.shape[0]
    ang = pos_ref[...].astype(F32) * invf_ref[...]
    lane = lax.broadcasted_iota(jnp.int32, (tm, LANES), 1)
    first = (lane % HEAD_DIM) < (HEAD_DIM // 2)
    cs = jnp.cos(ang)
    sn = jnp.sin(ang)
    sn = jnp.where(first, -sn, sn)
    grp = grp_ref[...]
    scale = HEAD_DIM ** -0.5
    zero = jnp.zeros((tm, LANES), F32)
    onecol = jnp.where(lane == 0, 1.0, 0.0).astype(BF16)

    def norm_rope(x, gain):
        ms = _dot3_rhs(x * x, grp)
        y = x * lax.rsqrt(ms + EPS) * gain
        yr = jnp.where(first, pltpu.roll(y, LANES - HEAD_DIM // 2, 1), pltpu.roll(y, HEAD_DIM // 2, 1))
        return y * cs + yr * sn

    for h in range(n_heads):
        cols = slice(h * LANES, (h + 1) * LANES)
        qr = norm_rope(q_ref[:, cols], gq_ref[...]) * scale
        kr = norm_rope(k_ref[:, cols], gk_ref[...])
        qm_ref[0, h, 0] = jnp.where(lane < HEAD_DIM, qr, zero).astype(BF16)
        qm_ref[0, h, 1] = jnp.where(lane < HEAD_DIM, zero, qr).astype(BF16)
        k2_ref[0, h] = kr.astype(BF16)
        va_ref[0, h] = jnp.concatenate([v_ref[:, cols].astype(BF16), onecol], axis=-1)


def _diff_prep(proj, positions, q_norm, k_norm, batch, seq):
    n = proj.shape[0]
    width = proj.shape[1] // 3
    n_heads = width // LANES
    tm = PREP_ROWS
    spb = seq // tm
    col = lambda j: pl.BlockSpec((tm, width), lambda b, s, j=j: (b * spb + s, j))
    full = lambda a: pl.BlockSpec(a.shape, lambda b, s: (0,) * a.ndim)
    half = HEAD_DIM // 2
    inv_freq = ROPE_THETA ** (-jnp.arange(half, dtype=F32) / half)
    invf = jnp.tile(inv_freq, LANES // half).reshape(1, LANES)
    gq = jnp.tile(q_norm, LANES // HEAD_DIM).reshape(1, LANES)
    gk = jnp.tile(k_norm, LANES // HEAD_DIM).reshape(1, LANES)
    lane = np.arange(LANES)
    grp = jnp.asarray(((lane[:, None] // HEAD_DIM) == (lane[None, :] // HEAD_DIM)) / HEAD_DIM, BF16)
    pos = positions.reshape(n, 1).astype(jnp.int32)
    return pl.pallas_call(
        _diff_prep_kernel,
        grid=(batch, spb),
        in_specs=[col(0), col(1), col(2),
                  pl.BlockSpec((tm, 1), lambda b, s: (b * spb + s, 0)),
                  full(invf), full(gq), full(gk), full(grp)],
        out_specs=[pl.BlockSpec((1, n_heads, 2, tm, LANES), lambda b, s: (b, 0, 0, s, 0)),
                   pl.BlockSpec((1, n_heads, tm, LANES), lambda b, s: (b, 0, s, 0)),
                   pl.BlockSpec((1, n_heads, tm, 2 * LANES), lambda b, s: (b, 0, s, 0))],
        out_shape=[jax.ShapeDtypeStruct((batch, n_heads, 2, seq, LANES), BF16),
                   jax.ShapeDtypeStruct((batch, n_heads, seq, LANES), BF16),
                   jax.ShapeDtypeStruct((batch, n_heads, seq, 2 * LANES), BF16)],
        compiler_params=pltpu.CompilerParams(
            dimension_semantics=("parallel", "parallel"), vmem_limit_bytes=VMEM_LIMIT),
    )(proj, proj, proj, pos, invf, gq, gk, grp)


def _diff_attn_kernel(lambda_init, qt_ref, kt_ref, q_ref, k_ref, v_ref, lam_ref, sub_ref,
                      o_ref, m_ref, acc_ref):
    p_idx = pl.program_id(2)
    qi = qt_ref[p_idx]
    ki = kt_ref[p_idx]
    t = k_ref.shape[2]
    dv = o_ref.shape[2]

    @pl.when(ki == 0)
    def _():
        m_ref[...] = jnp.full_like(m_ref, -jnp.inf)
        acc_ref[...] = jnp.zeros_like(acc_ref)

    def step(masked):
        for m in range(2):
            s = _dot_nt(q_ref[0, 0, m], k_ref[0, 0])
            if masked:
                row = lax.broadcasted_iota(jnp.int32, (t, t), 0) // CHUNK
                colm = lax.broadcasted_iota(jnp.int32, (t, t), 1) // CHUNK
                s = jnp.where(colm <= row, s, -jnp.inf)
            m_old = m_ref[m]
            m_new = jnp.maximum(m_old, jnp.max(s, axis=-1, keepdims=True))
            p = jnp.exp(s - m_new)
            acc_ref[m] = (jnp.exp(m_old - m_new) * acc_ref[m]
                          + jnp.dot(p.astype(BF16), v_ref[0, 0], preferred_element_type=F32))
            m_ref[m] = m_new

    @pl.when(ki < qi)
    def _():
        step(False)

    @pl.when(ki == qi)
    def _():
        step(True)
        lp = lam_ref[...]
        lam = (jnp.exp(jnp.sum(lp[0:1] * lp[1:2], axis=-1, keepdims=True))
               - jnp.exp(jnp.sum(lp[2:3] * lp[3:4], axis=-1, keepdims=True)) + lambda_init)
        a0 = acc_ref[0]
        a1 = acc_ref[1]
        o = a0[:, :dv] / a0[:, dv:dv + 1] - lam * (a1[:, :dv] / a1[:, dv:dv + 1])
        ms = jnp.mean(o * o, axis=-1, keepdims=True)
        o_ref[0] = ((o * lax.rsqrt(ms + EPS) * sub_ref[...]) * (1.0 - lambda_init)).astype(o_ref.dtype)


def _diff_attention(qm, k2, va, lam_params, subln, lambda_init):
    batch, n_heads, seq, _ = k2.shape
    dv = va.shape[3] // 2
    t = ATTN_TILE
    nq = seq // t
    qt, kt = _tri_tables(nq)
    lamp = jnp.zeros((8, LANES), F32).at[:4, :HEAD_DIM].set(lam_params)
    sub = subln.reshape(1, dv)
    return pl.pallas_call(
        functools.partial(_diff_attn_kernel, lambda_init),
        grid_spec=pltpu.PrefetchScalarGridSpec(
            num_scalar_prefetch=2,
            grid=(batch, n_heads, int(qt.shape[0])),
            in_specs=[pl.BlockSpec((1, 1, 2, t, LANES), lambda b, h, p, qt, kt: (b, h, 0, qt[p], 0)),
                      pl.BlockSpec((1, 1, t, LANES), lambda b, h, p, qt, kt: (b, h, kt[p], 0)),
                      pl.BlockSpec((1, 1, t, 2 * dv), lambda b, h, p, qt, kt: (b, h, kt[p], 0)),
                      pl.BlockSpec((8, LANES), lambda b, h, p, qt, kt: (0, 0)),
                      pl.BlockSpec((1, dv), lambda b, h, p, qt, kt: (0, 0))],
            out_specs=pl.BlockSpec((1, t, dv), lambda b, h, p, qt, kt: (b, qt[p], h)),
            scratch_shapes=[pltpu.VMEM((2, t, 1), F32), pltpu.VMEM((2, t, 2 * dv), F32)]),
        out_shape=jax.ShapeDtypeStruct((batch, seq, n_heads * dv), BF16),
        compiler_params=pltpu.CompilerParams(
            dimension_semantics=("parallel", "parallel", "arbitrary"), vmem_limit_bytes=VMEM_LIMIT),
    )(qt, kt, qm, k2, va, lamp, sub)


def _router_kernel(h_ref, g_ref, wr_ref, xn_ref, route_ref):
    x = h_ref[...]
    tm = x.shape[0]
    xn = x * lax.rsqrt(jnp.mean(x * x, axis=-1, keepdims=True) + EPS) * g_ref[...]
    xn_ref[...] = xn
    logits = jnp.dot(xn, wr_ref[...], precision=lax.Precision.HIGHEST, preferred_element_type=F32)
    lane = lax.broadcasted_iota(jnp.int32, (tm, LANES), 1)
    neg = jnp.full((tm, LANES), -jnp.inf, F32)
    big = jnp.full((tm, LANES), LANES, jnp.int32)

    def top1(vals):
        m = jnp.max(vals, axis=-1, keepdims=True)
        idx = jnp.min(jnp.where(vals == m, lane, big), axis=-1, keepdims=True)
        return m, idx

    grp_logits = jnp.where(lane < N_GROUPS, logits, neg)
    mg, gidx = top1(grp_logits)
    p_g = 1.0 / jnp.sum(jnp.exp(grp_logits - mg), axis=-1, keepdims=True)
    e_lane = lane - N_GROUPS
    in_grp = (e_lane >= gidx * EXPERTS_PER_GROUP) & (e_lane < (gidx + 1) * EXPERTS_PER_GROUP)
    sel = jnp.where(in_grp, logits, neg)
    m1, i1 = top1(sel)
    m2, i2 = top1(jnp.where(lane == i1, neg, sel))
    r = jnp.exp(m2 - m1)
    w1 = p_g / (1.0 + r)
    w2 = p_g * r / (1.0 + r)
    zero = jnp.zeros((tm, LANES), F32)
    route_ref[...] = jnp.where(lane == 0, (i1 - N_GROUPS).astype(F32),
                     jnp.where(lane == 1, (i2 - N_GROUPS).astype(F32),
                     jnp.where(lane == 2, w1, jnp.where(lane == 3, w2, zero))))


def _router(h, gain, w_group, w_expert):
    n, d = h.shape
    tm = ROUTER_ROWS
    wr = jnp.zeros((d, LANES), F32)
    wr = wr.at[:, :N_GROUPS].set(w_group)
    wr = wr.at[:, N_GROUPS:N_GROUPS + N_EXPERTS].set(
        jnp.transpose(w_expert, (1, 0, 2)).reshape(d, N_EXPERTS))
    return pl.pallas_call(
        _router_kernel,
        grid=(n // tm,),
        in_specs=[pl.BlockSpec((tm, d), lambda i: (i, 0)),
                  pl.BlockSpec((1, d), lambda i: (0, 0)),
                  pl.BlockSpec((d, LANES), lambda i: (0, 0))],
        out_specs=[pl.BlockSpec((tm, d), lambda i: (i, 0)),
                   pl.BlockSpec((tm, LANES), lambda i: (i, 0))],
        out_shape=[jax.ShapeDtypeStruct((n, d), F32), jax.ShapeDtypeStruct((n, LANES), F32)],
        compiler_params=pltpu.CompilerParams(
            dimension_semantics=("parallel",), vmem_limit_bytes=VMEM_LIMIT),
    )(h, gain.reshape(1, d), wr)


def _dispatch_tables(expert_ids, tm):
    n = expert_ids.shape[0]
    pairs = expert_ids.reshape(-1)
    n_pairs = pairs.shape[0]
    n_tiles = n_pairs // tm + N_EXPERTS
    onehot = (pairs[:, None] == jnp.arange(N_EXPERTS, dtype=jnp.int32)[None, :]).astype(jnp.int32)
    csum = jnp.cumsum(onehot, axis=0)
    rank = jnp.sum((csum - onehot) * onehot, axis=1)
    counts = csum[-1]
    padded = ((counts + tm - 1) // tm) * tm
    ends = jnp.cumsum(padded)
    starts = ends - padded
    pos = jnp.sum(onehot * starts[None, :], axis=1) + rank
    slot_token = jnp.zeros((n_tiles * tm,), jnp.int32).at[pos].set(
        jnp.arange(n_pairs, dtype=jnp.int32) // 2)
    tile_start = jnp.arange(n_tiles, dtype=jnp.int32) * tm
    tile_expert = jnp.minimum(jnp.sum(tile_start[:, None] >= ends[None, :], axis=1),
                              N_EXPERTS - 1).astype(jnp.int32)
    n_valid = (ends[-1] // tm).astype(jnp.int32).reshape(1)
    return slot_token.reshape(n_tiles, 1, tm), tile_expert, n_valid, pos.reshape(n, 2).astype(jnp.int32)


def _gather_rows(src_hbm, idx_ref, dst_ref, sem, n_rows):
    def body(r, carry):
        tok = idx_ref[0, 0, r]
        pltpu.make_async_copy(src_hbm.at[pl.ds(tok, 1)], dst_ref.at[pl.ds(r, 1)], sem).start()
        return carry
    lax.fori_loop(0, n_rows, body, 0, unroll=8)


def _wait_rows(src_hbm, dst_ref, sem, n_rows):
    pltpu.make_async_copy(src_hbm.at[pl.ds(0, n_rows)], dst_ref, sem).wait()


def _moe_kernel(te_ref, nv_ref, tok_ref, tok_next_ref, wg_ref, wu_ref, wd_ref, x_hbm,
                o_ref, xbuf, wg_b, wu_b, wd_b, sems):
    i = pl.program_id(0)
    tm = xbuf.shape[1]
    n_valid = nv_ref[0]
    slot = i % 2

    @pl.when(i == 0)
    def _():
        _gather_rows(x_hbm, tok_ref, xbuf.at[0], sems.at[0], tm)

    @pl.when(i + 1 < n_valid)
    def _():
        _gather_rows(x_hbm, tok_next_ref, xbuf.at[1 - slot], sems.at[1 - slot], tm)

    new_expert = jnp.logical_or(i == 0, te_ref[i] != te_ref[jnp.maximum(i - 1, 0)])

    @pl.when(jnp.logical_and(i < n_valid, new_expert))
    def _():
        wg_b[...] = wg_ref[0].astype(BF16)
        wu_b[...] = wu_ref[0].astype(BF16)
        wd_b[...] = wd_ref[0].astype(BF16)

    @pl.when(i < n_valid)
    def _():
        _wait_rows(x_hbm, xbuf.at[slot], sems.at[slot], tm)
        x = xbuf[slot].astype(BF16)
        g = jnp.dot(x, wg_b[...], preferred_element_type=F32)
        u = jnp.dot(x, wu_b[...], preferred_element_type=F32)
        hid = (g * _sigmoid(g) * u).astype(BF16)
        o_ref[...] = jnp.dot(hid, wd_b[...], preferred_element_type=F32)

    @pl.when(i >= n_valid)
    def _():
        o_ref[...] = jnp.zeros_like(o_ref)


def _moe_experts(xn, slot_token, tile_expert, n_valid, w_gate, w_up, w_down):
    n, d = xn.shape
    n_exp, _, f = w_gate.shape
    n_tiles, _, tm = slot_token.shape
    last = n_tiles - 1
    return pl.pallas_call(
        _moe_kernel,
        grid_spec=pltpu.PrefetchScalarGridSpec(
            num_scalar_prefetch=2,
            grid=(n_tiles,),
            in_specs=[
                pl.BlockSpec((1, 1, tm), lambda i, te, nv: (i, 0, 0), memory_space=pltpu.SMEM),
                pl.BlockSpec((1, 1, tm), lambda i, te, nv: (jnp.minimum(i + 1, last), 0, 0),
                             memory_space=pltpu.SMEM),
                pl.BlockSpec((1, d, f), lambda i, te, nv: (te[i], 0, 0)),
                pl.BlockSpec((1, d, f), lambda i, te, nv: (te[i], 0, 0)),
                pl.BlockSpec((1, f, d), lambda i, te, nv: (te[i], 0, 0)),
                pl.BlockSpec(memory_space=pl.ANY)],
            out_specs=pl.BlockSpec((tm, d), lambda i, te, nv: (i, 0)),
            scratch_shapes=[pltpu.VMEM((2, tm, d), F32),
                            pltpu.VMEM((d, f), BF16), pltpu.VMEM((d, f), BF16), pltpu.VMEM((f, d), BF16),
                            pltpu.SemaphoreType.DMA((2,))]),
        out_shape=jax.ShapeDtypeStruct((n_tiles * tm, d), F32),
        compiler_params=pltpu.CompilerParams(
            dimension_semantics=("arbitrary",), vmem_limit_bytes=VMEM_LIMIT),
    )(tile_expert, n_valid, slot_token, slot_token, w_gate, w_up, w_down, xn)


def _combine_kernel(pos_ref, pos_next_ref, h_ref, route_ref, y_hbm, o_ref, ybuf, sems):
    i = pl.program_id(0)
    n_steps = pl.num_programs(0)
    rows = ybuf.shape[1]
    slot = i % 2

    @pl.when(i == 0)
    def _():
        _gather_rows(y_hbm, pos_ref, ybuf.at[0], sems.at[0], rows)

    @pl.when(i + 1 < n_steps)
    def _():
        _gather_rows(y_hbm, pos_next_ref, ybuf.at[1 - slot], sems.at[1 - slot], rows)

    _wait_rows(y_hbm, ybuf.at[slot], sems.at[slot], rows)
    tc = rows // 2
    w = route_ref[...]
    o_ref[...] = (h_ref[...] + w[:, 2:3] * ybuf[slot, pl.ds(0, tc), :]
                  + w[:, 3:4] * ybuf[slot, pl.ds(tc, tc), :])


def _moe_combine(h, route, pos, y_sorted):
    n, d = h.shape
    tc = COMBINE_ROWS
    steps = n // tc
    pos_tab = jnp.transpose(pos.reshape(steps, tc, 2), (0, 2, 1)).reshape(steps, 1, 2 * tc)
    return pl.pallas_call(
        _combine_kernel,
        grid=(steps,),
        in_specs=[pl.BlockSpec((1, 1, 2 * tc), lambda i: (i, 0, 0), memory_space=pltpu.SMEM),
                  pl.BlockSpec((1, 1, 2 * tc), lambda i: (jnp.minimum(i + 1, steps - 1), 0, 0),
                               memory_space=pltpu.SMEM),
                  pl.BlockSpec((tc, d), lambda i: (i, 0)),
                  pl.BlockSpec((tc, LANES), lambda i: (i, 0)),
                  pl.BlockSpec(memory_space=pl.ANY)],
        out_specs=pl.BlockSpec((tc, d), lambda i: (i, 0)),
        out_shape=jax.ShapeDtypeStruct((n, d), F32),
        scratch_shapes=[pltpu.VMEM((2, 2 * tc, d), F32), pltpu.SemaphoreType.DMA((2,))],
        compiler_params=pltpu.CompilerParams(
            dimension_semantics=("arbitrary",), vmem_limit_bytes=VMEM_LIMIT),
    )(pos_tab, pos_tab, h, route, y_sorted)


def _moe_layer(h, gain, w_group, w_expert, w_gate, w_up, w_down):
    d = h.shape[1]
    f = w_gate.shape[-1]
    xn, route = _router(h, gain, w_group, w_expert)
    expert_ids = route[:, :2].astype(jnp.int32)
    slot_token, tile_expert, n_valid, pos = _dispatch_tables(expert_ids, MOE_TM)
    y_sorted = _moe_experts(xn, slot_token, tile_expert, n_valid,
                            w_gate.reshape(N_EXPERTS, d, f), w_up.reshape(N_EXPERTS, d, f),
                            w_down.reshape(N_EXPERTS, f, d))
    return _moe_combine(h, route, pos, y_sorted)


def _even_layer(h, batch, seq, gain, w_in, w_out, lb, f_bias, out_norm, q_norm, k_norm):
    d = h.shape[1]
    n_main = w_in.shape[1] - f_bias.shape[0]
    w_main = w_in[:, :n_main].astype(BF16)
    w_gate = jnp.zeros((d, LANES), F32).at[:, :f_bias.shape[0]].set(w_in[:, n_main:])
    proj, gates = _norm_proj(h, gain, w_main, w_gate)
    o_a = _hgrn2(proj, lb, out_norm, batch, seq)
    qa, ka, va = _fox_prep(proj, gates, f_bias, q_norm, k_norm, batch, seq, col0=4)
    o_b = _fox_attention(qa, ka, va).reshape(batch * seq, -1)
    wo = w_out.astype(BF16)
    ka_dim = o_a.shape[1]
    return _proj_residual([(o_a, wo[:ka_dim]), (o_b, wo[ka_dim:])], h)


def _odd_layer(h, positions, batch, seq, gain, w_in, w_out, q_norm, k_norm, lam_params, subln, lambda_init):
    proj = _norm_proj(h, gain, w_in.astype(BF16))
    qm, k2, va = _diff_prep(proj, positions, q_norm, k_norm, batch, seq)
    o = _diff_attention(qm, k2, va, lam_params, subln, lambda_init).reshape(batch * seq, -1)
    return _proj_residual([(o, w_out.astype(BF16))], h)


def kernel(x, positions, hgrn_lb_logits, norm_mix, norm_ffn, even_w_in, even_w_out, fox_f_bias,
           hgrn_out_norm, fox_q_norm, fox_k_norm, odd_w_in, odd_w_out, diff_q_norm, diff_k_norm,
           diff_lambda_q1, diff_lambda_k1, diff_lambda_q2, diff_lambda_k2, diff_subln,
           moe_router_group, moe_router_expert, moe_w_gate, moe_w_up, moe_w_down):
    batch, seq, d = x.shape
    depth = norm_mix.shape[0]
    lower_bounds = jnp.cumsum(jax.nn.softmax(hgrn_lb_logits.astype(F32), axis=0), axis=0)
    h = x.reshape(batch * seq, d)
    for layer in range(depth):
        j = layer // 2
        if layer % 2 == 0:
            h = _even_layer(h, batch, seq, norm_mix[layer], even_w_in[j], even_w_out[j], lower_bounds[j],
                            fox_f_bias[j], hgrn_out_norm[j], fox_q_norm[j], fox_k_norm[j])
        else:
            lambda_init = 0.8 - 0.6 * math.exp(-0.3 * layer)
            lam_params = jnp.stack([diff_lambda_q1[j], diff_lambda_k1[j],
                                    diff_lambda_q2[j], diff_lambda_k2[j]]).astype(F32)
            h = _odd_layer(h, positions, batch, seq, norm_mix[layer], odd_w_in[j], odd_w_out[j],
                           diff_q_norm[j], diff_k_norm[j], lam_params, diff_subln[j], lambda_init)
        h = _moe_layer(h, norm_ffn[layer], moe_router_group[layer], moe_router_expert[layer],
                       moe_w_gate[layer], moe_w_up[layer], moe_w_down[layer])
    return h.reshape(batch, seq, d)
```

```python
import functools
import math

import numpy as np
import jax
import jax.numpy as jnp
from jax import lax
from jax.experimental import pallas as pl
from jax.experimental.pallas import tpu as pltpu

F32 = jnp.float32
BF16 = jnp.bfloat16

EPS = 1e-6
ROPE_THETA = 10000.0
CHUNK = 64
HEAD_DIM = 64
N_GROUPS = 4
EXPERTS_PER_GROUP = 8
N_EXPERTS = N_GROUPS * EXPERTS_PER_GROUP
LANES = 128

HGRN_CHUNK = 64
HGRN_ROWS = 256
ATTN_TILE = 512
ATTN_FAST_TILE = 1024
ATTN_FAST_SUB = 512
LOGIT_BOUND_MAX = 60.0
LOG2E = math.log2(math.e)
PREP_ROWS = 256
PROJ_TM = 256
ROUTER_ROWS = 256
MOE_TM = 256
COMBINE_ROWS = 256
VMEM_LIMIT = 56 * 1024 * 1024


def _split3(x):
    hi = x.astype(BF16)
    r1 = x - hi.astype(F32)
    mid = r1.astype(BF16)
    lo = (r1 - mid.astype(F32)).astype(BF16)
    return hi, mid, lo


def _dot3(const_bf16, x):
    hi, mid, lo = _split3(x)
    d = lambda b: jnp.dot(const_bf16, b, preferred_element_type=F32)
    return d(hi) + d(mid) + d(lo)


def _dot3_stacked(const3_bf16, x):
    return jnp.dot(const3_bf16, jnp.concatenate(_split3(x), axis=0), preferred_element_type=F32)


def _dot3_rhs(x, const_bf16):
    hi, mid, lo = _split3(x)
    d = lambda a: jnp.dot(a, const_bf16, preferred_element_type=F32)
    return d(hi) + d(mid) + d(lo)


def _dot_nt(a, b):
    return lax.dot_general(a, b, (((1,), (1,)), ((), ())), preferred_element_type=F32)


def _dot_tn(a, b):
    return lax.dot_general(a, b, (((0,), (0,)), ((), ())), preferred_element_type=F32)


def _sigmoid(x):
    return 1.0 / (1.0 + jnp.exp(-x))


def _norm_proj_kernel(has_aux, x_ref, g_ref, w_ref, *rest):
    x = x_ref[...]
    ms = jnp.mean(x * x, axis=-1, keepdims=True)
    xn = x * lax.rsqrt(ms + EPS) * g_ref[...]
    xb = xn.astype(BF16)
    rest[-2 if has_aux else -1][...] = jnp.dot(xb, w_ref[...], preferred_element_type=F32).astype(BF16)
    if has_aux:
        whi_ref, wlo_ref, _, oaux_ref = rest
        xl = (xn - xb.astype(F32)).astype(BF16)
        d = lambda a, b: jnp.dot(a, b[...], preferred_element_type=F32)
        oaux_ref[...] = d(xb, whi_ref) + (d(xl, whi_ref) + d(xb, wlo_ref))


def _norm_proj(x, gain, w, w_aux=None):
    n, d = x.shape
    m = w.shape[1]
    tm = PROJ_TM
    has_aux = w_aux is not None
    in_specs = [pl.BlockSpec((tm, d), lambda i: (i, 0)),
                pl.BlockSpec((1, d), lambda i: (0, 0)),
                pl.BlockSpec((d, m), lambda i: (0, 0))]
    out_specs = [pl.BlockSpec((tm, m), lambda i: (i, 0))]
    out_shape = [jax.ShapeDtypeStruct((n, m), BF16)]
    args = [x, gain.reshape(1, d), w]
    if has_aux:
        w_hi = w_aux.astype(BF16)
        w_lo = (w_aux - w_hi.astype(F32)).astype(BF16)
        in_specs += [pl.BlockSpec((d, LANES), lambda i: (0, 0))] * 2
        out_specs.append(pl.BlockSpec((tm, LANES), lambda i: (i, 0)))
        out_shape.append(jax.ShapeDtypeStruct((n, LANES), F32))
        args += [w_hi, w_lo]
    res = pl.pallas_call(
        functools.partial(_norm_proj_kernel, has_aux),
        grid=(n // tm,),
        in_specs=in_specs, out_specs=out_specs, out_shape=out_shape,
        compiler_params=pltpu.CompilerParams(
            dimension_semantics=("parallel",), vmem_limit_bytes=VMEM_LIMIT),
    )(*args)
    return res if has_aux else res[0]


def _proj_res_kernel(n_in, *refs):
    h_ref = refs[2 * n_in]
    o_ref = refs[2 * n_in + 1]
    acc = h_ref[...]
    for t in range(n_in):
        acc = acc + jnp.dot(refs[2 * t][...], refs[2 * t + 1][...], preferred_element_type=F32)
    o_ref[...] = acc


def _proj_residual(pairs, h):
    n, d = h.shape
    tm = PROJ_TM
    in_specs, args = [], []
    for a, w in pairs:
        k = a.shape[1]
        in_specs += [pl.BlockSpec((tm, k), lambda i: (i, 0)),
                     pl.BlockSpec((k, d), lambda i: (0, 0))]
        args += [a, w]
    in_specs.append(pl.BlockSpec((tm, d), lambda i: (i, 0)))
    args.append(h)
    return pl.pallas_call(
        functools.partial(_proj_res_kernel, len(pairs)),
        grid=(n // tm,),
        in_specs=in_specs,
        out_specs=pl.BlockSpec((tm, d), lambda i: (i, 0)),
        out_shape=jax.ShapeDtypeStruct((n, d), F32),
        compiler_params=pltpu.CompilerParams(
            dimension_semantics=("parallel",), vmem_limit_bytes=VMEM_LIMIT),
    )(*args)


_HGRN_LEVELS = (64, 32, 16)
_HGRN_DIAG = 8


def _hgrn_constants():
    c = HGRN_CHUNK
    idx = np.arange(c)
    low = (idx[None, :] <= idx[:, None]).astype(np.float64)

    def ref_rows(r):
        return (idx[None, :] <= r[:, None]).astype(np.float64)

    blocks = [low, ref_rows(np.full(c, c - 1)) - low]
    masks = []
    for b in _HGRN_LEVELS:
        start = (idx // b) * b
        upper = (idx - start) >= b // 2
        ref = start + b // 2 - 1
        blocks.append(low - ref_rows(np.where(upper, ref, idx)))
        blocks.append(ref_rows(np.where(upper, idx, ref)) - low)
        same = (idx[:, None] // b) == (idx[None, :] // b)
        masks.append(same & upper[:, None] & ~upper[None, :])
    ref = (idx // _HGRN_DIAG) * _HGRN_DIAG + _HGRN_DIAG // 2 - 1
    blocks.append(low - ref_rows(ref))
    blocks.append(ref_rows(ref) - low)
    same = (idx[:, None] // _HGRN_DIAG) == (idx[None, :] // _HGRN_DIAG)
    masks.append(same & (idx[None, :] <= idx[:, None]))
    dst = np.concatenate(blocks, axis=0)
    return dst.astype(np.float32), np.stack(masks).astype(np.float32)


def _hgrn_kernel(q_ref, f_ref, i_ref, g_ref, lb_ref, gn_ref, dst_ref, mask_ref, o_ref, st_ref):
    c = HGRN_CHUNK
    n_heads = st_ref.shape[0]
    n_lvl = mask_ref.shape[0]

    @pl.when(pl.program_id(1) == 0)
    def _():
        st_ref[...] = jnp.zeros_like(st_ref)

    lb = lb_ref[...]
    gn = gn_ref[...]
    dst = dst_ref[...]
    for ch in range(q_ref.shape[0] // c):
        rows = pl.ds(ch * c, c)
        q = q_ref[rows, :].astype(F32)
        qf = q * _sigmoid(q)
        f = lb + (1.0 - lb) * _sigmoid(f_ref[rows, :].astype(F32))
        kk = 1.0 - f
        ex = jnp.exp(_dot3_stacked(dst, jnp.log(f)))
        v = i_ref[rows, :].astype(BF16)
        g = g_ref[rows, :].astype(F32)
        gate = g * _sigmoid(g)
        q_in = (qf * ex[0:c]).astype(BF16)
        k_st = (kk * ex[c:2 * c]).astype(BF16)
        dec = ex[c - 1:c]
        q_l = [(qf * ex[(2 + 2 * l) * c:(3 + 2 * l) * c]).astype(BF16) for l in range(n_lvl)]
        k_l = [(kk * ex[(3 + 2 * l) * c:(4 + 2 * l) * c]).astype(BF16) for l in range(n_lvl)]
        outs = []
        for h in range(n_heads):
            hs = slice(h * HEAD_DIM, (h + 1) * HEAD_DIM)
            scores = mask_ref[0] * _dot_nt(q_l[0][:, hs], k_l[0][:, hs])
            for l in range(1, n_lvl):
                scores = scores + mask_ref[l] * _dot_nt(q_l[l][:, hs], k_l[l][:, hs])
            st = st_ref[h]
            o = (jnp.dot(scores.astype(BF16), v[:, hs], preferred_element_type=F32)
                 + _dot_nt(q_in[:, hs], st.astype(BF16)))
            st_ref[h] = st * dec[:, hs] + _dot_tn(v[:, hs], k_st[:, hs])
            ms = jnp.mean(o * o, axis=-1, keepdims=True)
            outs.append(o * lax.rsqrt(ms + EPS) * gn)
        o_ref[rows, :] = (jnp.concatenate(outs, axis=-1) * gate).astype(o_ref.dtype)


def _hgrn2(proj, lb, out_norm, batch, seq):
    n = proj.shape[0]
    width = lb.shape[0]
    n_heads = width // HEAD_DIM
    rb = HGRN_ROWS
    spb = seq // rb
    dst, masks = _hgrn_constants()
    col = lambda j: pl.BlockSpec((rb, width), lambda b, s, j=j: (b * spb + s, j))
    full = lambda a: pl.BlockSpec(a.shape, lambda b, s: (0,) * a.ndim)
    lb2 = lb.reshape(1, width)
    gn = out_norm.reshape(1, HEAD_DIM)
    dst = jnp.asarray(np.concatenate([dst, dst, dst], axis=1), BF16)
    masks = jnp.asarray(masks, F32)
    return pl.pallas_call(
        _hgrn_kernel,
        grid=(batch, spb),
        in_specs=[col(0), col(1), col(2), col(3), full(lb2), full(gn), full(dst), full(masks)],
        out_specs=pl.BlockSpec((rb, width), lambda b, s: (b * spb + s, 0)),
        out_shape=jax.ShapeDtypeStruct((n, width), BF16),
        scratch_shapes=[pltpu.VMEM((n_heads, HEAD_DIM, HEAD_DIM), F32)],
        compiler_params=pltpu.CompilerParams(
            dimension_semantics=("parallel", "arbitrary"), vmem_limit_bytes=VMEM_LIMIT),
    )(proj, proj, proj, proj, lb2, gn, dst, masks)


def _fox_prep_kernel(q_ref, k_ref, v_ref, gate_ref, bias_ref, gq_ref, gk_ref, tril_ref,
                     qa_ref, ka_ref, va_ref, carry_ref):
    n_heads = qa_ref.shape[1]
    tm = q_ref.shape[0]

    @pl.when(pl.program_id(1) == 0)
    def _():
        carry_ref[...] = jnp.zeros_like(carry_ref)

    z = gate_ref[...] + bias_ref[...]
    ls = -(jnp.maximum(-z, 0.0) + jnp.log(1.0 + jnp.exp(-jnp.abs(z))))
    cum = _dot3(tril_ref[...], ls) + carry_ref[...]
    carry_ref[...] = cum[tm - 1:tm]
    cum = cum * LOG2E

    lane = lax.broadcasted_iota(jnp.int32, (tm, HEAD_DIM), 1)
    one = jnp.ones((tm, HEAD_DIM), F32)
    zero = jnp.zeros((tm, HEAD_DIM), F32)
    scale = HEAD_DIM ** -0.5 * LOG2E
    q_all = q_ref[...].astype(F32)
    k_all = k_ref[...].astype(F32)
    v_all = v_ref[...].astype(F32)
    for h in range(n_heads):
        hs = slice(h * HEAD_DIM, (h + 1) * HEAD_DIM)
        qh = q_all[:, hs]
        kh = k_all[:, hs]
        qn = qh * lax.rsqrt(jnp.mean(qh * qh, axis=-1, keepdims=True) + EPS) * gq_ref[...] * scale
        kn = kh * lax.rsqrt(jnp.mean(kh * kh, axis=-1, keepdims=True) + EPS) * gk_ref[...]
        c_hi, c_mid, c_lo = (t.astype(F32) for t in _split3(cum[:, h:h + 1]))
        exq = jnp.where(lane == 0, c_hi, jnp.where(lane == 1, c_mid, jnp.where(lane == 2, c_lo,
                        jnp.where(lane < 6, one, zero))))
        exk = jnp.where(lane < 3, one, jnp.where(lane == 3, -c_hi, jnp.where(lane == 4, -c_mid,
                        jnp.where(lane == 5, -c_lo, zero))))
        exv = jnp.where(lane == 0, one, zero)
        qa_ref[0, h] = jnp.concatenate([qn, exq], axis=-1).astype(BF16)
        ka_ref[0, h] = jnp.concatenate([kn, exk], axis=-1).astype(BF16)
        va_ref[0, h] = jnp.concatenate([v_all[:, hs], exv], axis=-1).astype(BF16)


def _fox_prep(proj, gates, f_bias, q_norm, k_norm, batch, seq, col0):
    width = 512
    n_heads = width // HEAD_DIM
    tm = PREP_ROWS
    spb = seq // tm
    col = lambda j: pl.BlockSpec((tm, width), lambda b, s, j=j: (b * spb + s, col0 + j))
    full = lambda a: pl.BlockSpec(a.shape, lambda b, s: (0,) * a.ndim)
    bias = jnp.zeros((1, LANES), F32).at[0, :n_heads].set(f_bias)
    gq = q_norm.reshape(1, HEAD_DIM)
    gk = k_norm.reshape(1, HEAD_DIM)
    tril = jnp.asarray(np.tril(np.ones((tm, tm), np.float32)), BF16)
    out = jax.ShapeDtypeStruct((batch, n_heads, seq, LANES), BF16)
    ospec = pl.BlockSpec((1, n_heads, tm, LANES), lambda b, s: (b, 0, s, 0))
    return pl.pallas_call(
        _fox_prep_kernel,
        grid=(batch, spb),
        in_specs=[col(0), col(1), col(2),
                  pl.BlockSpec((tm, LANES), lambda b, s: (b * spb + s, 0)),
                  full(bias), full(gq), full(gk), full(tril)],
        out_specs=[ospec, ospec, ospec],
        out_shape=[out, out, out],
        scratch_shapes=[pltpu.VMEM((1, LANES), F32)],
        compiler_params=pltpu.CompilerParams(
            dimension_semantics=("parallel", "arbitrary"), vmem_limit_bytes=VMEM_LIMIT),
    )(proj, proj, proj, gates, bias, gq, gk, tril)


def _tri_tables(nq):
    qi = [q for q in range(nq) for _ in range(q + 1)]
    ki = [k for q in range(nq) for k in range(q + 1)]
    return jnp.asarray(qi, jnp.int32), jnp.asarray(ki, jnp.int32)


def _fox_attn_kernel(qt_ref, kt_ref, q_ref, k_ref, v_ref, o_ref, m_ref, acc_ref):
    p_idx = pl.program_id(2)
    qi = qt_ref[p_idx]
    ki = kt_ref[p_idx]
    hp = q_ref.shape[1]
    t = q_ref.shape[2]

    @pl.when(ki == 0)
    def _():
        m_ref[...] = jnp.full_like(m_ref, -jnp.inf)
        acc_ref[...] = jnp.zeros_like(acc_ref)

    def step(masked):
        for h in range(hp):
            s = _dot_nt(q_ref[0, h], k_ref[0, h])
            if masked:
                row = lax.broadcasted_iota(jnp.int32, (t, t), 0)
                colm = lax.broadcasted_iota(jnp.int32, (t, t), 1)
                s = jnp.where(colm <= row, s, -jnp.inf)
            m_old = m_ref[h]
            m_new = jnp.maximum(m_old, jnp.max(s, axis=-1, keepdims=True))
            p = jnp.exp2(s - m_new)
            acc_ref[h] = (jnp.exp2(m_old - m_new) * acc_ref[h]
                          + jnp.dot(p.astype(BF16), v_ref[0, h], preferred_element_type=F32))
            m_ref[h] = m_new

    @pl.when(ki < qi)
    def _():
        step(False)

    @pl.when(ki == qi)
    def _():
        step(True)
        outs = []
        for h in range(hp):
            acc = acc_ref[h]
            outs.append(acc[:, :HEAD_DIM] / acc[:, HEAD_DIM:HEAD_DIM + 1])
        o_ref[0] = jnp.concatenate(outs, axis=-1).astype(o_ref.dtype)


def _tile_plan(n_sub, diagonal):
    plan = []
    for qb in range(n_sub):
        if not diagonal:
            plan.append((qb, 0, n_sub, False))
        else:
            if qb > 0:
                plan.append((qb, 0, qb, False))
            plan.append((qb, qb, qb + 1, True))
    return plan


def _fox_fast_kernel(qt_ref, kt_ref, q_ref, k_ref, v_ref, o_ref, acc_ref):
    p_idx = pl.program_id(2)
    qi = qt_ref[p_idx]
    ki = kt_ref[p_idx]
    hp = q_ref.shape[1]
    sb = ATTN_FAST_SUB
    n_sub = q_ref.shape[2] // sb

    @pl.when(ki == 0)
    def _():
        acc_ref[...] = jnp.zeros_like(acc_ref)

    def tile(diagonal):
        for h in range(hp):
            for qb, k0, k1, masked in _tile_plan(n_sub, diagonal):
                rows = pl.ds(qb * sb, sb)
                cols = pl.ds(k0 * sb, (k1 - k0) * sb)
                s = _dot_nt(q_ref[0, h, rows, :], k_ref[0, h, cols, :])
                if masked:
                    row = lax.broadcasted_iota(jnp.int32, (sb, sb), 0)
                    colm = lax.broadcasted_iota(jnp.int32, (sb, sb), 1)
                    s = jnp.where(colm <= row, s, -jnp.inf)
                p = jnp.exp2(s).astype(BF16)
                acc_ref[h, rows, :] += jnp.dot(p, v_ref[0, h, cols, :], preferred_element_type=F32)

    @pl.when(ki < qi)
    def _():
        tile(False)

    @pl.when(ki == qi)
    def _():
        tile(True)
        outs = []
        for h in range(hp):
            acc = acc_ref[h]
            outs.append(acc[:, :HEAD_DIM] / acc[:, HEAD_DIM:HEAD_DIM + 1])
        o_ref[0] = jnp.concatenate(outs, axis=-1).astype(o_ref.dtype)


def _fox_attention(qa, ka, va, fast):
    batch, n_heads, seq, _ = qa.shape
    t = ATTN_FAST_TILE if fast else ATTN_TILE
    hp = 2
    nq = seq // t
    qt, kt = _tri_tables(nq)
    qspec = pl.BlockSpec((1, hp, t, LANES), lambda b, g, p, qt, kt: (b, g, qt[p], 0))
    kspec = pl.BlockSpec((1, hp, t, LANES), lambda b, g, p, qt, kt: (b, g, kt[p], 0))
    scratch = [pltpu.VMEM((hp, t, LANES), F32)]
    if not fast:
        scratch = [pltpu.VMEM((hp, t, 1), F32)] + scratch
    return pl.pallas_call(
        _fox_fast_kernel if fast else _fox_attn_kernel,
        grid_spec=pltpu.PrefetchScalarGridSpec(
            num_scalar_prefetch=2,
            grid=(batch, n_heads // hp, int(qt.shape[0])),
            in_specs=[qspec, kspec, kspec],
            out_specs=pl.BlockSpec((1, t, hp * HEAD_DIM), lambda b, g, p, qt, kt: (b, qt[p], g)),
            scratch_shapes=scratch),
        out_shape=jax.ShapeDtypeStruct((batch, seq, n_heads * HEAD_DIM), BF16),
        compiler_params=pltpu.CompilerParams(
            dimension_semantics=("parallel", "parallel", "arbitrary"), vmem_limit_bytes=VMEM_LIMIT),
    )(qt, kt, qa, ka, va)


def _logit_bound(q_gain, k_gain):
    return HEAD_DIM ** 0.5 * jnp.max(jnp.abs(q_gain)) * jnp.max(jnp.abs(k_gain))


def _diff_prep_kernel(q_ref, k_ref, v_ref, pos_ref, invf_ref, gq_ref, gk_ref, grp_ref,
                      qm_ref, k2_ref, va_ref):
    n_heads = k2_ref.shape[1]
    tm = q_ref.shape[0]
    ang = pos_ref[...].astype(F32) * invf_ref[...]
    lane = lax.broadcasted_iota(jnp.int32, (tm, LANES), 1)
    first = (lane % HEAD_DIM) < (HEAD_DIM // 2)
    cs = jnp.cos(ang)
    sn = jnp.sin(ang)
    sn = jnp.where(first, -sn, sn)
    grp = grp_ref[...]
    scale = HEAD_DIM ** -0.5 * LOG2E
    zero = jnp.zeros((tm, LANES), F32)
    onecol = jnp.where(lane == 0, 1.0, 0.0).astype(BF16)

    def norm_rope(x, gain):
        ms = _dot3_rhs(x * x, grp)
        y = x * lax.rsqrt(ms + EPS) * gain
        yr = jnp.where(first, pltpu.roll(y, LANES - HEAD_DIM // 2, 1), pltpu.roll(y, HEAD_DIM // 2, 1))
        return y * cs + yr * sn

    for h in range(n_heads):
        cols = slice(h * LANES, (h + 1) * LANES)
        qr = norm_rope(q_ref[:, cols].astype(F32), gq_ref[...]) * scale
        kr = norm_rope(k_ref[:, cols].astype(F32), gk_ref[...])
        qm_ref[0, h, 0] = jnp.where(lane < HEAD_DIM, qr, zero).astype(BF16)
        qm_ref[0, h, 1] = jnp.where(lane < HEAD_DIM, zero, qr).astype(BF16)
        k2_ref[0, h] = kr.astype(BF16)
        va_ref[0, h] = jnp.concatenate([v_ref[:, cols].astype(BF16), onecol], axis=-1)


def _diff_prep(proj, positions, q_norm, k_norm, batch, seq):
    n = proj.shape[0]
    width = proj.shape[1] // 3
    n_heads = width // LANES
    tm = PREP_ROWS
    spb = seq // tm
    col = lambda j: pl.BlockSpec((tm, width), lambda b, s, j=j: (b * spb + s, j))
    full = lambda a: pl.BlockSpec(a.shape, lambda b, s: (0,) * a.ndim)
    half = HEAD_DIM // 2
    inv_freq = ROPE_THETA ** (-jnp.arange(half, dtype=F32) / half)
    invf = jnp.tile(inv_freq, LANES // half).reshape(1, LANES)
    gq = jnp.tile(q_norm, LANES // HEAD_DIM).reshape(1, LANES)
    gk = jnp.tile(k_norm, LANES // HEAD_DIM).reshape(1, LANES)
    lane = np.arange(LANES)
    grp = jnp.asarray(((lane[:, None] // HEAD_DIM) == (lane[None, :] // HEAD_DIM)) / HEAD_DIM, BF16)
    pos = positions.reshape(n, 1).astype(jnp.int32)
    return pl.pallas_call(
        _diff_prep_kernel,
        grid=(batch, spb),
        in_specs=[col(0), col(1), col(2),
                  pl.BlockSpec((tm, 1), lambda b, s: (b * spb + s, 0)),
                  full(invf), full(gq), full(gk), full(grp)],
        out_specs=[pl.BlockSpec((1, n_heads, 2, tm, LANES), lambda b, s: (b, 0, 0, s, 0)),
                   pl.BlockSpec((1, n_heads, tm, LANES), lambda b, s: (b, 0, s, 0)),
                   pl.BlockSpec((1, n_heads, tm, 2 * LANES), lambda b, s: (b, 0, s, 0))],
        out_shape=[jax.ShapeDtypeStruct((batch, n_heads, 2, seq, LANES), BF16),
                   jax.ShapeDtypeStruct((batch, n_heads, seq, LANES), BF16),
                   jax.ShapeDtypeStruct((batch, n_heads, seq, 2 * LANES), BF16)],
        compiler_params=pltpu.CompilerParams(
            dimension_semantics=("parallel", "parallel"), vmem_limit_bytes=VMEM_LIMIT),
    )(proj, proj, proj, pos, invf, gq, gk, grp)


def _diff_attn_kernel(lambda_init, qt_ref, kt_ref, q_ref, k_ref, v_ref, lam_ref, sub_ref,
                      o_ref, m_ref, acc_ref):
    p_idx = pl.program_id(2)
    qi = qt_ref[p_idx]
    ki = kt_ref[p_idx]
    t = k_ref.shape[2]
    dv = o_ref.shape[2]

    @pl.when(ki == 0)
    def _():
        m_ref[...] = jnp.full_like(m_ref, -jnp.inf)
        acc_ref[...] = jnp.zeros_like(acc_ref)

    def step(masked):
        for m in range(2):
            s = _dot_nt(q_ref[0, 0, m], k_ref[0, 0])
            if masked:
                row = lax.broadcasted_iota(jnp.int32, (t, t), 0) // CHUNK
                colm = lax.broadcasted_iota(jnp.int32, (t, t), 1) // CHUNK
                s = jnp.where(colm <= row, s, -jnp.inf)
            m_old = m_ref[m]
            m_new = jnp.maximum(m_old, jnp.max(s, axis=-1, keepdims=True))
            p = jnp.exp2(s - m_new)
            acc_ref[m] = (jnp.exp2(m_old - m_new) * acc_ref[m]
                          + jnp.dot(p.astype(BF16), v_ref[0, 0], preferred_element_type=F32))
            m_ref[m] = m_new

    @pl.when(ki < qi)
    def _():
        step(False)

    @pl.when(ki == qi)
    def _():
        step(True)
        _diff_finalize(lambda_init, acc_ref, lam_ref, sub_ref, o_ref)


def _diff_finalize(lambda_init, acc_ref, lam_ref, sub_ref, o_ref):
    dv = o_ref.shape[2]
    lp = lam_ref[...]
    lam = (jnp.exp(jnp.sum(lp[0:1] * lp[1:2], axis=-1, keepdims=True))
           - jnp.exp(jnp.sum(lp[2:3] * lp[3:4], axis=-1, keepdims=True)) + lambda_init)
    a0 = acc_ref[0]
    a1 = acc_ref[1]
    o = a0[:, :dv] / a0[:, dv:dv + 1] - lam * (a1[:, :dv] / a1[:, dv:dv + 1])
    ms = jnp.mean(o * o, axis=-1, keepdims=True)
    o_ref[0] = ((o * lax.rsqrt(ms + EPS) * sub_ref[...]) * (1.0 - lambda_init)).astype(o_ref.dtype)


def _diff_fast_kernel(lambda_init, qt_ref, kt_ref, q_ref, k_ref, v_ref, lam_ref, sub_ref, o_ref, acc_ref):
    p_idx = pl.program_id(2)
    qi = qt_ref[p_idx]
    ki = kt_ref[p_idx]
    sb = ATTN_FAST_SUB
    n_sub = k_ref.shape[2] // sb

    @pl.when(ki == 0)
    def _():
        acc_ref[...] = jnp.zeros_like(acc_ref)

    def tile(diagonal):
        for m in range(2):
            for qb, k0, k1, masked in _tile_plan(n_sub, diagonal):
                rows = pl.ds(qb * sb, sb)
                cols = pl.ds(k0 * sb, (k1 - k0) * sb)
                s = _dot_nt(q_ref[0, 0, m, rows, :], k_ref[0, 0, cols, :])
                if masked:
                    row = lax.broadcasted_iota(jnp.int32, (sb, sb), 0) // CHUNK
                    colm = lax.broadcasted_iota(jnp.int32, (sb, sb), 1) // CHUNK
                    s = jnp.where(colm <= row, s, -jnp.inf)
                p = jnp.exp2(s).astype(BF16)
                acc_ref[m, rows, :] += jnp.dot(p, v_ref[0, 0, cols, :], preferred_element_type=F32)

    @pl.when(ki < qi)
    def _():
        tile(False)

    @pl.when(ki == qi)
    def _():
        tile(True)
        _diff_finalize(lambda_init, acc_ref, lam_ref, sub_ref, o_ref)


def _diff_attention(qm, k2, va, lam_params, subln, lambda_init, fast):
    batch, n_heads, seq, _ = k2.shape
    dv = va.shape[3] // 2
    t = ATTN_FAST_TILE if fast else ATTN_TILE
    nq = seq // t
    qt, kt = _tri_tables(nq)
    lamp = jnp.zeros((8, LANES), F32).at[:4, :HEAD_DIM].set(lam_params)
    sub = subln.reshape(1, dv)
    scratch = [pltpu.VMEM((2, t, 2 * dv), F32)]
    if not fast:
        scratch = [pltpu.VMEM((2, t, 1), F32)] + scratch
    return pl.pallas_call(
        functools.partial(_diff_fast_kernel if fast else _diff_attn_kernel, lambda_init),
        grid_spec=pltpu.PrefetchScalarGridSpec(
            num_scalar_prefetch=2,
            grid=(batch, n_heads, int(qt.shape[0])),
            in_specs=[pl.BlockSpec((1, 1, 2, t, LANES), lambda b, h, p, qt, kt: (b, h, 0, qt[p], 0)),
                      pl.BlockSpec((1, 1, t, LANES), lambda b, h, p, qt, kt: (b, h, kt[p], 0)),
                      pl.BlockSpec((1, 1, t, 2 * dv), lambda b, h, p, qt, kt: (b, h, kt[p], 0)),
                      pl.BlockSpec((8, LANES), lambda b, h, p, qt, kt: (0, 0)),
                      pl.BlockSpec((1, dv), lambda b, h, p, qt, kt: (0, 0))],
            out_specs=pl.BlockSpec((1, t, dv), lambda b, h, p, qt, kt: (b, qt[p], h)),
            scratch_shapes=scratch),
        out_shape=jax.ShapeDtypeStruct((batch, seq, n_heads * dv), BF16),
        compiler_params=pltpu.CompilerParams(
            dimension_semantics=("parallel", "parallel", "arbitrary"), vmem_limit_bytes=VMEM_LIMIT),
    )(qt, kt, qm, k2, va, lamp, sub)


def _router_kernel(h_ref, g_ref, wr_ref, xn_ref, route_ref):
    x = h_ref[...]
    tm = x.shape[0]
    xn = x * lax.rsqrt(jnp.mean(x * x, axis=-1, keepdims=True) + EPS) * g_ref[...]
    xn_ref[...] = xn
    logits = jnp.dot(xn, wr_ref[...], precision=lax.Precision.HIGHEST, preferred_element_type=F32)
    lane = lax.broadcasted_iota(jnp.int32, (tm, LANES), 1)
    neg = jnp.full((tm, LANES), -jnp.inf, F32)
    big = jnp.full((tm, LANES), LANES, jnp.int32)

    def top1(vals):
        m = jnp.max(vals, axis=-1, keepdims=True)
        idx = jnp.min(jnp.where(vals == m, lane, big), axis=-1, keepdims=True)
        return m, idx

    grp_logits = jnp.where(lane < N_GROUPS, logits, neg)
    mg, gidx = top1(grp_logits)
    p_g = 1.0 / jnp.sum(jnp.exp(grp_logits - mg), axis=-1, keepdims=True)
    e_lane = lane - N_GROUPS
    in_grp = (e_lane >= gidx * EXPERTS_PER_GROUP) & (e_lane < (gidx + 1) * EXPERTS_PER_GROUP)
    sel = jnp.where(in_grp, logits, neg)
    m1, i1 = top1(sel)
    m2, i2 = top1(jnp.where(lane == i1, neg, sel))
    r = jnp.exp(m2 - m1)
    w1 = p_g / (1.0 + r)
    w2 = p_g * r / (1.0 + r)
    zero = jnp.zeros((tm, LANES), F32)
    route_ref[...] = jnp.where(lane == 0, (i1 - N_GROUPS).astype(F32),
                     jnp.where(lane == 1, (i2 - N_GROUPS).astype(F32),
                     jnp.where(lane == 2, w1, jnp.where(lane == 3, w2, zero))))


def _router(h, gain, w_group, w_expert):
    n, d = h.shape
    tm = ROUTER_ROWS
    wr = jnp.zeros((d, LANES), F32)
    wr = wr.at[:, :N_GROUPS].set(w_group)
    wr = wr.at[:, N_GROUPS:N_GROUPS + N_EXPERTS].set(
        jnp.transpose(w_expert, (1, 0, 2)).reshape(d, N_EXPERTS))
    return pl.pallas_call(
        _router_kernel,
        grid=(n // tm,),
        in_specs=[pl.BlockSpec((tm, d), lambda i: (i, 0)),
                  pl.BlockSpec((1, d), lambda i: (0, 0)),
                  pl.BlockSpec((d, LANES), lambda i: (0, 0))],
        out_specs=[pl.BlockSpec((tm, d), lambda i: (i, 0)),
                   pl.BlockSpec((tm, LANES), lambda i: (i, 0))],
        out_shape=[jax.ShapeDtypeStruct((n, d), F32), jax.ShapeDtypeStruct((n, LANES), F32)],
        compiler_params=pltpu.CompilerParams(
            dimension_semantics=("parallel",), vmem_limit_bytes=VMEM_LIMIT),
    )(h, gain.reshape(1, d), wr)


def _dispatch_tables(expert_ids, tm):
    n = expert_ids.shape[0]
    pairs = expert_ids.reshape(-1)
    n_pairs = pairs.shape[0]
    n_tiles = n_pairs // tm + N_EXPERTS
    onehot = (pairs[:, None] == jnp.arange(N_EXPERTS, dtype=jnp.int32)[None, :]).astype(jnp.int32)
    csum = jnp.cumsum(onehot, axis=0)
    rank = jnp.sum((csum - onehot) * onehot, axis=1)
    counts = csum[-1]
    padded = ((counts + tm - 1) // tm) * tm
    ends = jnp.cumsum(padded)
    starts = ends - padded
    pos = jnp.sum(onehot * starts[None, :], axis=1) + rank
    slot_token = jnp.zeros((n_tiles * tm,), jnp.int32).at[pos].set(
        jnp.arange(n_pairs, dtype=jnp.int32) // 2)
    tile_start = jnp.arange(n_tiles, dtype=jnp.int32) * tm
    tile_expert = jnp.minimum(jnp.sum(tile_start[:, None] >= ends[None, :], axis=1),
                              N_EXPERTS - 1).astype(jnp.int32)
    n_valid = (ends[-1] // tm).astype(jnp.int32).reshape(1)
    return slot_token.reshape(n_tiles, 1, tm), tile_expert, n_valid, pos.reshape(n, 2).astype(jnp.int32)


def _gather_rows(src_hbm, idx_ref, dst_ref, sem, n_rows):
    def body(r, carry):
        tok = idx_ref[0, 0, r]
        pltpu.make_async_copy(src_hbm.at[pl.ds(tok, 1)], dst_ref.at[pl.ds(r, 1)], sem).start()
        return carry
    lax.fori_loop(0, n_rows, body, 0, unroll=8)


def _wait_rows(src_hbm, dst_ref, sem, n_rows):
    pltpu.make_async_copy(src_hbm.at[pl.ds(0, n_rows)], dst_ref, sem).wait()


def _moe_kernel(te_ref, nv_ref, tok_ref, tok_next_ref, wg_ref, wu_ref, wd_ref, x_hbm,
                o_ref, xbuf, wg_b, wu_b, wd_b, sems):
    i = pl.program_id(0)
    tm = xbuf.shape[1]
    n_valid = nv_ref[0]
    slot = i % 2

    @pl.when(i == 0)
    def _():
        _gather_rows(x_hbm, tok_ref, xbuf.at[0], sems.at[0], tm)

    @pl.when(i + 1 < n_valid)
    def _():
        _gather_rows(x_hbm, tok_next_ref, xbuf.at[1 - slot], sems.at[1 - slot], tm)

    new_expert = jnp.logical_or(i == 0, te_ref[i] != te_ref[jnp.maximum(i - 1, 0)])

    @pl.when(jnp.logical_and(i < n_valid, new_expert))
    def _():
        wg_b[...] = wg_ref[0].astype(BF16)
        wu_b[...] = wu_ref[0].astype(BF16)
        wd_b[...] = wd_ref[0].astype(BF16)

    @pl.when(i < n_valid)
    def _():
        _wait_rows(x_hbm, xbuf.at[slot], sems.at[slot], tm)
        x = xbuf[slot].astype(BF16)
        g = jnp.dot(x, wg_b[...], preferred_element_type=F32)
        u = jnp.dot(x, wu_b[...], preferred_element_type=F32)
        hid = (g * _sigmoid(g) * u).astype(BF16)
        o_ref[...] = jnp.dot(hid, wd_b[...], preferred_element_type=F32)

    @pl.when(i >= n_valid)
    def _():
        o_ref[...] = jnp.zeros_like(o_ref)


def _moe_experts(xn, slot_token, tile_expert, n_valid, w_gate, w_up, w_down):
    n, d = xn.shape
    n_exp, _, f = w_gate.shape
    n_tiles, _, tm = slot_token.shape
    last = n_tiles - 1
    return pl.pallas_call(
        _moe_kernel,
        grid_spec=pltpu.PrefetchScalarGridSpec(
            num_scalar_prefetch=2,
            grid=(n_tiles,),
            in_specs=[
                pl.BlockSpec((1, 1, tm), lambda i, te, nv: (i, 0, 0), memory_space=pltpu.SMEM),
                pl.BlockSpec((1, 1, tm), lambda i, te, nv: (jnp.minimum(i + 1, last), 0, 0),
                             memory_space=pltpu.SMEM),
                pl.BlockSpec((1, d, f), lambda i, te, nv: (te[i], 0, 0)),
                pl.BlockSpec((1, d, f), lambda i, te, nv: (te[i], 0, 0)),
                pl.BlockSpec((1, f, d), lambda i, te, nv: (te[i], 0, 0)),
                pl.BlockSpec(memory_space=pl.ANY)],
            out_specs=pl.BlockSpec((tm, d), lambda i, te, nv: (i, 0)),
            scratch_shapes=[pltpu.VMEM((2, tm, d), F32),
                            pltpu.VMEM((d, f), BF16), pltpu.VMEM((d, f), BF16), pltpu.VMEM((f, d), BF16),
                            pltpu.SemaphoreType.DMA((2,))]),
        out_shape=jax.ShapeDtypeStruct((n_tiles * tm, d), F32),
        compiler_params=pltpu.CompilerParams(
            dimension_semantics=("arbitrary",), vmem_limit_bytes=VMEM_LIMIT),
    )(tile_expert, n_valid, slot_token, slot_token, w_gate, w_up, w_down, xn)


def _combine_kernel(pos_ref, pos_next_ref, h_ref, route_ref, y_hbm, o_ref, ybuf, sems):
    i = pl.program_id(0)
    n_steps = pl.num_programs(0)
    rows = ybuf.shape[1]
    slot = i % 2

    @pl.when(i == 0)
    def _():
        _gather_rows(y_hbm, pos_ref, ybuf.at[0], sems.at[0], rows)

    @pl.when(i + 1 < n_steps)
    def _():
        _gather_rows(y_hbm, pos_next_ref, ybuf.at[1 - slot], sems.at[1 - slot], rows)

    _wait_rows(y_hbm, ybuf.at[slot], sems.at[slot], rows)
    tc = rows // 2
    w = route_ref[...]
    o_ref[...] = (h_ref[...] + w[:, 2:3] * ybuf[slot, pl.ds(0, tc), :]
                  + w[:, 3:4] * ybuf[slot, pl.ds(tc, tc), :])


def _moe_combine(h, route, pos, y_sorted):
    n, d = h.shape
    tc = COMBINE_ROWS
    steps = n // tc
    pos_tab = jnp.transpose(pos.reshape(steps, tc, 2), (0, 2, 1)).reshape(steps, 1, 2 * tc)
    return pl.pallas_call(
        _combine_kernel,
        grid=(steps,),
        in_specs=[pl.BlockSpec((1, 1, 2 * tc), lambda i: (i, 0, 0), memory_space=pltpu.SMEM),
                  pl.BlockSpec((1, 1, 2 * tc), lambda i: (jnp.minimum(i + 1, steps - 1), 0, 0),
                               memory_space=pltpu.SMEM),
                  pl.BlockSpec((tc, d), lambda i: (i, 0)),
                  pl.BlockSpec((tc, LANES), lambda i: (i, 0)),
                  pl.BlockSpec(memory_space=pl.ANY)],
        out_specs=pl.BlockSpec((tc, d), lambda i: (i, 0)),
        out_shape=jax.ShapeDtypeStruct((n, d), F32),
        scratch_shapes=[pltpu.VMEM((2, 2 * tc, d), F32), pltpu.SemaphoreType.DMA((2,))],
        compiler_params=pltpu.CompilerParams(
            dimension_semantics=("arbitrary",), vmem_limit_bytes=VMEM_LIMIT),
    )(pos_tab, pos_tab, h, route, y_sorted)


def _moe_layer(h, gain, w_group, w_expert, w_gate, w_up, w_down):
    d = h.shape[1]
    f = w_gate.shape[-1]
    xn, route = _router(h, gain, w_group, w_expert)
    expert_ids = route[:, :2].astype(jnp.int32)
    slot_token, tile_expert, n_valid, pos = _dispatch_tables(expert_ids, MOE_TM)
    y_sorted = _moe_experts(xn, slot_token, tile_expert, n_valid,
                            w_gate.reshape(N_EXPERTS, d, f), w_up.reshape(N_EXPERTS, d, f),
                            w_down.reshape(N_EXPERTS, f, d))
    return _moe_combine(h, route, pos, y_sorted)


def _even_layer(h, batch, seq, gain, w_in, w_out, lb, f_bias, out_norm, q_norm, k_norm):
    d = h.shape[1]
    n_main = w_in.shape[1] - f_bias.shape[0]
    w_main = w_in[:, :n_main].astype(BF16)
    w_gate = jnp.zeros((d, LANES), F32).at[:, :f_bias.shape[0]].set(w_in[:, n_main:])
    proj, gates = _norm_proj(h, gain, w_main, w_gate)
    o_a = _hgrn2(proj, lb, out_norm, batch, seq)
    qa, ka, va = _fox_prep(proj, gates, f_bias, q_norm, k_norm, batch, seq, col0=4)
    o_b = lax.cond(_logit_bound(q_norm, k_norm) <= LOGIT_BOUND_MAX,
                   functools.partial(_fox_attention, fast=True),
                   functools.partial(_fox_attention, fast=False), qa, ka, va).reshape(batch * seq, -1)
    wo = w_out.astype(BF16)
    ka_dim = o_a.shape[1]
    return _proj_residual([(o_a, wo[:ka_dim]), (o_b, wo[ka_dim:])], h)


def _odd_layer(h, positions, batch, seq, gain, w_in, w_out, q_norm, k_norm, lam_params, subln, lambda_init):
    proj = _norm_proj(h, gain, w_in.astype(BF16))
    qm, k2, va = _diff_prep(proj, positions, q_norm, k_norm, batch, seq)
    attn = lambda fast: functools.partial(_diff_attention, lam_params=lam_params, subln=subln,
                                          lambda_init=lambda_init, fast=fast)
    o = lax.cond(_logit_bound(q_norm, k_norm) <= LOGIT_BOUND_MAX,
                 attn(True), attn(False), qm, k2, va).reshape(batch * seq, -1)
    return _proj_residual([(o, w_out.astype(BF16))], h)


def kernel(x, positions, hgrn_lb_logits, norm_mix, norm_ffn, even_w_in, even_w_out, fox_f_bias,
           hgrn_out_norm, fox_q_norm, fox_k_norm, odd_w_in, odd_w_out, diff_q_norm, diff_k_norm,
           diff_lambda_q1, diff_lambda_k1, diff_lambda_q2, diff_lambda_k2, diff_subln,
           moe_router_group, moe_router_expert, moe_w_gate, moe_w_up, moe_w_down):
    batch, seq, d = x.shape
    depth = norm_mix.shape[0]
    lower_bounds = jnp.cumsum(jax.nn.softmax(hgrn_lb_logits.astype(F32), axis=0), axis=0)
    h = x.reshape(batch * seq, d)
    for layer in range(depth):
        j = layer // 2
        if layer % 2 == 0:
            h = _even_layer(h, batch, seq, norm_mix[layer], even_w_in[j], even_w_out[j], lower_bounds[j],
                            fox_f_bias[j], hgrn_out_norm[j], fox_q_norm[j], fox_k_norm[j])
        else:
            lambda_init = 0.8 - 0.6 * math.exp(-0.3 * layer)
            lam_params = jnp.stack([diff_lambda_q1[j], diff_lambda_k1[j],
                                    diff_lambda_q2[j], diff_lambda_k2[j]]).astype(F32)
            h = _odd_layer(h, positions, batch, seq, norm_mix[layer], odd_w_in[j], odd_w_out[j],
                           diff_q_norm[j], diff_k_norm[j], lam_params, diff_subln[j], lambda_init)
        h = _moe_layer(h, norm_ffn[layer], moe_router_group[layer], moe_router_expert[layer],
                       moe_w_gate[layer], moe_w_up[layer], moe_w_down[layer])
    return h.reshape(batch, seq, d)
```

```python
import functools
import math

import numpy as np
import jax
import jax.numpy as jnp
from jax import lax
from jax.experimental import pallas as pl
from jax.experimental.pallas import tpu as pltpu

F32 = jnp.float32
BF16 = jnp.bfloat16

EPS = 1e-6
ROPE_THETA = 10000.0
CHUNK = 64
HEAD_DIM = 64
N_GROUPS = 4
EXPERTS_PER_GROUP = 8
N_EXPERTS = N_GROUPS * EXPERTS_PER_GROUP
LANES = 128

HGRN_CHUNK = 64
HGRN_ROWS = 256
ATTN_TILE = 512
ATTN_FAST_TILE = 1024
ATTN_FAST_SUB = 512
LOGIT_BOUND_MAX = 60.0
LOG2E = math.log2(math.e)
PREP_ROWS = 256
PROJ_TM = 256
ROUTER_ROWS = 256
MOE_TM = 256
COMBINE_ROWS = 256
VMEM_LIMIT = 56 * 1024 * 1024


def _split3(x):
    hi = x.astype(BF16)
    r1 = x - hi.astype(F32)
    mid = r1.astype(BF16)
    lo = (r1 - mid.astype(F32)).astype(BF16)
    return hi, mid, lo


def _dot3(const_bf16, x):
    hi, mid, lo = _split3(x)
    d = lambda b: jnp.dot(const_bf16, b, preferred_element_type=F32)
    return d(hi) + d(mid) + d(lo)


def _dot3_stacked(const3_bf16, x):
    return jnp.dot(const3_bf16, jnp.concatenate(_split3(x), axis=0), preferred_element_type=F32)


def _dot2_rhs(x, const_bf16):
    hi = x.astype(BF16)
    lo = (x - hi.astype(F32)).astype(BF16)
    d = lambda a: jnp.dot(a, const_bf16, preferred_element_type=F32)
    return d(hi) + d(lo)


def _dot_nt(a, b):
    return lax.dot_general(a, b, (((1,), (1,)), ((), ())), preferred_element_type=F32)


def _dot_tn(a, b):
    return lax.dot_general(a, b, (((0,), (0,)), ((), ())), preferred_element_type=F32)


def _sigmoid(x):
    return 1.0 / (1.0 + jnp.exp(-x))


def _norm_proj_kernel(has_aux, x_ref, g_ref, w_ref, *rest):
    x = x_ref[...]
    ms = jnp.mean(x * x, axis=-1, keepdims=True)
    xn = x * lax.rsqrt(ms + EPS) * g_ref[...]
    xb = xn.astype(BF16)
    rest[-2 if has_aux else -1][...] = jnp.dot(xb, w_ref[...], preferred_element_type=F32).astype(BF16)
    if has_aux:
        whi_ref, wlo_ref, _, oaux_ref = rest
        xl = (xn - xb.astype(F32)).astype(BF16)
        d = lambda a, b: jnp.dot(a, b[...], preferred_element_type=F32)
        oaux_ref[...] = d(xb, whi_ref) + (d(xl, whi_ref) + d(xb, wlo_ref))


def _norm_proj(x, gain, w, w_aux=None):
    n, d = x.shape
    m = w.shape[1]
    tm = PROJ_TM
    has_aux = w_aux is not None
    in_specs = [pl.BlockSpec((tm, d), lambda i: (i, 0)),
                pl.BlockSpec((1, d), lambda i: (0, 0)),
                pl.BlockSpec((d, m), lambda i: (0, 0))]
    out_specs = [pl.BlockSpec((tm, m), lambda i: (i, 0))]
    out_shape = [jax.ShapeDtypeStruct((n, m), BF16)]
    args = [x, gain.reshape(1, d), w]
    if has_aux:
        w_hi = w_aux.astype(BF16)
        w_lo = (w_aux - w_hi.astype(F32)).astype(BF16)
        in_specs += [pl.BlockSpec((d, LANES), lambda i: (0, 0))] * 2
        out_specs.append(pl.BlockSpec((tm, LANES), lambda i: (i, 0)))
        out_shape.append(jax.ShapeDtypeStruct((n, LANES), F32))
        args += [w_hi, w_lo]
    res = pl.pallas_call(
        functools.partial(_norm_proj_kernel, has_aux),
        grid=(n // tm,),
        in_specs=in_specs, out_specs=out_specs, out_shape=out_shape,
        compiler_params=pltpu.CompilerParams(
            dimension_semantics=("parallel",), vmem_limit_bytes=VMEM_LIMIT),
    )(*args)
    return res if has_aux else res[0]


def _proj_res_kernel(n_in, *refs):
    h_ref = refs[2 * n_in]
    o_ref = refs[2 * n_in + 1]
    acc = h_ref[...]
    for t in range(n_in):
        acc = acc + jnp.dot(refs[2 * t][...], refs[2 * t + 1][...], preferred_element_type=F32)
    o_ref[...] = acc


def _proj_residual(pairs, h):
    n, d = h.shape
    tm = PROJ_TM
    in_specs, args = [], []
    for a, w in pairs:
        k = a.shape[1]
        in_specs += [pl.BlockSpec((tm, k), lambda i: (i, 0)),
                     pl.BlockSpec((k, d), lambda i: (0, 0))]
        args += [a, w]
    in_specs.append(pl.BlockSpec((tm, d), lambda i: (i, 0)))
    args.append(h)
    return pl.pallas_call(
        functools.partial(_proj_res_kernel, len(pairs)),
        grid=(n // tm,),
        in_specs=in_specs,
        out_specs=pl.BlockSpec((tm, d), lambda i: (i, 0)),
        out_shape=jax.ShapeDtypeStruct((n, d), F32),
        compiler_params=pltpu.CompilerParams(
            dimension_semantics=("parallel",), vmem_limit_bytes=VMEM_LIMIT),
    )(*args)


_HGRN_LEVELS = (64, 32, 16)
_HGRN_DIAG = 8


def _hgrn_constants():
    c = HGRN_CHUNK
    idx = np.arange(c)
    low = (idx[None, :] <= idx[:, None]).astype(np.float64)

    def ref_rows(r):
        return (idx[None, :] <= r[:, None]).astype(np.float64)

    blocks = [low, ref_rows(np.full(c, c - 1)) - low]
    masks = []
    for b in _HGRN_LEVELS:
        start = (idx // b) * b
        upper = (idx - start) >= b // 2
        ref = start + b // 2 - 1
        blocks.append(low - ref_rows(np.where(upper, ref, idx)))
        blocks.append(ref_rows(np.where(upper, idx, ref)) - low)
        same = (idx[:, None] // b) == (idx[None, :] // b)
        masks.append(same & upper[:, None] & ~upper[None, :])
    ref = (idx // _HGRN_DIAG) * _HGRN_DIAG + _HGRN_DIAG // 2 - 1
    blocks.append(low - ref_rows(ref))
    blocks.append(ref_rows(ref) - low)
    same = (idx[:, None] // _HGRN_DIAG) == (idx[None, :] // _HGRN_DIAG)
    masks.append(same & (idx[None, :] <= idx[:, None]))
    dst = np.concatenate(blocks, axis=0)
    return dst.astype(np.float32), np.stack(masks).astype(np.float32)


def _hgrn_kernel(q_ref, f_ref, i_ref, g_ref, lb_ref, gn_ref, dst_ref, mask_ref, o_ref, st_ref):
    c = HGRN_CHUNK
    n_heads = st_ref.shape[0]
    n_lvl = mask_ref.shape[0]

    @pl.when(pl.program_id(1) == 0)
    def _():
        st_ref[...] = jnp.zeros_like(st_ref)

    lb = lb_ref[...]
    gn = gn_ref[...]
    dst = dst_ref[...]
    for ch in range(q_ref.shape[0] // c):
        rows = pl.ds(ch * c, c)
        q = q_ref[rows, :].astype(F32)
        qf = q * _sigmoid(q)
        f = lb + (1.0 - lb) * _sigmoid(f_ref[rows, :].astype(F32))
        kk = 1.0 - f
        ex = jnp.exp(_dot3_stacked(dst, jnp.log(f)))
        v = i_ref[rows, :].astype(BF16)
        g = g_ref[rows, :].astype(F32)
        gate = g * _sigmoid(g)
        q_in = (qf * ex[0:c]).astype(BF16)
        k_st = (kk * ex[c:2 * c]).astype(BF16)
        dec = ex[c - 1:c]
        q_l = [(qf * ex[(2 + 2 * l) * c:(3 + 2 * l) * c]).astype(BF16) for l in range(n_lvl)]
        k_l = [(kk * ex[(3 + 2 * l) * c:(4 + 2 * l) * c]).astype(BF16) for l in range(n_lvl)]
        outs = []
        for h in range(n_heads):
            hs = slice(h * HEAD_DIM, (h + 1) * HEAD_DIM)
            scores = mask_ref[0] * _dot_nt(q_l[0][:, hs], k_l[0][:, hs])
            for l in range(1, n_lvl):
                scores = scores + mask_ref[l] * _dot_nt(q_l[l][:, hs], k_l[l][:, hs])
            st = st_ref[h]
            o = (jnp.dot(scores.astype(BF16), v[:, hs], preferred_element_type=F32)
                 + _dot_nt(q_in[:, hs], st.astype(BF16)))
            st_ref[h] = st * dec[:, hs] + _dot_tn(v[:, hs], k_st[:, hs])
            ms = jnp.mean(o * o, axis=-1, keepdims=True)
            outs.append(o * lax.rsqrt(ms + EPS) * gn)
        o_ref[rows, :] = (jnp.concatenate(outs, axis=-1) * gate).astype(o_ref.dtype)


def _hgrn2(proj, lb, out_norm, batch, seq):
    n = proj.shape[0]
    width = lb.shape[0]
    n_heads = width // HEAD_DIM
    rb = HGRN_ROWS
    spb = seq // rb
    dst, masks = _hgrn_constants()
    col = lambda j: pl.BlockSpec((rb, width), lambda b, s, j=j: (b * spb + s, j))
    full = lambda a: pl.BlockSpec(a.shape, lambda b, s: (0,) * a.ndim)
    lb2 = lb.reshape(1, width)
    gn = out_norm.reshape(1, HEAD_DIM)
    dst = jnp.asarray(np.concatenate([dst, dst, dst], axis=1), BF16)
    masks = jnp.asarray(masks, F32)
    return pl.pallas_call(
        _hgrn_kernel,
        grid=(batch, spb),
        in_specs=[col(0), col(1), col(2), col(3), full(lb2), full(gn), full(dst), full(masks)],
        out_specs=pl.BlockSpec((rb, width), lambda b, s: (b * spb + s, 0)),
        out_shape=jax.ShapeDtypeStruct((n, width), BF16),
        scratch_shapes=[pltpu.VMEM((n_heads, HEAD_DIM, HEAD_DIM), F32)],
        compiler_params=pltpu.CompilerParams(
            dimension_semantics=("parallel", "arbitrary"), vmem_limit_bytes=VMEM_LIMIT),
    )(proj, proj, proj, proj, lb2, gn, dst, masks)


def _fox_prep_kernel(q_ref, k_ref, v_ref, gate_ref, bias_ref, gq_ref, gk_ref, tril_ref,
                     sq_ref, sk_ref, cq_ref, ck_ref, cv_ref, grp_ref,
                     qa_ref, ka_ref, va_ref, carry_ref):
    n_heads = qa_ref.shape[1]
    tm = q_ref.shape[0]

    @pl.when(pl.program_id(1) == 0)
    def _():
        carry_ref[...] = jnp.zeros_like(carry_ref)

    z = gate_ref[...] + bias_ref[...]
    ls = -(jnp.maximum(-z, 0.0) + jnp.log(1.0 + jnp.exp(-jnp.abs(z))))
    cum = _dot3(tril_ref[...], ls) + carry_ref[...]
    carry_ref[...] = cum[tm - 1:tm]
    cum = cum * LOG2E

    c3 = jnp.concatenate(_split3(cum), axis=-1)
    ext_q = jnp.dot(c3, sq_ref[...], preferred_element_type=F32) + cq_ref[...]
    ext_k = jnp.dot(c3, sk_ref[...], preferred_element_type=F32) + ck_ref[...]

    lane = lax.broadcasted_iota(jnp.int32, (tm, LANES), 1)
    low_half = lane < HEAD_DIM
    grp = grp_ref[...]
    scale = HEAD_DIM ** -0.5 * LOG2E
    for c in range(n_heads // 2):
        cols = slice(c * LANES, (c + 1) * LANES)
        q = q_ref[:, cols].astype(F32)
        k = k_ref[:, cols].astype(F32)
        v = v_ref[:, cols]
        qn = q * lax.rsqrt(_dot2_rhs(q * q, grp) + EPS) * gq_ref[...] * scale
        kn = k * lax.rsqrt(_dot2_rhs(k * k, grp) + EPS) * gk_ref[...]
        for par in range(2):
            h = 2 * c + par
            data = low_half if par == 0 else jnp.logical_not(low_half)
            ext = slice(h * LANES, (h + 1) * LANES)
            qa_ref[0, h] = jnp.where(data, qn, ext_q[:, ext]).astype(BF16)
            ka_ref[0, h] = jnp.where(data, kn, ext_k[:, ext]).astype(BF16)
            va_ref[0, h] = jnp.where(data, v, cv_ref[par:par + 1, :].astype(BF16))


def _fox_layout_constants(n_heads):
    sq = np.zeros((3 * LANES, n_heads * LANES), np.float32)
    sk = np.zeros((3 * LANES, n_heads * LANES), np.float32)
    cq = np.zeros((1, n_heads * LANES), np.float32)
    ck = np.zeros((1, n_heads * LANES), np.float32)
    cv = np.zeros((2, LANES), np.float32)
    for h in range(n_heads):
        x0 = h * LANES + (HEAD_DIM if h % 2 == 0 else 0)
        for t in range(3):
            sq[t * LANES + h, x0 + t] = 1.0
            sk[t * LANES + h, x0 + 3 + t] = -1.0
        cq[0, x0 + 3:x0 + 6] = 1.0
        ck[0, x0:x0 + 3] = 1.0
    cv[0, HEAD_DIM] = 1.0
    cv[1, 0] = 1.0
    return (jnp.asarray(sq, BF16), jnp.asarray(sk, BF16), jnp.asarray(cq), jnp.asarray(ck), jnp.asarray(cv))


def _group_mean_matrix():
    lane = np.arange(LANES)
    return jnp.asarray(((lane[:, None] // HEAD_DIM) == (lane[None, :] // HEAD_DIM)) / HEAD_DIM, BF16)


def _fox_prep(proj, gates, f_bias, q_norm, k_norm, batch, seq, col0):
    width = 512
    n_heads = width // HEAD_DIM
    tm = PREP_ROWS
    spb = seq // tm
    col = lambda j: pl.BlockSpec((tm, width), lambda b, s, j=j: (b * spb + s, col0 + j))
    full = lambda a: pl.BlockSpec(a.shape, lambda b, s: (0,) * a.ndim)
    bias = jnp.zeros((1, LANES), F32).at[0, :n_heads].set(f_bias)
    gq = jnp.tile(q_norm, LANES // HEAD_DIM).reshape(1, LANES)
    gk = jnp.tile(k_norm, LANES // HEAD_DIM).reshape(1, LANES)
    tril = jnp.asarray(np.tril(np.ones((tm, tm), np.float32)), BF16)
    consts = _fox_layout_constants(n_heads) + (_group_mean_matrix(),)
    out = jax.ShapeDtypeStruct((batch, n_heads, seq, LANES), BF16)
    ospec = pl.BlockSpec((1, n_heads, tm, LANES), lambda b, s: (b, 0, s, 0))
    return pl.pallas_call(
        _fox_prep_kernel,
        grid=(batch, spb),
        in_specs=[col(0), col(1), col(2),
                  pl.BlockSpec((tm, LANES), lambda b, s: (b * spb + s, 0)),
                  full(bias), full(gq), full(gk), full(tril)] + [full(a) for a in consts],
        out_specs=[ospec, ospec, ospec],
        out_shape=[out, out, out],
        scratch_shapes=[pltpu.VMEM((1, LANES), F32)],
        compiler_params=pltpu.CompilerParams(
            dimension_semantics=("parallel", "arbitrary"), vmem_limit_bytes=VMEM_LIMIT),
    )(proj, proj, proj, gates, bias, gq, gk, tril, *consts)


def _tri_tables(nq):
    qi = [q for q in range(nq) for _ in range(q + 1)]
    ki = [k for q in range(nq) for k in range(q + 1)]
    return jnp.asarray(qi, jnp.int32), jnp.asarray(ki, jnp.int32)


def _fox_attn_kernel(qt_ref, kt_ref, q_ref, k_ref, v_ref, o_ref, m_ref, acc_ref):
    p_idx = pl.program_id(2)
    qi = qt_ref[p_idx]
    ki = kt_ref[p_idx]
    hp = q_ref.shape[1]
    t = q_ref.shape[2]

    @pl.when(ki == 0)
    def _():
        m_ref[...] = jnp.full_like(m_ref, -jnp.inf)
        acc_ref[...] = jnp.zeros_like(acc_ref)

    def step(masked):
        for h in range(hp):
            s = _dot_nt(q_ref[0, h], k_ref[0, h])
            if masked:
                row = lax.broadcasted_iota(jnp.int32, (t, t), 0)
                colm = lax.broadcasted_iota(jnp.int32, (t, t), 1)
                s = jnp.where(colm <= row, s, -jnp.inf)
            m_old = m_ref[h]
            m_new = jnp.maximum(m_old, jnp.max(s, axis=-1, keepdims=True))
            p = jnp.exp2(s - m_new)
            acc_ref[h] = (jnp.exp2(m_old - m_new) * acc_ref[h]
                          + jnp.dot(p.astype(BF16), v_ref[0, h], preferred_element_type=F32))
            m_ref[h] = m_new

    @pl.when(ki < qi)
    def _():
        step(False)

    @pl.when(ki == qi)
    def _():
        step(True)
        _fox_finalize(acc_ref, o_ref)


def _fox_finalize(acc_ref, o_ref):
    a0 = acc_ref[0]
    a1 = acc_ref[1]
    lane = lax.broadcasted_iota(jnp.int32, a0.shape, 1)
    o_ref[0] = jnp.where(lane < HEAD_DIM, a0 / a0[:, HEAD_DIM:HEAD_DIM + 1], a1 / a1[:, 0:1]).astype(o_ref.dtype)


def _tile_plan(n_sub, diagonal):
    plan = []
    for qb in range(n_sub):
        if not diagonal:
            plan.append((qb, 0, n_sub, False))
        else:
            if qb > 0:
                plan.append((qb, 0, qb, False))
            plan.append((qb, qb, qb + 1, True))
    return plan


def _fox_fast_kernel(qt_ref, kt_ref, q_ref, k_ref, v_ref, o_ref, acc_ref):
    p_idx = pl.program_id(2)
    qi = qt_ref[p_idx]
    ki = kt_ref[p_idx]
    hp = q_ref.shape[1]
    sb = ATTN_FAST_SUB
    n_sub = q_ref.shape[2] // sb

    @pl.when(ki == 0)
    def _():
        acc_ref[...] = jnp.zeros_like(acc_ref)

    def tile(diagonal):
        for h in range(hp):
            for qb, k0, k1, masked in _tile_plan(n_sub, diagonal):
                rows = pl.ds(qb * sb, sb)
                cols = pl.ds(k0 * sb, (k1 - k0) * sb)
                s = _dot_nt(q_ref[0, h, rows, :], k_ref[0, h, cols, :])
                if masked:
                    row = lax.broadcasted_iota(jnp.int32, (sb, sb), 0)
                    colm = lax.broadcasted_iota(jnp.int32, (sb, sb), 1)
                    s = jnp.where(colm <= row, s, -jnp.inf)
                p = jnp.exp2(s).astype(BF16)
                acc_ref[h, rows, :] += jnp.dot(p, v_ref[0, h, cols, :], preferred_element_type=F32)

    @pl.when(ki < qi)
    def _():
        tile(False)

    @pl.when(ki == qi)
    def _():
        tile(True)
        _fox_finalize(acc_ref, o_ref)


def _fox_attention(qa, ka, va, fast):
    batch, n_heads, seq, _ = qa.shape
    t = ATTN_FAST_TILE if fast else ATTN_TILE
    hp = 2
    nq = seq // t
    qt, kt = _tri_tables(nq)
    qspec = pl.BlockSpec((1, hp, t, LANES), lambda b, g, p, qt, kt: (b, g, qt[p], 0))
    kspec = pl.BlockSpec((1, hp, t, LANES), lambda b, g, p, qt, kt: (b, g, kt[p], 0))
    scratch = [pltpu.VMEM((hp, t, LANES), F32)]
    if not fast:
        scratch = [pltpu.VMEM((hp, t, 1), F32)] + scratch
    return pl.pallas_call(
        _fox_fast_kernel if fast else _fox_attn_kernel,
        grid_spec=pltpu.PrefetchScalarGridSpec(
            num_scalar_prefetch=2,
            grid=(batch, n_heads // hp, int(qt.shape[0])),
            in_specs=[qspec, kspec, kspec],
            out_specs=pl.BlockSpec((1, t, hp * HEAD_DIM), lambda b, g, p, qt, kt: (b, qt[p], g)),
            scratch_shapes=scratch),
        out_shape=jax.ShapeDtypeStruct((batch, seq, n_heads * HEAD_DIM), BF16),
        compiler_params=pltpu.CompilerParams(
            dimension_semantics=("parallel", "parallel", "arbitrary"), vmem_limit_bytes=VMEM_LIMIT),
    )(qt, kt, qa, ka, va)


def _logit_bound(q_gain, k_gain):
    return HEAD_DIM ** 0.5 * jnp.max(jnp.abs(q_gain)) * jnp.max(jnp.abs(k_gain))


def _diff_prep_kernel(q_ref, k_ref, v_ref, pos_ref, invf_ref, gq_ref, gk_ref, grp_ref,
                      qm_ref, k2_ref, va_ref):
    n_heads = k2_ref.shape[1]
    tm = q_ref.shape[0]
    ang = pos_ref[...].astype(F32) * invf_ref[...]
    lane = lax.broadcasted_iota(jnp.int32, (tm, LANES), 1)
    first = (lane % HEAD_DIM) < (HEAD_DIM // 2)
    cs = jnp.cos(ang)
    sn = jnp.sin(ang)
    sn = jnp.where(first, -sn, sn)
    grp = grp_ref[...]
    scale = HEAD_DIM ** -0.5 * LOG2E
    zero = jnp.zeros((tm, LANES), F32)
    onecol = jnp.where(lane == 0, 1.0, 0.0).astype(BF16)

    def norm_rope(x, gain):
        ms = _dot2_rhs(x * x, grp)
        y = x * lax.rsqrt(ms + EPS) * gain
        yr = jnp.where(first, pltpu.roll(y, LANES - HEAD_DIM // 2, 1), pltpu.roll(y, HEAD_DIM // 2, 1))
        return y * cs + yr * sn

    for h in range(n_heads):
        cols = slice(h * LANES, (h + 1) * LANES)
        qr = norm_rope(q_ref[:, cols].astype(F32), gq_ref[...]) * scale
        kr = norm_rope(k_ref[:, cols].astype(F32), gk_ref[...])
        qm_ref[0, h, 0] = jnp.where(lane < HEAD_DIM, qr, zero).astype(BF16)
        qm_ref[0, h, 1] = jnp.where(lane < HEAD_DIM, zero, qr).astype(BF16)
        k2_ref[0, h] = kr.astype(BF16)
        va_ref[0, h] = jnp.concatenate([v_ref[:, cols].astype(BF16), onecol], axis=-1)


def _diff_prep(proj, positions, q_norm, k_norm, batch, seq):
    n = proj.shape[0]
    width = proj.shape[1] // 3
    n_heads = width // LANES
    tm = PREP_ROWS
    spb = seq // tm
    col = lambda j: pl.BlockSpec((tm, width), lambda b, s, j=j: (b * spb + s, j))
    full = lambda a: pl.BlockSpec(a.shape, lambda b, s: (0,) * a.ndim)
    half = HEAD_DIM // 2
    inv_freq = ROPE_THETA ** (-jnp.arange(half, dtype=F32) / half)
    invf = jnp.tile(inv_freq, LANES // half).reshape(1, LANES)
    gq = jnp.tile(q_norm, LANES // HEAD_DIM).reshape(1, LANES)
    gk = jnp.tile(k_norm, LANES // HEAD_DIM).reshape(1, LANES)
    grp = _group_mean_matrix()
    pos = positions.reshape(n, 1).astype(jnp.int32)
    return pl.pallas_call(
        _diff_prep_kernel,
        grid=(batch, spb),
        in_specs=[col(0), col(1), col(2),
                  pl.BlockSpec((tm, 1), lambda b, s: (b * spb + s, 0)),
                  full(invf), full(gq), full(gk), full(grp)],
        out_specs=[pl.BlockSpec((1, n_heads, 2, tm, LANES), lambda b, s: (b, 0, 0, s, 0)),
                   pl.BlockSpec((1, n_heads, tm, LANES), lambda b, s: (b, 0, s, 0)),
                   pl.BlockSpec((1, n_heads, tm, 2 * LANES), lambda b, s: (b, 0, s, 0))],
        out_shape=[jax.ShapeDtypeStruct((batch, n_heads, 2, seq, LANES), BF16),
                   jax.ShapeDtypeStruct((batch, n_heads, seq, LANES), BF16),
                   jax.ShapeDtypeStruct((batch, n_heads, seq, 2 * LANES), BF16)],
        compiler_params=pltpu.CompilerParams(
            dimension_semantics=("parallel", "parallel"), vmem_limit_bytes=VMEM_LIMIT),
    )(proj, proj, proj, pos, invf, gq, gk, grp)


def _diff_attn_kernel(lambda_init, qt_ref, kt_ref, q_ref, k_ref, v_ref, lam_ref, sub_ref,
                      o_ref, m_ref, acc_ref):
    p_idx = pl.program_id(2)
    qi = qt_ref[p_idx]
    ki = kt_ref[p_idx]
    t = k_ref.shape[2]
    dv = o_ref.shape[2]

    @pl.when(ki == 0)
    def _():
        m_ref[...] = jnp.full_like(m_ref, -jnp.inf)
        acc_ref[...] = jnp.zeros_like(acc_ref)

    def step(masked):
        for m in range(2):
            s = _dot_nt(q_ref[0, 0, m], k_ref[0, 0])
            if masked:
                row = lax.broadcasted_iota(jnp.int32, (t, t), 0) // CHUNK
                colm = lax.broadcasted_iota(jnp.int32, (t, t), 1) // CHUNK
                s = jnp.where(colm <= row, s, -jnp.inf)
            m_old = m_ref[m]
            m_new = jnp.maximum(m_old, jnp.max(s, axis=-1, keepdims=True))
            p = jnp.exp2(s - m_new)
            acc_ref[m] = (jnp.exp2(m_old - m_new) * acc_ref[m]
                          + jnp.dot(p.astype(BF16), v_ref[0, 0], preferred_element_type=F32))
            m_ref[m] = m_new

    @pl.when(ki < qi)
    def _():
        step(False)

    @pl.when(ki == qi)
    def _():
        step(True)
        _diff_finalize(lambda_init, acc_ref, lam_ref, sub_ref, o_ref)


def _diff_finalize(lambda_init, acc_ref, lam_ref, sub_ref, o_ref):
    dv = o_ref.shape[2]
    lp = lam_ref[...]
    lam = (jnp.exp(jnp.sum(lp[0:1] * lp[1:2], axis=-1, keepdims=True))
           - jnp.exp(jnp.sum(lp[2:3] * lp[3:4], axis=-1, keepdims=True)) + lambda_init)
    a0 = acc_ref[0]
    a1 = acc_ref[1]
    o = a0[:, :dv] / a0[:, dv:dv + 1] - lam * (a1[:, :dv] / a1[:, dv:dv + 1])
    ms = jnp.mean(o * o, axis=-1, keepdims=True)
    o_ref[0] = ((o * lax.rsqrt(ms + EPS) * sub_ref[...]) * (1.0 - lambda_init)).astype(o_ref.dtype)


def _diff_fast_kernel(lambda_init, qt_ref, kt_ref, q_ref, k_ref, v_ref, lam_ref, sub_ref, o_ref, acc_ref):
    p_idx = pl.program_id(2)
    qi = qt_ref[p_idx]
    ki = kt_ref[p_idx]
    sb = ATTN_FAST_SUB
    n_sub = k_ref.shape[2] // sb

    @pl.when(ki == 0)
    def _():
        acc_ref[...] = jnp.zeros_like(acc_ref)

    def tile(diagonal):
        for m in range(2):
            for qb, k0, k1, masked in _tile_plan(n_sub, diagonal):
                rows = pl.ds(qb * sb, sb)
                cols = pl.ds(k0 * sb, (k1 - k0) * sb)
                s = _dot_nt(q_ref[0, 0, m, rows, :], k_ref[0, 0, cols, :])
                if masked:
                    row = lax.broadcasted_iota(jnp.int32, (sb, sb), 0) // CHUNK
                    colm = lax.broadcasted_iota(jnp.int32, (sb, sb), 1) // CHUNK
                    s = jnp.where(colm <= row, s, -jnp.inf)
                p = jnp.exp2(s).astype(BF16)
                acc_ref[m, rows, :] += jnp.dot(p, v_ref[0, 0, cols, :], preferred_element_type=F32)

    @pl.when(ki < qi)
    def _():
        tile(False)

    @pl.when(ki == qi)
    def _():
        tile(True)
        _diff_finalize(lambda_init, acc_ref, lam_ref, sub_ref, o_ref)


def _diff_attention(qm, k2, va, lam_params, subln, lambda_init, fast):
    batch, n_heads, seq, _ = k2.shape
    dv = va.shape[3] // 2
    t = ATTN_FAST_TILE if fast else ATTN_TILE
    nq = seq // t
    qt, kt = _tri_tables(nq)
    lamp = jnp.zeros((8, LANES), F32).at[:4, :HEAD_DIM].set(lam_params)
    sub = subln.reshape(1, dv)
    scratch = [pltpu.VMEM((2, t, 2 * dv), F32)]
    if not fast:
        scratch = [pltpu.VMEM((2, t, 1), F32)] + scratch
    return pl.pallas_call(
        functools.partial(_diff_fast_kernel if fast else _diff_attn_kernel, lambda_init),
        grid_spec=pltpu.PrefetchScalarGridSpec(
            num_scalar_prefetch=2,
            grid=(batch, n_heads, int(qt.shape[0])),
            in_specs=[pl.BlockSpec((1, 1, 2, t, LANES), lambda b, h, p, qt, kt: (b, h, 0, qt[p], 0)),
                      pl.BlockSpec((1, 1, t, LANES), lambda b, h, p, qt, kt: (b, h, kt[p], 0)),
                      pl.BlockSpec((1, 1, t, 2 * dv), lambda b, h, p, qt, kt: (b, h, kt[p], 0)),
                      pl.BlockSpec((8, LANES), lambda b, h, p, qt, kt: (0, 0)),
                      pl.BlockSpec((1, dv), lambda b, h, p, qt, kt: (0, 0))],
            out_specs=pl.BlockSpec((1, t, dv), lambda b, h, p, qt, kt: (b, qt[p], h)),
            scratch_shapes=scratch),
        out_shape=jax.ShapeDtypeStruct((batch, seq, n_heads * dv), BF16),
        compiler_params=pltpu.CompilerParams(
            dimension_semantics=("parallel", "parallel", "arbitrary"), vmem_limit_bytes=VMEM_LIMIT),
    )(qt, kt, qm, k2, va, lamp, sub)


def _router_kernel(h_ref, g_ref, whi_ref, wlo_ref, xn_ref, route_ref):
    x = h_ref[...]
    tm = x.shape[0]
    xn = x * lax.rsqrt(jnp.mean(x * x, axis=-1, keepdims=True) + EPS) * g_ref[...]
    xn_ref[...] = xn
    xh = xn.astype(BF16)
    xl = (xn - xh.astype(F32)).astype(BF16)
    d = lambda a, b: jnp.dot(a, b[...], preferred_element_type=F32)
    logits = d(xh, whi_ref) + (d(xl, whi_ref) + d(xh, wlo_ref))
    lane = lax.broadcasted_iota(jnp.int32, (tm, LANES), 1)
    neg = jnp.full((tm, LANES), -jnp.inf, F32)
    big = jnp.full((tm, LANES), LANES, jnp.int32)

    def top1(vals):
        m = jnp.max(vals, axis=-1, keepdims=True)
        idx = jnp.min(jnp.where(vals == m, lane, big), axis=-1, keepdims=True)
        return m, idx

    grp_logits = jnp.where(lane < N_GROUPS, logits, neg)
    mg, gidx = top1(grp_logits)
    p_g = 1.0 / jnp.sum(jnp.exp(grp_logits - mg), axis=-1, keepdims=True)
    e_lane = lane - N_GROUPS
    in_grp = (e_lane >= gidx * EXPERTS_PER_GROUP) & (e_lane < (gidx + 1) * EXPERTS_PER_GROUP)
    sel = jnp.where(in_grp, logits, neg)
    m1, i1 = top1(sel)
    m2, i2 = top1(jnp.where(lane == i1, neg, sel))
    r = jnp.exp(m2 - m1)
    w1 = p_g / (1.0 + r)
    w2 = p_g * r / (1.0 + r)
    zero = jnp.zeros((tm, LANES), F32)
    route_ref[...] = jnp.where(lane == 0, (i1 - N_GROUPS).astype(F32),
                     jnp.where(lane == 1, (i2 - N_GROUPS).astype(F32),
                     jnp.where(lane == 2, w1, jnp.where(lane == 3, w2, zero))))


def _router(h, gain, w_group, w_expert):
    n, d = h.shape
    tm = ROUTER_ROWS
    wr = jnp.zeros((d, LANES), F32)
    wr = wr.at[:, :N_GROUPS].set(w_group)
    wr = wr.at[:, N_GROUPS:N_GROUPS + N_EXPERTS].set(
        jnp.transpose(w_expert, (1, 0, 2)).reshape(d, N_EXPERTS))
    w_hi = wr.astype(BF16)
    w_lo = (wr - w_hi.astype(F32)).astype(BF16)
    return pl.pallas_call(
        _router_kernel,
        grid=(n // tm,),
        in_specs=[pl.BlockSpec((tm, d), lambda i: (i, 0)),
                  pl.BlockSpec((1, d), lambda i: (0, 0)),
                  pl.BlockSpec((d, LANES), lambda i: (0, 0)),
                  pl.BlockSpec((d, LANES), lambda i: (0, 0))],
        out_specs=[pl.BlockSpec((tm, d), lambda i: (i, 0)),
                   pl.BlockSpec((tm, LANES), lambda i: (i, 0))],
        out_shape=[jax.ShapeDtypeStruct((n, d), F32), jax.ShapeDtypeStruct((n, LANES), F32)],
        compiler_params=pltpu.CompilerParams(
            dimension_semantics=("parallel",), vmem_limit_bytes=VMEM_LIMIT),
    )(h, gain.reshape(1, d), w_hi, w_lo)


def _dispatch_tables(expert_ids, tm):
    n = expert_ids.shape[0]
    pairs = expert_ids.reshape(-1)
    n_pairs = pairs.shape[0]
    n_tiles = n_pairs // tm + N_EXPERTS
    onehot = (pairs[:, None] == jnp.arange(N_EXPERTS, dtype=jnp.int32)[None, :]).astype(jnp.int32)
    csum = jnp.cumsum(onehot, axis=0)
    rank = jnp.sum((csum - onehot) * onehot, axis=1)
    counts = csum[-1]
    padded = ((counts + tm - 1) // tm) * tm
    ends = jnp.cumsum(padded)
    starts = ends - padded
    pos = jnp.sum(onehot * starts[None, :], axis=1) + rank
    slot_token = jnp.zeros((n_tiles * tm,), jnp.int32).at[pos].set(
        jnp.arange(n_pairs, dtype=jnp.int32) // 2)
    tile_start = jnp.arange(n_tiles, dtype=jnp.int32) * tm
    tile_expert = jnp.minimum(jnp.sum(tile_start[:, None] >= ends[None, :], axis=1),
                              N_EXPERTS - 1).astype(jnp.int32)
    n_valid = (ends[-1] // tm).astype(jnp.int32).reshape(1)
    return slot_token.reshape(n_tiles, 1, tm), tile_expert, n_valid, pos.reshape(n, 2).astype(jnp.int32)


def _gather_rows(src_hbm, idx_ref, dst_ref, sem, n_rows):
    def body(r, carry):
        tok = idx_ref[0, 0, r]
        pltpu.make_async_copy(src_hbm.at[pl.ds(tok, 1)], dst_ref.at[pl.ds(r, 1)], sem).start()
        return carry
    lax.fori_loop(0, n_rows, body, 0, unroll=8)


def _wait_rows(src_hbm, dst_ref, sem, n_rows):
    pltpu.make_async_copy(src_hbm.at[pl.ds(0, n_rows)], dst_ref, sem).wait()


def _moe_kernel(te_ref, nv_ref, tok_ref, tok_next_ref, wg_ref, wu_ref, wd_ref, x_hbm,
                o_ref, xbuf, wg_b, wu_b, wd_b, sems):
    i = pl.program_id(0)
    tm = xbuf.shape[1]
    n_valid = nv_ref[0]
    slot = i % 2
    n_phases = 4
    per_phase = tm // n_phases

    @pl.when(i == 0)
    def _():
        _gather_rows(x_hbm, tok_ref, xbuf.at[0], sems.at[0], tm)

    new_expert = jnp.logical_or(i == 0, te_ref[i] != te_ref[jnp.maximum(i - 1, 0)])

    @pl.when(jnp.logical_and(i < n_valid, new_expert))
    def _():
        wg_b[...] = wg_ref[0].astype(BF16)
        wu_b[...] = wu_ref[0].astype(BF16)
        wd_b[...] = wd_ref[0].astype(BF16)

    def prefetch(phase):
        for r in range(phase * per_phase, (phase + 1) * per_phase):
            tok = tok_next_ref[0, 0, r]
            pltpu.make_async_copy(x_hbm.at[pl.ds(tok, 1)], xbuf.at[1 - slot, pl.ds(r, 1)],
                                  sems.at[1 - slot]).start(priority=r % 2)

    @pl.when(i < n_valid)
    def _():
        _wait_rows(x_hbm, xbuf.at[slot], sems.at[slot], tm)
        x = xbuf[slot].astype(BF16)
        prefetch(0)
        g = jnp.dot(x, wg_b[...], preferred_element_type=F32)
        prefetch(1)
        u = jnp.dot(x, wu_b[...], preferred_element_type=F32)
        prefetch(2)
        hid = (g * _sigmoid(g) * u).astype(BF16)
        o_ref[...] = jnp.dot(hid, wd_b[...], preferred_element_type=F32)
        prefetch(3)

    @pl.when(i == n_valid)
    def _():
        _wait_rows(x_hbm, xbuf.at[slot], sems.at[slot], tm)

    @pl.when(i >= n_valid)
    def _():
        o_ref[...] = jnp.zeros_like(o_ref)


def _moe_experts(xn, slot_token, tile_expert, n_valid, w_gate, w_up, w_down):
    n, d = xn.shape
    n_exp, _, f = w_gate.shape
    n_tiles, _, tm = slot_token.shape
    last = n_tiles - 1
    return pl.pallas_call(
        _moe_kernel,
        grid_spec=pltpu.PrefetchScalarGridSpec(
            num_scalar_prefetch=2,
            grid=(n_tiles,),
            in_specs=[
                pl.BlockSpec((1, 1, tm), lambda i, te, nv: (i, 0, 0), memory_space=pltpu.SMEM),
                pl.BlockSpec((1, 1, tm), lambda i, te, nv: (jnp.minimum(i + 1, last), 0, 0),
                             memory_space=pltpu.SMEM),
                pl.BlockSpec((1, d, f), lambda i, te, nv: (te[i], 0, 0)),
                pl.BlockSpec((1, d, f), lambda i, te, nv: (te[i], 0, 0)),
                pl.BlockSpec((1, f, d), lambda i, te, nv: (te[i], 0, 0)),
                pl.BlockSpec(memory_space=pl.ANY)],
            out_specs=pl.BlockSpec((tm, d), lambda i, te, nv: (i, 0)),
            scratch_shapes=[pltpu.VMEM((2, tm, d), F32),
                            pltpu.VMEM((d, f), BF16), pltpu.VMEM((d, f), BF16), pltpu.VMEM((f, d), BF16),
                            pltpu.SemaphoreType.DMA((2,))]),
        out_shape=jax.ShapeDtypeStruct((n_tiles * tm, d), F32),
        compiler_params=pltpu.CompilerParams(
            dimension_semantics=("arbitrary",), vmem_limit_bytes=VMEM_LIMIT),
    )(tile_expert, n_valid, slot_token, slot_token, w_gate, w_up, w_down, xn)


def _combine_kernel(pos_ref, pos_next_ref, h_ref, route_ref, y_hbm, o_ref, ybuf, sems):
    i = pl.program_id(0)
    n_steps = pl.num_programs(0)
    rows = ybuf.shape[1]
    slot = i % 2

    @pl.when(i == 0)
    def _():
        _gather_rows(y_hbm, pos_ref, ybuf.at[0], sems.at[0], rows)

    @pl.when(i + 1 < n_steps)
    def _():
        for r in range(rows):
            pltpu.make_async_copy(y_hbm.at[pl.ds(pos_next_ref[0, 0, r], 1)],
                                  ybuf.at[1 - slot, pl.ds(r, 1)], sems.at[1 - slot]).start(priority=r % 2)

    _wait_rows(y_hbm, ybuf.at[slot], sems.at[slot], rows)
    tc = rows // 2
    w = route_ref[...]
    o_ref[...] = (h_ref[...] + w[:, 2:3] * ybuf[slot, pl.ds(0, tc), :]
                  + w[:, 3:4] * ybuf[slot, pl.ds(tc, tc), :])


def _moe_combine(h, route, pos, y_sorted):
    n, d = h.shape
    tc = COMBINE_ROWS
    steps = n // tc
    pos_tab = jnp.transpose(pos.reshape(steps, tc, 2), (0, 2, 1)).reshape(steps, 1, 2 * tc)
    return pl.pallas_call(
        _combine_kernel,
        grid=(steps,),
        in_specs=[pl.BlockSpec((1, 1, 2 * tc), lambda i: (i, 0, 0), memory_space=pltpu.SMEM),
                  pl.BlockSpec((1, 1, 2 * tc), lambda i: (jnp.minimum(i + 1, steps - 1), 0, 0),
                               memory_space=pltpu.SMEM),
                  pl.BlockSpec((tc, d), lambda i: (i, 0)),
                  pl.BlockSpec((tc, LANES), lambda i: (i, 0)),
                  pl.BlockSpec(memory_space=pl.ANY)],
        out_specs=pl.BlockSpec((tc, d), lambda i: (i, 0)),
        out_shape=jax.ShapeDtypeStruct((n, d), F32),
        scratch_shapes=[pltpu.VMEM((2, 2 * tc, d), F32), pltpu.SemaphoreType.DMA((2,))],
        compiler_params=pltpu.CompilerParams(
            dimension_semantics=("arbitrary",), vmem_limit_bytes=VMEM_LIMIT),
    )(pos_tab, pos_tab, h, route, y_sorted)


def _moe_layer(h, layer, gain, w_group, w_expert, w_gate, w_up, w_down):
    d = h.shape[1]
    f = w_gate.shape[-1]
    xn, route = _router(h, gain, w_group, w_expert)
    expert_ids = route[:, :2].astype(jnp.int32)
    slot_token, tile_expert, n_valid, pos = _dispatch_tables(expert_ids, MOE_TM)
    y_sorted = _moe_experts(xn, slot_token, tile_expert + layer * N_EXPERTS, n_valid,
                            w_gate.reshape(-1, d, f), w_up.reshape(-1, d, f), w_down.reshape(-1, f, d))
    return _moe_combine(h, route, pos, y_sorted)


def _even_layer(h, batch, seq, gain, w_in, w_out, lb, f_bias, out_norm, q_norm, k_norm):
    d = h.shape[1]
    n_main = w_in.shape[1] - f_bias.shape[0]
    w_main = w_in[:, :n_main].astype(BF16)
    w_gate = jnp.zeros((d, LANES), F32).at[:, :f_bias.shape[0]].set(w_in[:, n_main:])
    proj, gates = _norm_proj(h, gain, w_main, w_gate)
    o_a = _hgrn2(proj, lb, out_norm, batch, seq)
    qa, ka, va = _fox_prep(proj, gates, f_bias, q_norm, k_norm, batch, seq, col0=4)
    o_b = lax.cond(_logit_bound(q_norm, k_norm) <= LOGIT_BOUND_MAX,
                   functools.partial(_fox_attention, fast=True),
                   functools.partial(_fox_attention, fast=False), qa, ka, va).reshape(batch * seq, -1)
    wo = w_out.astype(BF16)
    ka_dim = o_a.shape[1]
    return _proj_residual([(o_a, wo[:ka_dim]), (o_b, wo[ka_dim:])], h)


def _odd_layer(h, positions, batch, seq, gain, w_in, w_out, q_norm, k_norm, lam_params, subln, lambda_init):
    proj = _norm_proj(h, gain, w_in.astype(BF16))
    qm, k2, va = _diff_prep(proj, positions, q_norm, k_norm, batch, seq)
    attn = lambda fast: functools.partial(_diff_attention, lam_params=lam_params, subln=subln,
                                          lambda_init=lambda_init, fast=fast)
    o = lax.cond(_logit_bound(q_norm, k_norm) <= LOGIT_BOUND_MAX,
                 attn(True), attn(False), qm, k2, va).reshape(batch * seq, -1)
    return _proj_residual([(o, w_out.astype(BF16))], h)


def kernel(x, positions, hgrn_lb_logits, norm_mix, norm_ffn, even_w_in, even_w_out, fox_f_bias,
           hgrn_out_norm, fox_q_norm, fox_k_norm, odd_w_in, odd_w_out, diff_q_norm, diff_k_norm,
           diff_lambda_q1, diff_lambda_k1, diff_lambda_q2, diff_lambda_k2, diff_subln,
           moe_router_group, moe_router_expert, moe_w_gate, moe_w_up, moe_w_down):
    batch, seq, d = x.shape
    depth = norm_mix.shape[0]
    lower_bounds = jnp.cumsum(jax.nn.softmax(hgrn_lb_logits.astype(F32), axis=0), axis=0)
    h = x.reshape(batch * seq, d)
    for layer in range(depth):
        j = layer // 2
        if layer % 2 == 0:
            h = _even_layer(h, batch, seq, norm_mix[layer], even_w_in[j], even_w_out[j], lower_bounds[j],
                            fox_f_bias[j], hgrn_out_norm[j], fox_q_norm[j], fox_k_norm[j])
        else:
            lambda_init = 0.8 - 0.6 * math.exp(-0.3 * layer)
            lam_params = jnp.stack([diff_lambda_q1[j], diff_lambda_k1[j],
                                    diff_lambda_q2[j], diff_lambda_k2[j]]).astype(F32)
            h = _odd_layer(h, positions, batch, seq, norm_mix[layer], odd_w_in[j], odd_w_out[j],
                           diff_q_norm[j], diff_k_norm[j], lam_params, diff_subln[j], lambda_init)
        h = _moe_layer(h, layer, norm_ffn[layer], moe_router_group[layer], moe_router_expert[layer],
                       moe_w_gate, moe_w_up, moe_w_down)
    return h.reshape(batch, seq, d)
```

```python
import functools
import math

import numpy as np
import jax
import jax.numpy as jnp
from jax import lax
from jax.experimental import pallas as pl
from jax.experimental.pallas import tpu as pltpu

F32 = jnp.float32
BF16 = jnp.bfloat16

EPS = 1e-6
ROPE_THETA = 10000.0
CHUNK = 64
HEAD_DIM = 64
N_GROUPS = 4
EXPERTS_PER_GROUP = 8
N_EXPERTS = N_GROUPS * EXPERTS_PER_GROUP
LANES = 128
TILE_ROWS = 8

HGRN_CHUNK = 64
HGRN_ROWS = 256
ATTN_TILE = 512
ATTN_FAST_TILE = 2048
ATTN_FAST_SUB = 512
LOGIT_BOUND_MAX = 60.0
LOG2E = math.log2(math.e)
PREP_ROWS = 256
PROJ_TM = 256
ROUTER_ROWS = 256
MOE_TM = 256
COMBINE_ROWS = 256
VMEM_LIMIT = 56 * 1024 * 1024


def _split3(x):
    hi = x.astype(BF16)
    r1 = x - hi.astype(F32)
    mid = r1.astype(BF16)
    lo = (r1 - mid.astype(F32)).astype(BF16)
    return hi, mid, lo


def _dot3(const_bf16, x):
    hi, mid, lo = _split3(x)
    d = lambda b: jnp.dot(const_bf16, b, preferred_element_type=F32)
    return d(hi) + d(mid) + d(lo)


def _dot3_stacked(const3_bf16, x):
    return jnp.dot(const3_bf16, jnp.concatenate(_split3(x), axis=0), preferred_element_type=F32)


def _dot2_rhs(x, const_bf16):
    hi = x.astype(BF16)
    lo = (x - hi.astype(F32)).astype(BF16)
    d = lambda a: jnp.dot(a, const_bf16, preferred_element_type=F32)
    return d(hi) + d(lo)


def _dot_nt(a, b):
    return lax.dot_general(a, b, (((1,), (1,)), ((), ())), preferred_element_type=F32)


def _dot_tn(a, b):
    return lax.dot_general(a, b, (((0,), (0,)), ((), ())), preferred_element_type=F32)


def _sigmoid(x):
    return 1.0 / (1.0 + jnp.exp(-x))


def _norm_proj_kernel(has_aux, x_ref, g_ref, w_ref, *rest):
    x = x_ref[...]
    ms = jnp.mean(x * x, axis=-1, keepdims=True)
    xn = x * lax.rsqrt(ms + EPS) * g_ref[...]
    xb = xn.astype(BF16)
    rest[-2 if has_aux else -1][...] = jnp.dot(xb, w_ref[...], preferred_element_type=F32).astype(BF16)
    if has_aux:
        whi_ref, wlo_ref, _, oaux_ref = rest
        xl = (xn - xb.astype(F32)).astype(BF16)
        d = lambda a, b: jnp.dot(a, b[...], preferred_element_type=F32)
        oaux_ref[...] = d(xb, whi_ref) + (d(xl, whi_ref) + d(xb, wlo_ref))


def _norm_proj(x, gain, w, w_aux=None):
    n, d = x.shape
    m = w.shape[1]
    tm = PROJ_TM
    has_aux = w_aux is not None
    in_specs = [pl.BlockSpec((tm, d), lambda i: (i, 0)),
                pl.BlockSpec((1, d), lambda i: (0, 0)),
                pl.BlockSpec((d, m), lambda i: (0, 0))]
    out_specs = [pl.BlockSpec((tm, m), lambda i: (i, 0))]
    out_shape = [jax.ShapeDtypeStruct((n, m), BF16)]
    args = [x, gain.reshape(1, d), w]
    if has_aux:
        w_hi = w_aux.astype(BF16)
        w_lo = (w_aux - w_hi.astype(F32)).astype(BF16)
        in_specs += [pl.BlockSpec((d, LANES), lambda i: (0, 0))] * 2
        out_specs.append(pl.BlockSpec((tm, LANES), lambda i: (i, 0)))
        out_shape.append(jax.ShapeDtypeStruct((n, LANES), F32))
        args += [w_hi, w_lo]
    res = pl.pallas_call(
        functools.partial(_norm_proj_kernel, has_aux),
        grid=(n // tm,),
        in_specs=in_specs, out_specs=out_specs, out_shape=out_shape,
        compiler_params=pltpu.CompilerParams(
            dimension_semantics=("parallel",), vmem_limit_bytes=VMEM_LIMIT),
    )(*args)
    return res if has_aux else res[0]


def _proj_res_kernel(n_in, *refs):
    h_ref = refs[2 * n_in]
    o_ref = refs[2 * n_in + 1]
    acc = h_ref[...]
    for t in range(n_in):
        acc = acc + jnp.dot(refs[2 * t][...], refs[2 * t + 1][...], preferred_element_type=F32)
    o_ref[...] = acc


def _proj_residual(pairs, h):
    n, d = h.shape
    tm = PROJ_TM
    in_specs, args = [], []
    for a, w in pairs:
        k = a.shape[1]
        in_specs += [pl.BlockSpec((tm, k), lambda i: (i, 0)),
                     pl.BlockSpec((k, d), lambda i: (0, 0))]
        args += [a, w]
    in_specs.append(pl.BlockSpec((tm, d), lambda i: (i, 0)))
    args.append(h)
    return pl.pallas_call(
        functools.partial(_proj_res_kernel, len(pairs)),
        grid=(n // tm,),
        in_specs=in_specs,
        out_specs=pl.BlockSpec((tm, d), lambda i: (i, 0)),
        out_shape=jax.ShapeDtypeStruct((n, d), F32),
        compiler_params=pltpu.CompilerParams(
            dimension_semantics=("parallel",), vmem_limit_bytes=VMEM_LIMIT),
    )(*args)


_HGRN_LEVELS = (64, 32, 16)
_HGRN_DIAG = 8


def _hgrn_constants():
    c = HGRN_CHUNK
    idx = np.arange(c)
    low = (idx[None, :] <= idx[:, None]).astype(np.float64)

    def ref_rows(r):
        return (idx[None, :] <= r[:, None]).astype(np.float64)

    blocks = [low, ref_rows(np.full(c, c - 1)) - low]
    masks = []
    for b in _HGRN_LEVELS:
        start = (idx // b) * b
        upper = (idx - start) >= b // 2
        ref = start + b // 2 - 1
        blocks.append(low - ref_rows(np.where(upper, ref, idx)))
        blocks.append(ref_rows(np.where(upper, idx, ref)) - low)
        same = (idx[:, None] // b) == (idx[None, :] // b)
        masks.append(same & upper[:, None] & ~upper[None, :])
    ref = (idx // _HGRN_DIAG) * _HGRN_DIAG + _HGRN_DIAG // 2 - 1
    blocks.append(low - ref_rows(ref))
    blocks.append(ref_rows(ref) - low)
    same = (idx[:, None] // _HGRN_DIAG) == (idx[None, :] // _HGRN_DIAG)
    masks.append(same & (idx[None, :] <= idx[:, None]))
    dst = np.concatenate(blocks, axis=0)
    return dst.astype(np.float32), np.stack(masks).astype(np.float32)


def _hgrn_kernel(q_ref, f_ref, i_ref, g_ref, lb_ref, gn_ref, dst_ref, mask_ref, bd_ref, grp_ref,
                 o_ref, st_ref):
    c = HGRN_CHUNK
    n_pairs = st_ref.shape[0]
    n_lvl = mask_ref.shape[0]

    @pl.when(pl.program_id(1) == 0)
    def _():
        st_ref[...] = jnp.zeros_like(st_ref)

    lb = lb_ref[...]
    gn = gn_ref[...]
    dst = dst_ref[...]
    bd = bd_ref[...]
    grp = grp_ref[...]
    low = lax.broadcasted_iota(jnp.int32, (c, LANES), 1) < HEAD_DIM

    def stack(x):
        return jnp.concatenate([jnp.where(low, x, jnp.zeros_like(x)), jnp.where(low, jnp.zeros_like(x), x)],
                               axis=0)

    for ch in range(q_ref.shape[0] // c):
        rows = pl.ds(ch * c, c)
        q = q_ref[rows, :].astype(F32)
        qf = q * _sigmoid(q)
        f = lb + (1.0 - lb) * _sigmoid(f_ref[rows, :].astype(F32))
        kk = 1.0 - f
        ex = jnp.exp(_dot3_stacked(dst, jnp.log(f)))
        v = i_ref[rows, :].astype(BF16)
        g = g_ref[rows, :].astype(F32)
        gate = g * _sigmoid(g)
        q_in = (qf * ex[0:c]).astype(BF16)
        k_st = (kk * ex[c:2 * c]).astype(BF16)
        dec = ex[c - 1:c]
        q_l = [(qf * ex[(2 + 2 * l) * c:(3 + 2 * l) * c]).astype(BF16) for l in range(n_lvl)]
        k_l = [(kk * ex[(3 + 2 * l) * c:(4 + 2 * l) * c]).astype(BF16) for l in range(n_lvl)]
        outs = []
        for p in range(n_pairs):
            ps = slice(p * LANES, (p + 1) * LANES)
            scores = mask_ref[0] * _dot_nt(stack(q_l[0][:, ps]), k_l[0][:, ps])
            for l in range(1, n_lvl):
                scores = scores + mask_ref[l] * _dot_nt(stack(q_l[l][:, ps]), k_l[l][:, ps])
            pv = jnp.dot(scores.astype(BF16), v[:, ps], preferred_element_type=F32)
            st = st_ref[p]
            o = jnp.where(low, pv[:c], pv[c:]) + _dot_nt(q_in[:, ps], st.astype(BF16))
            st_ref[p] = st * dec[:, ps] + bd * _dot_tn(v[:, ps], k_st[:, ps])
            outs.append(o * lax.rsqrt(_dot2_rhs(o * o, grp) + EPS) * gn)
        o_ref[rows, :] = (jnp.concatenate(outs, axis=-1) * gate).astype(o_ref.dtype)


def _hgrn2(proj, lb, out_norm, batch, seq):
    n = proj.shape[0]
    width = lb.shape[0]
    n_heads = width // HEAD_DIM
    rb = HGRN_ROWS
    spb = seq // rb
    dst, masks = _hgrn_constants()
    col = lambda j: pl.BlockSpec((rb, width), lambda b, s, j=j: (b * spb + s, j))
    full = lambda a: pl.BlockSpec(a.shape, lambda b, s: (0,) * a.ndim)
    lb2 = lb.reshape(1, width)
    gn = jnp.tile(out_norm, LANES // HEAD_DIM).reshape(1, LANES)
    dst = jnp.asarray(np.concatenate([dst, dst, dst], axis=1), BF16)
    masks = jnp.asarray(np.concatenate([masks, masks], axis=1), F32)
    lane = np.arange(LANES)
    bd = jnp.asarray((lane[:, None] // HEAD_DIM) == (lane[None, :] // HEAD_DIM), F32)
    grp = _group_mean_matrix()
    return pl.pallas_call(
        _hgrn_kernel,
        grid=(batch, spb),
        in_specs=[col(0), col(1), col(2), col(3), full(lb2), full(gn), full(dst), full(masks),
                  full(bd), full(grp)],
        out_specs=pl.BlockSpec((rb, width), lambda b, s: (b * spb + s, 0)),
        out_shape=jax.ShapeDtypeStruct((n, width), BF16),
        scratch_shapes=[pltpu.VMEM((n_heads // 2, LANES, LANES), F32)],
        compiler_params=pltpu.CompilerParams(
            dimension_semantics=("parallel", "arbitrary"), vmem_limit_bytes=VMEM_LIMIT),
    )(proj, proj, proj, proj, lb2, gn, dst, masks, bd, grp)


def _fox_prep_kernel(q_ref, k_ref, v_ref, gate_ref, bias_ref, gq_ref, gk_ref, tril_ref,
                     sq_ref, sk_ref, cq_ref, ck_ref, cv_ref, grp_ref,
                     qa_ref, ka_ref, va_ref, carry_ref):
    n_heads = qa_ref.shape[1]
    tm = q_ref.shape[0]

    @pl.when(pl.program_id(1) == 0)
    def _():
        carry_ref[...] = jnp.zeros_like(carry_ref)

    z = gate_ref[...] + bias_ref[...]
    ls = -(jnp.maximum(-z, 0.0) + jnp.log(1.0 + jnp.exp(-jnp.abs(z))))
    cum = _dot3(tril_ref[...], ls) + carry_ref[...]
    carry_ref[...] = cum[tm - 1:tm]
    cum = cum * LOG2E

    c3 = jnp.concatenate(_split3(cum), axis=-1)
    ext_q = jnp.dot(c3, sq_ref[...], preferred_element_type=F32) + cq_ref[...]
    ext_k = jnp.dot(c3, sk_ref[...], preferred_element_type=F32) + ck_ref[...]

    lane = lax.broadcasted_iota(jnp.int32, (tm, LANES), 1)
    low_half = lane < HEAD_DIM
    grp = grp_ref[...]
    scale = HEAD_DIM ** -0.5 * LOG2E
    for c in range(n_heads // 2):
        cols = slice(c * LANES, (c + 1) * LANES)
        q = q_ref[:, cols].astype(F32)
        k = k_ref[:, cols].astype(F32)
        v = v_ref[:, cols]
        qn = q * lax.rsqrt(_dot2_rhs(q * q, grp) + EPS) * gq_ref[...] * scale
        kn = k * lax.rsqrt(_dot2_rhs(k * k, grp) + EPS) * gk_ref[...]
        for par in range(2):
            h = 2 * c + par
            data = low_half if par == 0 else jnp.logical_not(low_half)
            ext = slice(h * LANES, (h + 1) * LANES)
            qa_ref[0, h] = jnp.where(data, qn, ext_q[:, ext]).astype(BF16)
            ka_ref[0, h] = jnp.where(data, kn, ext_k[:, ext]).astype(BF16)
            va_ref[0, h] = jnp.where(data, v, cv_ref[par:par + 1, :].astype(BF16))


def _fox_layout_constants(n_heads):
    sq = np.zeros((3 * LANES, n_heads * LANES), np.float32)
    sk = np.zeros((3 * LANES, n_heads * LANES), np.float32)
    cq = np.zeros((1, n_heads * LANES), np.float32)
    ck = np.zeros((1, n_heads * LANES), np.float32)
    cv = np.zeros((2, LANES), np.float32)
    for h in range(n_heads):
        x0 = h * LANES + (HEAD_DIM if h % 2 == 0 else 0)
        for t in range(3):
            sq[t * LANES + h, x0 + t] = 1.0
            sk[t * LANES + h, x0 + 3 + t] = -1.0
        cq[0, x0 + 3:x0 + 6] = 1.0
        ck[0, x0:x0 + 3] = 1.0
    cv[0, HEAD_DIM] = 1.0
    cv[1, 0] = 1.0
    return (jnp.asarray(sq, BF16), jnp.asarray(sk, BF16), jnp.asarray(cq), jnp.asarray(ck), jnp.asarray(cv))


def _group_mean_matrix():
    lane = np.arange(LANES)
    return jnp.asarray(((lane[:, None] // HEAD_DIM) == (lane[None, :] // HEAD_DIM)) / HEAD_DIM, BF16)


def _fox_prep(proj, gates, f_bias, q_norm, k_norm, batch, seq, col0):
    width = 512
    n_heads = width // HEAD_DIM
    tm = PREP_ROWS
    spb = seq // tm
    col = lambda j: pl.BlockSpec((tm, width), lambda b, s, j=j: (b * spb + s, col0 + j))
    full = lambda a: pl.BlockSpec(a.shape, lambda b, s: (0,) * a.ndim)
    bias = jnp.zeros((1, LANES), F32).at[0, :n_heads].set(f_bias)
    gq = jnp.tile(q_norm, LANES // HEAD_DIM).reshape(1, LANES)
    gk = jnp.tile(k_norm, LANES // HEAD_DIM).reshape(1, LANES)
    tril = jnp.asarray(np.tril(np.ones((tm, tm), np.float32)), BF16)
    consts = _fox_layout_constants(n_heads) + (_group_mean_matrix(),)
    out = jax.ShapeDtypeStruct((batch, n_heads, seq, LANES), BF16)
    ospec = pl.BlockSpec((1, n_heads, tm, LANES), lambda b, s: (b, 0, s, 0))
    return pl.pallas_call(
        _fox_prep_kernel,
        grid=(batch, spb),
        in_specs=[col(0), col(1), col(2),
                  pl.BlockSpec((tm, LANES), lambda b, s: (b * spb + s, 0)),
                  full(bias), full(gq), full(gk), full(tril)] + [full(a) for a in consts],
        out_specs=[ospec, ospec, ospec],
        out_shape=[out, out, out],
        scratch_shapes=[pltpu.VMEM((1, LANES), F32)],
        compiler_params=pltpu.CompilerParams(
            dimension_semantics=("parallel", "arbitrary"), vmem_limit_bytes=VMEM_LIMIT),
    )(proj, proj, proj, gates, bias, gq, gk, tril, *consts)


def _tri_tables(nq):
    qi = [q for q in range(nq) for _ in range(q + 1)]
    ki = [k for q in range(nq) for k in range(q + 1)]
    return jnp.asarray(qi, jnp.int32), jnp.asarray(ki, jnp.int32)


def _fox_attn_kernel(qt_ref, kt_ref, q_ref, k_ref, v_ref, o_ref, m_ref, acc_ref):
    p_idx = pl.program_id(2)
    qi = qt_ref[p_idx]
    ki = kt_ref[p_idx]
    hp = q_ref.shape[1]
    t = q_ref.shape[2]

    @pl.when(ki == 0)
    def _():
        m_ref[...] = jnp.full_like(m_ref, -jnp.inf)
        acc_ref[...] = jnp.zeros_like(acc_ref)

    def step(masked):
        for h in range(hp):
            s = _dot_nt(q_ref[0, h], k_ref[0, h])
            if masked:
                row = lax.broadcasted_iota(jnp.int32, (t, t), 0)
                colm = lax.broadcasted_iota(jnp.int32, (t, t), 1)
                s = jnp.where(colm <= row, s, -jnp.inf)
            m_old = m_ref[h]
            m_new = jnp.maximum(m_old, jnp.max(s, axis=-1, keepdims=True))
            p = jnp.exp2(s - m_new)
            acc_ref[h] = (jnp.exp2(m_old - m_new) * acc_ref[h]
                          + jnp.dot(p.astype(BF16), v_ref[0, h], preferred_element_type=F32))
            m_ref[h] = m_new

    @pl.when(ki < qi)
    def _():
        step(False)

    @pl.when(ki == qi)
    def _():
        step(True)
        _fox_finalize(acc_ref, o_ref)


def _fox_finalize(acc_ref, o_ref):
    a0 = acc_ref[0]
    a1 = acc_ref[1]
    lane = lax.broadcasted_iota(jnp.int32, a0.shape, 1)
    o_ref[0] = jnp.where(lane < HEAD_DIM, a0 / a0[:, HEAD_DIM:HEAD_DIM + 1], a1 / a1[:, 0:1]).astype(o_ref.dtype)


def _tile_plan(n_sub, diagonal):
    plan = []
    for qb in range(n_sub):
        if not diagonal:
            plan.append((qb, 0, n_sub, False))
        else:
            if qb > 0:
                plan.append((qb, 0, qb, False))
            plan.append((qb, qb, qb + 1, True))
    return plan


def _fox_fast_kernel(qt_ref, kt_ref, q_ref, k_ref, v_ref, o_ref, acc_ref):
    p_idx = pl.program_id(2)
    qi = qt_ref[p_idx]
    ki = kt_ref[p_idx]
    hp = q_ref.shape[1]
    sb = ATTN_FAST_SUB
    n_sub = q_ref.shape[2] // sb

    @pl.when(ki == 0)
    def _():
        acc_ref[...] = jnp.zeros_like(acc_ref)

    def tile(diagonal):
        for h in range(hp):
            for qb, k0, k1, masked in _tile_plan(n_sub, diagonal):
                rows = pl.ds(qb * sb, sb)
                cols = pl.ds(k0 * sb, (k1 - k0) * sb)
                s = _dot_nt(q_ref[0, h, rows, :], k_ref[0, h, cols, :])
                if masked:
                    row = lax.broadcasted_iota(jnp.int32, (sb, sb), 0)
                    colm = lax.broadcasted_iota(jnp.int32, (sb, sb), 1)
                    s = jnp.where(colm <= row, s, -jnp.inf)
                p = jnp.exp2(s).astype(BF16)
                acc_ref[h, rows, :] += jnp.dot(p, v_ref[0, h, cols, :], preferred_element_type=F32)

    @pl.when(ki < qi)
    def _():
        tile(False)

    @pl.when(ki == qi)
    def _():
        tile(True)
        _fox_finalize(acc_ref, o_ref)


def _fox_attention(qa, ka, va, fast):
    batch, n_heads, seq, _ = qa.shape
    t = ATTN_FAST_TILE if fast else ATTN_TILE
    hp = 2
    nq = seq // t
    qt, kt = _tri_tables(nq)
    qspec = pl.BlockSpec((1, hp, t, LANES), lambda b, g, p, qt, kt: (b, g, qt[p], 0))
    kspec = pl.BlockSpec((1, hp, t, LANES), lambda b, g, p, qt, kt: (b, g, kt[p], 0))
    scratch = [pltpu.VMEM((hp, t, LANES), F32)]
    if not fast:
        scratch = [pltpu.VMEM((hp, t, 1), F32)] + scratch
    return pl.pallas_call(
        _fox_fast_kernel if fast else _fox_attn_kernel,
        grid_spec=pltpu.PrefetchScalarGridSpec(
            num_scalar_prefetch=2,
            grid=(batch, n_heads // hp, int(qt.shape[0])),
            in_specs=[qspec, kspec, kspec],
            out_specs=pl.BlockSpec((1, t, hp * HEAD_DIM), lambda b, g, p, qt, kt: (b, qt[p], g)),
            scratch_shapes=scratch),
        out_shape=jax.ShapeDtypeStruct((batch, seq, n_heads * HEAD_DIM), BF16),
        compiler_params=pltpu.CompilerParams(
            dimension_semantics=("parallel", "parallel", "arbitrary"), vmem_limit_bytes=VMEM_LIMIT),
    )(qt, kt, qa, ka, va)


def _logit_bound(q_gain, k_gain):
    return HEAD_DIM ** 0.5 * jnp.max(jnp.abs(q_gain)) * jnp.max(jnp.abs(k_gain))


def _diff_prep_kernel(q_ref, k_ref, v_ref, pos_ref, invf_ref, gq_ref, gk_ref, grp_ref,
                      qm_ref, k2_ref, va_ref):
    n_heads = k2_ref.shape[1]
    tm = q_ref.shape[0]
    ang = pos_ref[...].astype(F32) * invf_ref[...]
    lane = lax.broadcasted_iota(jnp.int32, (tm, LANES), 1)
    first = (lane % HEAD_DIM) < (HEAD_DIM // 2)
    cs = jnp.cos(ang)
    sn = jnp.sin(ang)
    sn = jnp.where(first, -sn, sn)
    grp = grp_ref[...]
    scale = HEAD_DIM ** -0.5 * LOG2E
    zero = jnp.zeros((tm, LANES), F32)
    onecol = jnp.where(lane == 0, 1.0, 0.0).astype(BF16)

    def norm_rope(x, gain):
        ms = _dot2_rhs(x * x, grp)
        y = x * lax.rsqrt(ms + EPS) * gain
        yr = jnp.where(first, pltpu.roll(y, LANES - HEAD_DIM // 2, 1), pltpu.roll(y, HEAD_DIM // 2, 1))
        return y * cs + yr * sn

    for h in range(n_heads):
        cols = slice(h * LANES, (h + 1) * LANES)
        qr = norm_rope(q_ref[:, cols].astype(F32), gq_ref[...]) * scale
        kr = norm_rope(k_ref[:, cols].astype(F32), gk_ref[...])
        qm_ref[0, h, 0] = jnp.where(lane < HEAD_DIM, qr, zero).astype(BF16)
        qm_ref[0, h, 1] = jnp.where(lane < HEAD_DIM, zero, qr).astype(BF16)
        k2_ref[0, h] = kr.astype(BF16)
        va_ref[0, h] = jnp.concatenate([v_ref[:, cols].astype(BF16), onecol], axis=-1)


def _diff_prep(proj, positions, q_norm, k_norm, batch, seq):
    n = proj.shape[0]
    width = proj.shape[1] // 3
    n_heads = width // LANES
    tm = PREP_ROWS
    spb = seq // tm
    col = lambda j: pl.BlockSpec((tm, width), lambda b, s, j=j: (b * spb + s, j))
    full = lambda a: pl.BlockSpec(a.shape, lambda b, s: (0,) * a.ndim)
    half = HEAD_DIM // 2
    inv_freq = ROPE_THETA ** (-jnp.arange(half, dtype=F32) / half)
    invf = jnp.tile(inv_freq, LANES // half).reshape(1, LANES)
    gq = jnp.tile(q_norm, LANES // HEAD_DIM).reshape(1, LANES)
    gk = jnp.tile(k_norm, LANES // HEAD_DIM).reshape(1, LANES)
    grp = _group_mean_matrix()
    pos = positions.reshape(n, 1).astype(jnp.int32)
    return pl.pallas_call(
        _diff_prep_kernel,
        grid=(batch, spb),
        in_specs=[col(0), col(1), col(2),
                  pl.BlockSpec((tm, 1), lambda b, s: (b * spb + s, 0)),
                  full(invf), full(gq), full(gk), full(grp)],
        out_specs=[pl.BlockSpec((1, n_heads, 2, tm, LANES), lambda b, s: (b, 0, 0, s, 0)),
                   pl.BlockSpec((1, n_heads, tm, LANES), lambda b, s: (b, 0, s, 0)),
                   pl.BlockSpec((1, n_heads, tm, 2 * LANES), lambda b, s: (b, 0, s, 0))],
        out_shape=[jax.ShapeDtypeStruct((batch, n_heads, 2, seq, LANES), BF16),
                   jax.ShapeDtypeStruct((batch, n_heads, seq, LANES), BF16),
                   jax.ShapeDtypeStruct((batch, n_heads, seq, 2 * LANES), BF16)],
        compiler_params=pltpu.CompilerParams(
            dimension_semantics=("parallel", "parallel"), vmem_limit_bytes=VMEM_LIMIT),
    )(proj, proj, proj, pos, invf, gq, gk, grp)


def _diff_attn_kernel(lambda_init, qt_ref, kt_ref, q_ref, k_ref, v_ref, lam_ref, sub_ref,
                      o_ref, m_ref, acc_ref):
    p_idx = pl.program_id(2)
    qi = qt_ref[p_idx]
    ki = kt_ref[p_idx]
    t = k_ref.shape[2]
    dv = o_ref.shape[2]

    @pl.when(ki == 0)
    def _():
        m_ref[...] = jnp.full_like(m_ref, -jnp.inf)
        acc_ref[...] = jnp.zeros_like(acc_ref)

    def step(masked):
        for m in range(2):
            s = _dot_nt(q_ref[0, 0, m], k_ref[0, 0])
            if masked:
                row = lax.broadcasted_iota(jnp.int32, (t, t), 0) // CHUNK
                colm = lax.broadcasted_iota(jnp.int32, (t, t), 1) // CHUNK
                s = jnp.where(colm <= row, s, -jnp.inf)
            m_old = m_ref[m]
            m_new = jnp.maximum(m_old, jnp.max(s, axis=-1, keepdims=True))
            p = jnp.exp2(s - m_new)
            acc_ref[m] = (jnp.exp2(m_old - m_new) * acc_ref[m]
                          + jnp.dot(p.astype(BF16), v_ref[0, 0], preferred_element_type=F32))
            m_ref[m] = m_new

    @pl.when(ki < qi)
    def _():
        step(False)

    @pl.when(ki == qi)
    def _():
        step(True)
        _diff_finalize(lambda_init, acc_ref, lam_ref, sub_ref, o_ref)


def _diff_finalize(lambda_init, acc_ref, lam_ref, sub_ref, o_ref):
    dv = o_ref.shape[2]
    lp = lam_ref[...]
    lam = (jnp.exp(jnp.sum(lp[0:1] * lp[1:2], axis=-1, keepdims=True))
           - jnp.exp(jnp.sum(lp[2:3] * lp[3:4], axis=-1, keepdims=True)) + lambda_init)
    a0 = acc_ref[0]
    a1 = acc_ref[1]
    o = a0[:, :dv] / a0[:, dv:dv + 1] - lam * (a1[:, :dv] / a1[:, dv:dv + 1])
    ms = jnp.mean(o * o, axis=-1, keepdims=True)
    o_ref[0] = ((o * lax.rsqrt(ms + EPS) * sub_ref[...]) * (1.0 - lambda_init)).astype(o_ref.dtype)


def _diff_fast_kernel(lambda_init, qt_ref, kt_ref, q_ref, k_ref, v_ref, lam_ref, sub_ref, o_ref, acc_ref):
    p_idx = pl.program_id(2)
    qi = qt_ref[p_idx]
    ki = kt_ref[p_idx]
    sb = ATTN_FAST_SUB
    n_sub = k_ref.shape[2] // sb

    @pl.when(ki == 0)
    def _():
        acc_ref[...] = jnp.zeros_like(acc_ref)

    def tile(diagonal):
        for m in range(2):
            for qb, k0, k1, masked in _tile_plan(n_sub, diagonal):
                rows = pl.ds(qb * sb, sb)
                cols = pl.ds(k0 * sb, (k1 - k0) * sb)
                s = _dot_nt(q_ref[0, 0, m, rows, :], k_ref[0, 0, cols, :])
                if masked:
                    row = lax.broadcasted_iota(jnp.int32, (sb, sb), 0) // CHUNK
                    colm = lax.broadcasted_iota(jnp.int32, (sb, sb), 1) // CHUNK
                    s = jnp.where(colm <= row, s, -jnp.inf)
                p = jnp.exp2(s).astype(BF16)
                acc_ref[m, rows, :] += jnp.dot(p, v_ref[0, 0, cols, :], preferred_element_type=F32)

    @pl.when(ki < qi)
    def _():
        tile(False)

    @pl.when(ki == qi)
    def _():
        tile(True)
        _diff_finalize(lambda_init, acc_ref, lam_ref, sub_ref, o_ref)


def _diff_attention(qm, k2, va, lam_params, subln, lambda_init, fast):
    batch, n_heads, seq, _ = k2.shape
    dv = va.shape[3] // 2
    t = ATTN_FAST_TILE if fast else ATTN_TILE
    nq = seq // t
    qt, kt = _tri_tables(nq)
    lamp = jnp.zeros((8, LANES), F32).at[:4, :HEAD_DIM].set(lam_params)
    sub = subln.reshape(1, dv)
    scratch = [pltpu.VMEM((2, t, 2 * dv), F32)]
    if not fast:
        scratch = [pltpu.VMEM((2, t, 1), F32)] + scratch
    return pl.pallas_call(
        functools.partial(_diff_fast_kernel if fast else _diff_attn_kernel, lambda_init),
        grid_spec=pltpu.PrefetchScalarGridSpec(
            num_scalar_prefetch=2,
            grid=(batch, n_heads, int(qt.shape[0])),
            in_specs=[pl.BlockSpec((1, 1, 2, t, LANES), lambda b, h, p, qt, kt: (b, h, 0, qt[p], 0)),
                      pl.BlockSpec((1, 1, t, LANES), lambda b, h, p, qt, kt: (b, h, kt[p], 0)),
                      pl.BlockSpec((1, 1, t, 2 * dv), lambda b, h, p, qt, kt: (b, h, kt[p], 0)),
                      pl.BlockSpec((8, LANES), lambda b, h, p, qt, kt: (0, 0)),
                      pl.BlockSpec((1, dv), lambda b, h, p, qt, kt: (0, 0))],
            out_specs=pl.BlockSpec((1, t, dv), lambda b, h, p, qt, kt: (b, qt[p], h)),
            scratch_shapes=scratch),
        out_shape=jax.ShapeDtypeStruct((batch, seq, n_heads * dv), BF16),
        compiler_params=pltpu.CompilerParams(
            dimension_semantics=("parallel", "parallel", "arbitrary"), vmem_limit_bytes=VMEM_LIMIT),
    )(qt, kt, qm, k2, va, lamp, sub)


def _store_token_tiles(ref, value):
    t, width = value.shape
    s = width // LANES
    for j in range(s):
        ref[pl.ds(j, t, stride=s), :] = value[:, j * LANES:(j + 1) * LANES]


def _load_token_tiles(ref, first_row, t, s):
    return jnp.concatenate([ref[pl.ds(first_row + j, t, stride=s), :] for j in range(s)], axis=-1)


def _router_kernel(h_ref, g_ref, whi_ref, wlo_ref, xn_ref, route_ref):
    x = h_ref[...]
    tm = x.shape[0]
    xn = x * lax.rsqrt(jnp.mean(x * x, axis=-1, keepdims=True) + EPS) * g_ref[...]
    _store_token_tiles(xn_ref, xn)
    xh = xn.astype(BF16)
    xl = (xn - xh.astype(F32)).astype(BF16)
    d = lambda a, b: jnp.dot(a, b[...], preferred_element_type=F32)
    logits = d(xh, whi_ref) + (d(xl, whi_ref) + d(xh, wlo_ref))
    lane = lax.broadcasted_iota(jnp.int32, (tm, LANES), 1)
    neg = jnp.full((tm, LANES), -jnp.inf, F32)
    big = jnp.full((tm, LANES), LANES, jnp.int32)

    def top1(vals):
        m = jnp.max(vals, axis=-1, keepdims=True)
        idx = jnp.min(jnp.where(vals == m, lane, big), axis=-1, keepdims=True)
        return m, idx

    grp_logits = jnp.where(lane < N_GROUPS, logits, neg)
    mg, gidx = top1(grp_logits)
    p_g = 1.0 / jnp.sum(jnp.exp(grp_logits - mg), axis=-1, keepdims=True)
    e_lane = lane - N_GROUPS
    in_grp = (e_lane >= gidx * EXPERTS_PER_GROUP) & (e_lane < (gidx + 1) * EXPERTS_PER_GROUP)
    sel = jnp.where(in_grp, logits, neg)
    m1, i1 = top1(sel)
    m2, i2 = top1(jnp.where(lane == i1, neg, sel))
    r = jnp.exp(m2 - m1)
    w1 = p_g / (1.0 + r)
    w2 = p_g * r / (1.0 + r)
    zero = jnp.zeros((tm, LANES), F32)
    route_ref[...] = jnp.where(lane == 0, (i1 - N_GROUPS).astype(F32),
                     jnp.where(lane == 1, (i2 - N_GROUPS).astype(F32),
                     jnp.where(lane == 2, w1, jnp.where(lane == 3, w2, zero))))


def _router(h, gain, w_group, w_expert):
    n, d = h.shape
    tm = ROUTER_ROWS
    wr = jnp.zeros((d, LANES), F32)
    wr = wr.at[:, :N_GROUPS].set(w_group)
    wr = wr.at[:, N_GROUPS:N_GROUPS + N_EXPERTS].set(
        jnp.transpose(w_expert, (1, 0, 2)).reshape(d, N_EXPERTS))
    w_hi = wr.astype(BF16)
    w_lo = (wr - w_hi.astype(F32)).astype(BF16)
    return pl.pallas_call(
        _router_kernel,
        grid=(n // tm,),
        in_specs=[pl.BlockSpec((tm, d), lambda i: (i, 0)),
                  pl.BlockSpec((1, d), lambda i: (0, 0)),
                  pl.BlockSpec((d, LANES), lambda i: (0, 0)),
                  pl.BlockSpec((d, LANES), lambda i: (0, 0))],
        out_specs=[pl.BlockSpec((tm * d // LANES, LANES), lambda i: (i, 0)),
                   pl.BlockSpec((tm, LANES), lambda i: (i, 0))],
        out_shape=[jax.ShapeDtypeStruct((n * d // LANES, LANES), F32),
                   jax.ShapeDtypeStruct((n, LANES), F32)],
        compiler_params=pltpu.CompilerParams(
            dimension_semantics=("parallel",), vmem_limit_bytes=VMEM_LIMIT),
    )(h, gain.reshape(1, d), w_hi, w_lo)


def _dispatch_tables(expert_ids, tm):
    n = expert_ids.shape[0]
    pairs = expert_ids.reshape(-1)
    n_pairs = pairs.shape[0]
    n_tiles = n_pairs // tm + N_EXPERTS
    onehot = (pairs[:, None] == jnp.arange(N_EXPERTS, dtype=jnp.int32)[None, :]).astype(jnp.int32)
    csum = jnp.cumsum(onehot, axis=0)
    rank = jnp.sum((csum - onehot) * onehot, axis=1)
    counts = csum[-1]
    padded = ((counts + tm - 1) // tm) * tm
    ends = jnp.cumsum(padded)
    starts = ends - padded
    pos = jnp.sum(onehot * starts[None, :], axis=1) + rank
    slot_token = jnp.zeros((n_tiles * tm,), jnp.int32).at[pos].set(
        jnp.arange(n_pairs, dtype=jnp.int32) // 2)
    tile_start = jnp.arange(n_tiles, dtype=jnp.int32) * tm
    tile_expert = jnp.minimum(jnp.sum(tile_start[:, None] >= ends[None, :], axis=1),
                              N_EXPERTS - 1).astype(jnp.int32)
    n_valid = (ends[-1] // tm).astype(jnp.int32).reshape(1)
    return slot_token.reshape(n_tiles, 1, tm), tile_expert, n_valid, pos.reshape(n, 2).astype(jnp.int32)


def _token_copy(src_hbm, tok, dst_ref, r, sem):
    src = src_hbm.at[pl.ds(pl.multiple_of(tok * TILE_ROWS, TILE_ROWS), TILE_ROWS)]
    first = r * TILE_ROWS if isinstance(r, int) else pl.multiple_of(r * TILE_ROWS, TILE_ROWS)
    return pltpu.make_async_copy(src, dst_ref.at[pl.ds(first, TILE_ROWS)], sem)


def _gather_tokens(src_hbm, idx_ref, dst_ref, sem, n_tokens):
    def body(r, carry):
        _token_copy(src_hbm, idx_ref[0, 0, r], dst_ref, r, sem).start()
        return carry
    lax.fori_loop(0, n_tokens, body, 0, unroll=8)


def _wait_tokens(src_hbm, dst_ref, sem):
    pltpu.make_async_copy(src_hbm.at[pl.ds(0, dst_ref.shape[0])], dst_ref, sem).wait()


def _moe_kernel(te_ref, nv_ref, tok_ref, tok_next_ref, wg_ref, wu_ref, wd_ref, x_hbm,
                o_ref, xbuf, wg_b, wu_b, wd_b, sems):
    i = pl.program_id(0)
    tm = xbuf.shape[1] // TILE_ROWS
    n_valid = nv_ref[0]
    slot = i % 2
    n_phases = 4
    per_phase = tm // n_phases

    @pl.when(i == 0)
    def _():
        _gather_tokens(x_hbm, tok_ref, xbuf.at[0], sems.at[0], tm)

    new_expert = jnp.logical_or(i == 0, te_ref[i] != te_ref[jnp.maximum(i - 1, 0)])

    @pl.when(jnp.logical_and(i < n_valid, new_expert))
    def _():
        wg_b[...] = wg_ref[0].astype(BF16)
        wu_b[...] = wu_ref[0].astype(BF16)
        wd_b[...] = wd_ref[0].astype(BF16)

    def prefetch(phase):
        for r in range(phase * per_phase, (phase + 1) * per_phase):
            _token_copy(x_hbm, tok_next_ref[0, 0, r], xbuf.at[1 - slot], r,
                        sems.at[1 - slot]).start(priority=r % 2)

    @pl.when(i < n_valid)
    def _():
        _wait_tokens(x_hbm, xbuf.at[slot], sems.at[slot])
        x = _load_token_tiles(xbuf.at[slot], 0, tm, TILE_ROWS).astype(BF16)
        prefetch(0)
        g = jnp.dot(x, wg_b[...], preferred_element_type=F32)
        prefetch(1)
        u = jnp.dot(x, wu_b[...], preferred_element_type=F32)
        prefetch(2)
        hid = (g * _sigmoid(g) * u).astype(BF16)
        _store_token_tiles(o_ref, jnp.dot(hid, wd_b[...], preferred_element_type=F32))
        prefetch(3)

    @pl.when(i == n_valid)
    def _():
        _wait_tokens(x_hbm, xbuf.at[slot], sems.at[slot])

    @pl.when(i >= n_valid)
    def _():
        o_ref[...] = jnp.zeros_like(o_ref)


def _moe_experts(xn, slot_token, tile_expert, n_valid, w_gate, w_up, w_down):
    n_exp, d, f = w_gate.shape
    assert d == TILE_ROWS * LANES
    n_tiles, _, tm = slot_token.shape
    last = n_tiles - 1
    return pl.pallas_call(
        _moe_kernel,
        grid_spec=pltpu.PrefetchScalarGridSpec(
            num_scalar_prefetch=2,
            grid=(n_tiles,),
            in_specs=[
                pl.BlockSpec((1, 1, tm), lambda i, te, nv: (i, 0, 0), memory_space=pltpu.SMEM),
                pl.BlockSpec((1, 1, tm), lambda i, te, nv: (jnp.minimum(i + 1, last), 0, 0),
                             memory_space=pltpu.SMEM),
                pl.BlockSpec((1, d, f), lambda i, te, nv: (te[i], 0, 0)),
                pl.BlockSpec((1, d, f), lambda i, te, nv: (te[i], 0, 0)),
                pl.BlockSpec((1, f, d), lambda i, te, nv: (te[i], 0, 0)),
                pl.BlockSpec(memory_space=pl.ANY)],
            out_specs=pl.BlockSpec((tm * TILE_ROWS, LANES), lambda i, te, nv: (i, 0)),
            scratch_shapes=[pltpu.VMEM((2, tm * TILE_ROWS, LANES), F32),
                            pltpu.VMEM((d, f), BF16), pltpu.VMEM((d, f), BF16), pltpu.VMEM((f, d), BF16),
                            pltpu.SemaphoreType.DMA((2,))]),
        out_shape=jax.ShapeDtypeStruct((n_tiles * tm * TILE_ROWS, LANES), F32),
        compiler_params=pltpu.CompilerParams(
            dimension_semantics=("arbitrary",), vmem_limit_bytes=VMEM_LIMIT),
    )(tile_expert, n_valid, slot_token, slot_token, w_gate, w_up, w_down, xn)


def _combine_kernel(pos_ref, pos_next_ref, h_ref, route_ref, y_hbm, o_ref, ybuf, sems):
    i = pl.program_id(0)
    n_steps = pl.num_programs(0)
    tokens = ybuf.shape[1] // TILE_ROWS
    slot = i % 2

    @pl.when(i == 0)
    def _():
        _gather_tokens(y_hbm, pos_ref, ybuf.at[0], sems.at[0], tokens)

    @pl.when(i + 1 < n_steps)
    def _():
        for r in range(tokens):
            _token_copy(y_hbm, pos_next_ref[0, 0, r], ybuf.at[1 - slot], r,
                        sems.at[1 - slot]).start(priority=r % 2)

    _wait_tokens(y_hbm, ybuf.at[slot], sems.at[slot])
    tc = tokens // 2
    w = route_ref[...]
    first = _load_token_tiles(ybuf.at[slot], 0, tc, TILE_ROWS)
    second = _load_token_tiles(ybuf.at[slot], tc * TILE_ROWS, tc, TILE_ROWS)
    o_ref[...] = h_ref[...] + w[:, 2:3] * first + w[:, 3:4] * second


def _moe_combine(h, route, pos, y_sorted):
    n, d = h.shape
    tc = COMBINE_ROWS
    steps = n // tc
    pos_tab = jnp.transpose(pos.reshape(steps, tc, 2), (0, 2, 1)).reshape(steps, 1, 2 * tc)
    return pl.pallas_call(
        _combine_kernel,
        grid=(steps,),
        in_specs=[pl.BlockSpec((1, 1, 2 * tc), lambda i: (i, 0, 0), memory_space=pltpu.SMEM),
                  pl.BlockSpec((1, 1, 2 * tc), lambda i: (jnp.minimum(i + 1, steps - 1), 0, 0),
                               memory_space=pltpu.SMEM),
                  pl.BlockSpec((tc, d), lambda i: (i, 0)),
                  pl.BlockSpec((tc, LANES), lambda i: (i, 0)),
                  pl.BlockSpec(memory_space=pl.ANY)],
        out_specs=pl.BlockSpec((tc, d), lambda i: (i, 0)),
        out_shape=jax.ShapeDtypeStruct((n, d), F32),
        scratch_shapes=[pltpu.VMEM((2, 2 * tc * TILE_ROWS, LANES), F32), pltpu.SemaphoreType.DMA((2,))],
        compiler_params=pltpu.CompilerParams(
            dimension_semantics=("arbitrary",), vmem_limit_bytes=VMEM_LIMIT),
    )(pos_tab, pos_tab, h, route, y_sorted)


def _moe_layer(h, layer, gain, w_group, w_expert, w_gate, w_up, w_down):
    d = h.shape[1]
    f = w_gate.shape[-1]
    xn, route = _router(h, gain, w_group, w_expert)
    expert_ids = route[:, :2].astype(jnp.int32)
    slot_token, tile_expert, n_valid, pos = _dispatch_tables(expert_ids, MOE_TM)
    y_sorted = _moe_experts(xn, slot_token, tile_expert + layer * N_EXPERTS, n_valid,
                            w_gate.reshape(-1, d, f), w_up.reshape(-1, d, f), w_down.reshape(-1, f, d))
    return _moe_combine(h, route, pos, y_sorted)


def _even_layer(h, batch, seq, gain, w_in, w_out, lb, f_bias, out_norm, q_norm, k_norm):
    d = h.shape[1]
    n_main = w_in.shape[1] - f_bias.shape[0]
    w_main = w_in[:, :n_main].astype(BF16)
    w_gate = jnp.zeros((d, LANES), F32).at[:, :f_bias.shape[0]].set(w_in[:, n_main:])
    proj, gates = _norm_proj(h, gain, w_main, w_gate)
    o_a = _hgrn2(proj, lb, out_norm, batch, seq)
    qa, ka, va = _fox_prep(proj, gates, f_bias, q_norm, k_norm, batch, seq, col0=4)
    o_b = lax.cond(_logit_bound(q_norm, k_norm) <= LOGIT_BOUND_MAX,
                   functools.partial(_fox_attention, fast=True),
                   functools.partial(_fox_attention, fast=False), qa, ka, va).reshape(batch * seq, -1)
    wo = w_out.astype(BF16)
    ka_dim = o_a.shape[1]
    return _proj_residual([(o_a, wo[:ka_dim]), (o_b, wo[ka_dim:])], h)


def _odd_layer(h, positions, batch, seq, gain, w_in, w_out, q_norm, k_norm, lam_params, subln, lambda_init):
    proj = _norm_proj(h, gain, w_in.astype(BF16))
    qm, k2, va = _diff_prep(proj, positions, q_norm, k_norm, batch, seq)
    attn = lambda fast: functools.partial(_diff_attention, lam_params=lam_params, subln=subln,
                                          lambda_init=lambda_init, fast=fast)
    o = lax.cond(_logit_bound(q_norm, k_norm) <= LOGIT_BOUND_MAX,
                 attn(True), attn(False), qm, k2, va).reshape(batch * seq, -1)
    return _proj_residual([(o, w_out.astype(BF16))], h)


def kernel(x, positions, hgrn_lb_logits, norm_mix, norm_ffn, even_w_in, even_w_out, fox_f_bias,
           hgrn_out_norm, fox_q_norm, fox_k_norm, odd_w_in, odd_w_out, diff_q_norm, diff_k_norm,
           diff_lambda_q1, diff_lambda_k1, diff_lambda_q2, diff_lambda_k2, diff_subln,
           moe_router_group, moe_router_expert, moe_w_gate, moe_w_up, moe_w_down):
    batch, seq, d = x.shape
    depth = norm_mix.shape[0]
    lower_bounds = jnp.cumsum(jax.nn.softmax(hgrn_lb_logits.astype(F32), axis=0), axis=0)
    h = x.reshape(batch * seq, d)
    for layer in range(depth):
        j = layer // 2
        if layer % 2 == 0:
            h = _even_layer(h, batch, seq, norm_mix[layer], even_w_in[j], even_w_out[j], lower_bounds[j],
                            fox_f_bias[j], hgrn_out_norm[j], fox_q_norm[j], fox_k_norm[j])
        else:
            lambda_init = 0.8 - 0.6 * math.exp(-0.3 * layer)
            lam_params = jnp.stack([diff_lambda_q1[j], diff_lambda_k1[j],
                                    diff_lambda_q2[j], diff_lambda_k2[j]]).astype(F32)
            h = _odd_layer(h, positions, batch, seq, norm_mix[layer], odd_w_in[j], odd_w_out[j],
                           diff_q_norm[j], diff_k_norm[j], lam_params, diff_subln[j], lambda_init)
        h = _moe_layer(h, layer, norm_ffn[layer], moe_router_group[layer], moe_router_expert[layer],
                       moe_w_gate, moe_w_up, moe_w_down)
    return h.reshape(batch, seq, d)
```

```python
import functools
import math

import numpy as np
import jax
import jax.numpy as jnp
from jax import lax
from jax.experimental import pallas as pl
from jax.experimental.pallas import tpu as pltpu

F32 = jnp.float32
BF16 = jnp.bfloat16

EPS = 1e-6
ROPE_THETA = 10000.0
CHUNK = 64
HEAD_DIM = 64
N_GROUPS = 4
EXPERTS_PER_GROUP = 8
N_EXPERTS = N_GROUPS * EXPERTS_PER_GROUP
LANES = 128
TILE_ROWS = 8

HGRN_CHUNK = 64
HGRN_ROWS = 256
ATTN_TILE = 512
ATTN_FAST_TILE = 2048
ATTN_FAST_SUB = 512
LOGIT_BOUND_MAX = 60.0
LOG2E = math.log2(math.e)
PREP_ROWS = 256
PROJ_TM = 256
ROUTER_ROWS = 256
MOE_TM = 256
COMBINE_ROWS = 256
VMEM_LIMIT = 56 * 1024 * 1024


def _split3(x):
    hi = x.astype(BF16)
    r1 = x - hi.astype(F32)
    mid = r1.astype(BF16)
    lo = (r1 - mid.astype(F32)).astype(BF16)
    return hi, mid, lo


def _dot3(const_bf16, x):
    hi, mid, lo = _split3(x)
    d = lambda b: jnp.dot(const_bf16, b, preferred_element_type=F32)
    return d(hi) + d(mid) + d(lo)


def _dot3_stacked(const3_bf16, x):
    return jnp.dot(const3_bf16, jnp.concatenate(_split3(x), axis=0), preferred_element_type=F32)


def _dot3_rhs(x, const_bf16):
    hi, mid, lo = _split3(x)
    d = lambda a: jnp.dot(a, const_bf16, preferred_element_type=F32)
    return d(hi) + d(mid) + d(lo)


def _dot2_rhs(x, const_bf16):
    hi = x.astype(BF16)
    lo = (x - hi.astype(F32)).astype(BF16)
    d = lambda a: jnp.dot(a, const_bf16, preferred_element_type=F32)
    return d(hi) + d(lo)


def _dot_nt(a, b):
    return lax.dot_general(a, b, (((1,), (1,)), ((), ())), preferred_element_type=F32)


def _dot_tn(a, b):
    return lax.dot_general(a, b, (((0,), (0,)), ((), ())), preferred_element_type=F32)


def _sigmoid(x):
    return 1.0 / (1.0 + jnp.exp(-x))


def _norm_proj_kernel(has_aux, x_ref, g_ref, w_ref, *rest):
    x = x_ref[...]
    ms = jnp.mean(x * x, axis=-1, keepdims=True)
    xn = x * lax.rsqrt(ms + EPS) * g_ref[...]
    xb = xn.astype(BF16)
    rest[-2 if has_aux else -1][...] = jnp.dot(xb, w_ref[...], preferred_element_type=F32).astype(BF16)
    if has_aux:
        whi_ref, wlo_ref, _, oaux_ref = rest
        xl = (xn - xb.astype(F32)).astype(BF16)
        d = lambda a, b: jnp.dot(a, b[...], preferred_element_type=F32)
        oaux_ref[...] = d(xb, whi_ref) + (d(xl, whi_ref) + d(xb, wlo_ref))


def _norm_proj(x, gain, w, w_aux=None):
    n, d = x.shape
    m = w.shape[1]
    tm = PROJ_TM
    has_aux = w_aux is not None
    in_specs = [pl.BlockSpec((tm, d), lambda i: (i, 0)),
                pl.BlockSpec((1, d), lambda i: (0, 0)),
                pl.BlockSpec((d, m), lambda i: (0, 0))]
    out_specs = [pl.BlockSpec((tm, m), lambda i: (i, 0))]
    out_shape = [jax.ShapeDtypeStruct((n, m), BF16)]
    args = [x, gain.reshape(1, d), w]
    if has_aux:
        w_hi = w_aux.astype(BF16)
        w_lo = (w_aux - w_hi.astype(F32)).astype(BF16)
        in_specs += [pl.BlockSpec((d, LANES), lambda i: (0, 0))] * 2
        out_specs.append(pl.BlockSpec((tm, LANES), lambda i: (i, 0)))
        out_shape.append(jax.ShapeDtypeStruct((n, LANES), F32))
        args += [w_hi, w_lo]
    res = pl.pallas_call(
        functools.partial(_norm_proj_kernel, has_aux),
        grid=(n // tm,),
        in_specs=in_specs, out_specs=out_specs, out_shape=out_shape,
        compiler_params=pltpu.CompilerParams(
            dimension_semantics=("parallel",), vmem_limit_bytes=VMEM_LIMIT),
    )(*args)
    return res if has_aux else res[0]


def _proj_res_kernel(n_in, *refs):
    h_ref = refs[2 * n_in]
    o_ref = refs[2 * n_in + 1]
    acc = h_ref[...]
    for t in range(n_in):
        acc = acc + jnp.dot(refs[2 * t][...], refs[2 * t + 1][...], preferred_element_type=F32)
    o_ref[...] = acc


def _proj_residual(pairs, h):
    n, d = h.shape
    tm = PROJ_TM
    in_specs, args = [], []
    for a, w in pairs:
        k = a.shape[1]
        in_specs += [pl.BlockSpec((tm, k), lambda i: (i, 0)),
                     pl.BlockSpec((k, d), lambda i: (0, 0))]
        args += [a, w]
    in_specs.append(pl.BlockSpec((tm, d), lambda i: (i, 0)))
    args.append(h)
    return pl.pallas_call(
        functools.partial(_proj_res_kernel, len(pairs)),
        grid=(n // tm,),
        in_specs=in_specs,
        out_specs=pl.BlockSpec((tm, d), lambda i: (i, 0)),
        out_shape=jax.ShapeDtypeStruct((n, d), F32),
        compiler_params=pltpu.CompilerParams(
            dimension_semantics=("parallel",), vmem_limit_bytes=VMEM_LIMIT),
    )(*args)


_HGRN_LEVELS = (64, 32, 16)
_HGRN_DIAG = 8


def _hgrn_constants():
    c = HGRN_CHUNK
    idx = np.arange(c)
    low = (idx[None, :] <= idx[:, None]).astype(np.float64)

    def ref_rows(r):
        return (idx[None, :] <= r[:, None]).astype(np.float64)

    blocks = [low, ref_rows(np.full(c, c - 1)) - low]
    masks = []
    for b in _HGRN_LEVELS:
        start = (idx // b) * b
        upper = (idx - start) >= b // 2
        ref = start + b // 2 - 1
        blocks.append(low - ref_rows(np.where(upper, ref, idx)))
        blocks.append(ref_rows(np.where(upper, idx, ref)) - low)
        same = (idx[:, None] // b) == (idx[None, :] // b)
        masks.append(same & upper[:, None] & ~upper[None, :])
    ref = (idx // _HGRN_DIAG) * _HGRN_DIAG + _HGRN_DIAG // 2 - 1
    blocks.append(low - ref_rows(ref))
    blocks.append(ref_rows(ref) - low)
    same = (idx[:, None] // _HGRN_DIAG) == (idx[None, :] // _HGRN_DIAG)
    masks.append(same & (idx[None, :] <= idx[:, None]))
    dst = np.concatenate(blocks, axis=0)
    return dst.astype(np.float32), np.stack(masks).astype(np.float32)


def _hgrn_kernel(q_ref, f_ref, i_ref, g_ref, lb_ref, gn_ref, dst_ref, mask_ref, bd_ref, grp_ref,
                 o_ref, st_ref):
    c = HGRN_CHUNK
    n_batch, n_pairs = st_ref.shape[:2]
    n_lvl = mask_ref.shape[0]

    @pl.when(pl.program_id(0) == 0)
    def _():
        st_ref[...] = jnp.zeros_like(st_ref)

    lb = lb_ref[...]
    gn = gn_ref[...]
    dst = dst_ref[...]
    bd = bd_ref[...]
    grp = grp_ref[...]
    low = lax.broadcasted_iota(jnp.int32, (c, LANES), 1) < HEAD_DIM

    def stack(x):
        return jnp.concatenate([jnp.where(low, x, jnp.zeros_like(x)), jnp.where(low, jnp.zeros_like(x), x)],
                               axis=0)

    for ch, b in [(ch, b) for ch in range(q_ref.shape[1] // c) for b in range(n_batch)]:
        rows = pl.ds(ch * c, c)
        q = q_ref[b, rows, :].astype(F32)
        qf = q * _sigmoid(q)
        f = lb + (1.0 - lb) * _sigmoid(f_ref[b, rows, :].astype(F32))
        kk = 1.0 - f
        ex = jnp.exp(_dot3_stacked(dst, jnp.log(f)))
        v = i_ref[b, rows, :].astype(BF16)
        g = g_ref[b, rows, :].astype(F32)
        gate = g * _sigmoid(g)
        q_in = (qf * ex[0:c]).astype(BF16)
        k_st = (kk * ex[c:2 * c]).astype(BF16)
        dec = ex[c - 1:c]
        q_l = [(qf * ex[(2 + 2 * l) * c:(3 + 2 * l) * c]).astype(BF16) for l in range(n_lvl)]
        k_l = [(kk * ex[(3 + 2 * l) * c:(4 + 2 * l) * c]).astype(BF16) for l in range(n_lvl)]
        outs = []
        for p in range(n_pairs):
            ps = slice(p * LANES, (p + 1) * LANES)
            scores = mask_ref[0] * _dot_nt(stack(q_l[0][:, ps]), k_l[0][:, ps])
            for l in range(1, n_lvl):
                scores = scores + mask_ref[l] * _dot_nt(stack(q_l[l][:, ps]), k_l[l][:, ps])
            pv = jnp.dot(scores.astype(BF16), v[:, ps], preferred_element_type=F32)
            st = st_ref[b, p]
            o = jnp.where(low, pv[:c], pv[c:]) + _dot_nt(q_in[:, ps], st.astype(BF16))
            st_ref[b, p] = st * dec[:, ps] + bd * _dot_tn(v[:, ps], k_st[:, ps])
            outs.append(o * lax.rsqrt(_dot2_rhs(o * o, grp) + EPS) * gn)
        o_ref[b, rows, :] = (jnp.concatenate(outs, axis=-1) * gate).astype(o_ref.dtype)


def _hgrn2(proj, lb, out_norm, batch, seq):
    n = proj.shape[0]
    width = lb.shape[0]
    n_heads = width // HEAD_DIM
    rb = HGRN_ROWS
    spb = seq // rb
    dst, masks = _hgrn_constants()
    col = lambda j: pl.BlockSpec((batch, rb, width), lambda s, j=j: (0, s, j))
    full = lambda a: pl.BlockSpec(a.shape, lambda s: (0,) * a.ndim)
    lb2 = lb.reshape(1, width)
    gn = jnp.tile(out_norm, LANES // HEAD_DIM).reshape(1, LANES)
    dst = jnp.asarray(np.concatenate([dst, dst, dst], axis=1), BF16)
    masks = jnp.asarray(np.concatenate([masks, masks], axis=1), F32)
    lane = np.arange(LANES)
    bd = jnp.asarray((lane[:, None] // HEAD_DIM) == (lane[None, :] // HEAD_DIM), F32)
    grp = _group_mean_matrix()
    proj3 = proj.reshape(batch, seq, proj.shape[1])
    out = pl.pallas_call(
        _hgrn_kernel,
        grid=(spb,),
        in_specs=[col(0), col(1), col(2), col(3), full(lb2), full(gn), full(dst), full(masks),
                  full(bd), full(grp)],
        out_specs=pl.BlockSpec((batch, rb, width), lambda s: (0, s, 0)),
        out_shape=jax.ShapeDtypeStruct((batch, seq, width), BF16),
        scratch_shapes=[pltpu.VMEM((batch, n_heads // 2, LANES, LANES), F32)],
        compiler_params=pltpu.CompilerParams(
            dimension_semantics=("arbitrary",), vmem_limit_bytes=VMEM_LIMIT),
    )(proj3, proj3, proj3, proj3, lb2, gn, dst, masks, bd, grp)
    return out.reshape(n, width)


def _fox_prep_kernel(q_ref, k_ref, v_ref, gate_ref, bias_ref, gq_ref, gk_ref, tril_ref,
                     sq_ref, sk_ref, cq_ref, ck_ref, cv_ref, grp_ref,
                     qa_ref, ka_ref, va_ref, carry_ref):
    n_heads = qa_ref.shape[1]
    tm = q_ref.shape[0]

    @pl.when(pl.program_id(1) == 0)
    def _():
        carry_ref[...] = jnp.zeros_like(carry_ref)

    z = gate_ref[...] + bias_ref[...]
    ls = -(jnp.maximum(-z, 0.0) + jnp.log(1.0 + jnp.exp(-jnp.abs(z))))
    cum = _dot3(tril_ref[...], ls) + carry_ref[...]
    carry_ref[...] = cum[tm - 1:tm]
    cum = cum * LOG2E

    c3 = jnp.concatenate(_split3(cum), axis=-1)
    ext_q = jnp.dot(c3, sq_ref[...], preferred_element_type=F32) + cq_ref[...]
    ext_k = jnp.dot(c3, sk_ref[...], preferred_element_type=F32) + ck_ref[...]

    lane = lax.broadcasted_iota(jnp.int32, (tm, LANES), 1)
    low_half = lane < HEAD_DIM
    grp = grp_ref[...]
    scale = HEAD_DIM ** -0.5 * LOG2E
    for c in range(n_heads // 2):
        cols = slice(c * LANES, (c + 1) * LANES)
        q = q_ref[:, cols].astype(F32)
        k = k_ref[:, cols].astype(F32)
        v = v_ref[:, cols]
        qn = q * lax.rsqrt(_dot2_rhs(q * q, grp) + EPS) * gq_ref[...] * scale
        kn = k * lax.rsqrt(_dot2_rhs(k * k, grp) + EPS) * gk_ref[...]
        for par in range(2):
            h = 2 * c + par
            data = low_half if par == 0 else jnp.logical_not(low_half)
            ext = slice(h * LANES, (h + 1) * LANES)
            qa_ref[0, h] = jnp.where(data, qn, ext_q[:, ext]).astype(BF16)
            ka_ref[0, h] = jnp.where(data, kn, ext_k[:, ext]).astype(BF16)
            va_ref[0, h] = jnp.where(data, v, cv_ref[par:par + 1, :].astype(BF16))


def _fox_layout_constants(n_heads):
    sq = np.zeros((3 * LANES, n_heads * LANES), np.float32)
    sk = np.zeros((3 * LANES, n_heads * LANES), np.float32)
    cq = np.zeros((1, n_heads * LANES), np.float32)
    ck = np.zeros((1, n_heads * LANES), np.float32)
    cv = np.zeros((2, LANES), np.float32)
    for h in range(n_heads):
        x0 = h * LANES + (HEAD_DIM if h % 2 == 0 else 0)
        for t in range(3):
            sq[t * LANES + h, x0 + t] = 1.0
            sk[t * LANES + h, x0 + 3 + t] = -1.0
        cq[0, x0 + 3:x0 + 6] = 1.0
        ck[0, x0:x0 + 3] = 1.0
    cv[0, HEAD_DIM] = 1.0
    cv[1, 0] = 1.0
    return (jnp.asarray(sq, BF16), jnp.asarray(sk, BF16), jnp.asarray(cq), jnp.asarray(ck), jnp.asarray(cv))


def _group_mean_matrix():
    lane = np.arange(LANES)
    return jnp.asarray(((lane[:, None] // HEAD_DIM) == (lane[None, :] // HEAD_DIM)) / HEAD_DIM, BF16)


def _fox_prep(proj, gates, f_bias, q_norm, k_norm, batch, seq, col0):
    width = 512
    n_heads = width // HEAD_DIM
    tm = PREP_ROWS
    spb = seq // tm
    col = lambda j: pl.BlockSpec((tm, width), lambda b, s, j=j: (b * spb + s, col0 + j))
    full = lambda a: pl.BlockSpec(a.shape, lambda b, s: (0,) * a.ndim)
    bias = jnp.zeros((1, LANES), F32).at[0, :n_heads].set(f_bias)
    gq = jnp.tile(q_norm, LANES // HEAD_DIM).reshape(1, LANES)
    gk = jnp.tile(k_norm, LANES // HEAD_DIM).reshape(1, LANES)
    tril = jnp.asarray(np.tril(np.ones((tm, tm), np.float32)), BF16)
    consts = _fox_layout_constants(n_heads) + (_group_mean_matrix(),)
    out = jax.ShapeDtypeStruct((batch, n_heads, seq, LANES), BF16)
    ospec = pl.BlockSpec((1, n_heads, tm, LANES), lambda b, s: (b, 0, s, 0))
    return pl.pallas_call(
        _fox_prep_kernel,
        grid=(batch, spb),
        in_specs=[col(0), col(1), col(2),
                  pl.BlockSpec((tm, LANES), lambda b, s: (b * spb + s, 0)),
                  full(bias), full(gq), full(gk), full(tril)] + [full(a) for a in consts],
        out_specs=[ospec, ospec, ospec],
        out_shape=[out, out, out],
        scratch_shapes=[pltpu.VMEM((1, LANES), F32)],
        compiler_params=pltpu.CompilerParams(
            dimension_semantics=("parallel", "arbitrary"), vmem_limit_bytes=VMEM_LIMIT),
    )(proj, proj, proj, gates, bias, gq, gk, tril, *consts)


def _tri_tables(nq):
    qi = [q for q in range(nq) for _ in range(q + 1)]
    ki = [k for q in range(nq) for k in range(q + 1)]
    return jnp.asarray(qi, jnp.int32), jnp.asarray(ki, jnp.int32)


def _fox_attn_kernel(qt_ref, kt_ref, q_ref, k_ref, v_ref, o_ref, m_ref, acc_ref):
    p_idx = pl.program_id(2)
    qi = qt_ref[p_idx]
    ki = kt_ref[p_idx]
    hp = q_ref.shape[1]
    t = q_ref.shape[2]

    @pl.when(ki == 0)
    def _():
        m_ref[...] = jnp.full_like(m_ref, -jnp.inf)
        acc_ref[...] = jnp.zeros_like(acc_ref)

    def step(masked):
        for h in range(hp):
            s = _dot_nt(q_ref[0, h], k_ref[0, h])
            if masked:
                row = lax.broadcasted_iota(jnp.int32, (t, t), 0)
                colm = lax.broadcasted_iota(jnp.int32, (t, t), 1)
                s = jnp.where(colm <= row, s, -jnp.inf)
            m_old = m_ref[h]
            m_new = jnp.maximum(m_old, jnp.max(s, axis=-1, keepdims=True))
            p = jnp.exp2(s - m_new)
            acc_ref[h] = (jnp.exp2(m_old - m_new) * acc_ref[h]
                          + jnp.dot(p.astype(BF16), v_ref[0, h], preferred_element_type=F32))
            m_ref[h] = m_new

    @pl.when(ki < qi)
    def _():
        step(False)

    @pl.when(ki == qi)
    def _():
        step(True)
        _fox_finalize(acc_ref, o_ref)


def _fox_finalize(acc_ref, o_ref):
    a0 = acc_ref[0]
    a1 = acc_ref[1]
    lane = lax.broadcasted_iota(jnp.int32, a0.shape, 1)
    o_ref[0] = jnp.where(lane < HEAD_DIM, a0 / a0[:, HEAD_DIM:HEAD_DIM + 1], a1 / a1[:, 0:1]).astype(o_ref.dtype)


def _tile_plan(n_sub, diagonal):
    plan = []
    for qb in range(n_sub):
        if not diagonal:
            plan.append((qb, 0, n_sub, False))
        else:
            if qb > 0:
                plan.append((qb, 0, qb, False))
            plan.append((qb, qb, qb + 1, True))
    return plan


def _fox_fast_kernel(qt_ref, kt_ref, q_ref, k_ref, v_ref, o_ref, acc_ref):
    p_idx = pl.program_id(2)
    qi = qt_ref[p_idx]
    ki = kt_ref[p_idx]
    hp = q_ref.shape[1]
    sb = ATTN_FAST_SUB
    n_sub = q_ref.shape[2] // sb

    @pl.when(ki == 0)
    def _():
        acc_ref[...] = jnp.zeros_like(acc_ref)

    def tile(diagonal):
        for h in range(hp):
            for qb, k0, k1, masked in _tile_plan(n_sub, diagonal):
                rows = pl.ds(qb * sb, sb)
                cols = pl.ds(k0 * sb, (k1 - k0) * sb)
                s = _dot_nt(q_ref[0, h, rows, :], k_ref[0, h, cols, :])
                if masked:
                    row = lax.broadcasted_iota(jnp.int32, (sb, sb), 0)
                    colm = lax.broadcasted_iota(jnp.int32, (sb, sb), 1)
                    s = jnp.where(colm <= row, s, -jnp.inf)
                p = jnp.exp2(s).astype(BF16)
                acc_ref[h, rows, :] += jnp.dot(p, v_ref[0, h, cols, :], preferred_element_type=F32)

    @pl.when(ki < qi)
    def _():
        tile(False)

    @pl.when(ki == qi)
    def _():
        tile(True)
        _fox_finalize(acc_ref, o_ref)


def _fox_attention(qa, ka, va, fast):
    batch, n_heads, seq, _ = qa.shape
    t = ATTN_FAST_TILE if fast else ATTN_TILE
    hp = 2
    nq = seq // t
    qt, kt = _tri_tables(nq)
    qspec = pl.BlockSpec((1, hp, t, LANES), lambda b, g, p, qt, kt: (b, g, qt[p], 0))
    kspec = pl.BlockSpec((1, hp, t, LANES), lambda b, g, p, qt, kt: (b, g, kt[p], 0))
    scratch = [pltpu.VMEM((hp, t, LANES), F32)]
    if not fast:
        scratch = [pltpu.VMEM((hp, t, 1), F32)] + scratch
    return pl.pallas_call(
        _fox_fast_kernel if fast else _fox_attn_kernel,
        grid_spec=pltpu.PrefetchScalarGridSpec(
            num_scalar_prefetch=2,
            grid=(batch, n_heads // hp, int(qt.shape[0])),
            in_specs=[qspec, kspec, kspec],
            out_specs=pl.BlockSpec((1, t, hp * HEAD_DIM), lambda b, g, p, qt, kt: (b, qt[p], g)),
            scratch_shapes=scratch),
        out_shape=jax.ShapeDtypeStruct((batch, seq, n_heads * HEAD_DIM), BF16),
        compiler_params=pltpu.CompilerParams(
            dimension_semantics=("parallel", "parallel", "arbitrary"), vmem_limit_bytes=VMEM_LIMIT),
    )(qt, kt, qa, ka, va)


def _logit_bound(q_gain, k_gain):
    return HEAD_DIM ** 0.5 * jnp.max(jnp.abs(q_gain)) * jnp.max(jnp.abs(k_gain))


def _diff_prep_kernel(q_ref, k_ref, v_ref, pos_ref, invf_ref, gq_ref, gk_ref, grp_ref, sel_ref,
                      qm_ref, k2_ref, va_ref, cs_ref, sn_ref):
    n_heads = k2_ref.shape[1]
    tm = q_ref.shape[0]
    ang = pos_ref[...].astype(F32) * invf_ref[...]
    cs_c = jnp.cos(ang)
    sn_c = jnp.sin(ang)
    per_row = LANES // (HEAD_DIM // 2)
    for j in range(per_row):
        cs_ref[pl.ds(j, tm // per_row, stride=per_row), :] = _dot3_rhs(cs_c, sel_ref[j])
        sn_ref[pl.ds(j, tm // per_row, stride=per_row), :] = _dot3_rhs(sn_c, sel_ref[j])
    lane = lax.broadcasted_iota(jnp.int32, (tm, LANES), 1)
    first = (lane % HEAD_DIM) < (HEAD_DIM // 2)
    cs = cs_ref[...]
    sn = sn_ref[...]
    sn = jnp.where(first, -sn, sn)
    grp = grp_ref[...]
    scale = HEAD_DIM ** -0.5 * LOG2E
    zero = jnp.zeros((tm, LANES), F32)
    onecol = jnp.where(lane == 0, 1.0, 0.0).astype(BF16)

    def norm_rope(x, gain):
        ms = _dot2_rhs(x * x, grp)
        y = x * lax.rsqrt(ms + EPS) * gain
        yr = jnp.where(first, pltpu.roll(y, LANES - HEAD_DIM // 2, 1), pltpu.roll(y, HEAD_DIM // 2, 1))
        return y * cs + yr * sn

    for h in range(n_heads):
        cols = slice(h * LANES, (h + 1) * LANES)
        qr = norm_rope(q_ref[:, cols].astype(F32), gq_ref[...]) * scale
        kr = norm_rope(k_ref[:, cols].astype(F32), gk_ref[...])
        qm_ref[0, h, 0] = jnp.where(lane < HEAD_DIM, qr, zero).astype(BF16)
        qm_ref[0, h, 1] = jnp.where(lane < HEAD_DIM, zero, qr).astype(BF16)
        k2_ref[0, h] = kr.astype(BF16)
        va_ref[0, h] = jnp.concatenate([v_ref[:, cols].astype(BF16), onecol], axis=-1)


def _diff_prep(proj, positions, q_norm, k_norm, batch, seq):
    n = proj.shape[0]
    width = proj.shape[1] // 3
    n_heads = width // LANES
    tm = PREP_ROWS
    spb = seq // tm
    col = lambda j: pl.BlockSpec((tm, width), lambda b, s, j=j: (b * spb + s, j))
    full = lambda a: pl.BlockSpec(a.shape, lambda b, s: (0,) * a.ndim)
    half = HEAD_DIM // 2
    inv_freq = ROPE_THETA ** (-jnp.arange(half, dtype=F32) / half)
    invf = jnp.tile(inv_freq, LANES // half).reshape(1, LANES)
    gq = jnp.tile(q_norm, LANES // HEAD_DIM).reshape(1, LANES)
    gk = jnp.tile(k_norm, LANES // HEAD_DIM).reshape(1, LANES)
    grp = _group_mean_matrix()
    per_row = LANES // half
    pos = jnp.repeat(positions.reshape(n // per_row, per_row).astype(jnp.int32), half, axis=1)
    lane = np.arange(LANES)
    sel = jnp.asarray(np.stack([(lane[:, None] == j * half + lane[None, :] % half) for j in range(per_row)]),
                      BF16)
    return pl.pallas_call(
        _diff_prep_kernel,
        grid=(batch, spb),
        in_specs=[col(0), col(1), col(2),
                  pl.BlockSpec((tm // per_row, LANES), lambda b, s: (b * spb + s, 0)),
                  full(invf), full(gq), full(gk), full(grp), full(sel)],
        out_specs=[pl.BlockSpec((1, n_heads, 2, tm, LANES), lambda b, s: (b, 0, 0, s, 0)),
                   pl.BlockSpec((1, n_heads, tm, LANES), lambda b, s: (b, 0, s, 0)),
                   pl.BlockSpec((1, n_heads, tm, 2 * LANES), lambda b, s: (b, 0, s, 0))],
        out_shape=[jax.ShapeDtypeStruct((batch, n_heads, 2, seq, LANES), BF16),
                   jax.ShapeDtypeStruct((batch, n_heads, seq, LANES), BF16),
                   jax.ShapeDtypeStruct((batch, n_heads, seq, 2 * LANES), BF16)],
        scratch_shapes=[pltpu.VMEM((tm, LANES), F32), pltpu.VMEM((tm, LANES), F32)],
        compiler_params=pltpu.CompilerParams(
            dimension_semantics=("parallel", "parallel"), vmem_limit_bytes=VMEM_LIMIT),
    )(proj, proj, proj, pos, invf, gq, gk, grp, sel)


def _diff_attn_kernel(lambda_init, qt_ref, kt_ref, q_ref, k_ref, v_ref, lam_ref, sub_ref,
                      o_ref, m_ref, acc_ref):
    p_idx = pl.program_id(2)
    qi = qt_ref[p_idx]
    ki = kt_ref[p_idx]
    t = k_ref.shape[2]
    dv = o_ref.shape[2]

    @pl.when(ki == 0)
    def _():
        m_ref[...] = jnp.full_like(m_ref, -jnp.inf)
        acc_ref[...] = jnp.zeros_like(acc_ref)

    def step(masked):
        for m in range(2):
            s = _dot_nt(q_ref[0, 0, m], k_ref[0, 0])
            if masked:
                row = lax.broadcasted_iota(jnp.int32, (t, t), 0) // CHUNK
                colm = lax.broadcasted_iota(jnp.int32, (t, t), 1) // CHUNK
                s = jnp.where(colm <= row, s, -jnp.inf)
            m_old = m_ref[m]
            m_new = jnp.maximum(m_old, jnp.max(s, axis=-1, keepdims=True))
            p = jnp.exp2(s - m_new)
            acc_ref[m] = (jnp.exp2(m_old - m_new) * acc_ref[m]
                          + jnp.dot(p.astype(BF16), v_ref[0, 0], preferred_element_type=F32))
            m_ref[m] = m_new

    @pl.when(ki < qi)
    def _():
        step(False)

    @pl.when(ki == qi)
    def _():
        step(True)
        _diff_finalize(lambda_init, acc_ref, lam_ref, sub_ref, o_ref)


def _diff_finalize(lambda_init, acc_ref, lam_ref, sub_ref, o_ref):
    dv = o_ref.shape[2]
    lp = lam_ref[...]
    lam = (jnp.exp(jnp.sum(lp[0:1] * lp[1:2], axis=-1, keepdims=True))
           - jnp.exp(jnp.sum(lp[2:3] * lp[3:4], axis=-1, keepdims=True)) + lambda_init)
    a0 = acc_ref[0]
    a1 = acc_ref[1]
    o = a0[:, :dv] / a0[:, dv:dv + 1] - lam * (a1[:, :dv] / a1[:, dv:dv + 1])
    ms = jnp.mean(o * o, axis=-1, keepdims=True)
    o_ref[0] = ((o * lax.rsqrt(ms + EPS) * sub_ref[...]) * (1.0 - lambda_init)).astype(o_ref.dtype)


def _diff_fast_kernel(lambda_init, qt_ref, kt_ref, q_ref, k_ref, v_ref, lam_ref, sub_ref, o_ref, acc_ref):
    p_idx = pl.program_id(2)
    qi = qt_ref[p_idx]
    ki = kt_ref[p_idx]
    sb = ATTN_FAST_SUB
    n_sub = k_ref.shape[2] // sb

    @pl.when(ki == 0)
    def _():
        acc_ref[...] = jnp.zeros_like(acc_ref)

    def tile(diagonal):
        for m in range(2):
            for qb, k0, k1, masked in _tile_plan(n_sub, diagonal):
                rows = pl.ds(qb * sb, sb)
                cols = pl.ds(k0 * sb, (k1 - k0) * sb)
                s = _dot_nt(q_ref[0, 0, m, rows, :], k_ref[0, 0, cols, :])
                if masked:
                    row = lax.broadcasted_iota(jnp.int32, (sb, sb), 0) // CHUNK
                    colm = lax.broadcasted_iota(jnp.int32, (sb, sb), 1) // CHUNK
                    s = jnp.where(colm <= row, s, -jnp.inf)
                p = jnp.exp2(s).astype(BF16)
                acc_ref[m, rows, :] += jnp.dot(p, v_ref[0, 0, cols, :], preferred_element_type=F32)

    @pl.when(ki < qi)
    def _():
        tile(False)

    @pl.when(ki == qi)
    def _():
        tile(True)
        _diff_finalize(lambda_init, acc_ref, lam_ref, sub_ref, o_ref)


def _diff_attention(qm, k2, va, lam_params, subln, lambda_init, fast):
    batch, n_heads, seq, _ = k2.shape
    dv = va.shape[3] // 2
    t = ATTN_FAST_TILE if fast else ATTN_TILE
    nq = seq // t
    qt, kt = _tri_tables(nq)
    lamp = jnp.zeros((8, LANES), F32).at[:4, :HEAD_DIM].set(lam_params)
    sub = subln.reshape(1, dv)
    scratch = [pltpu.VMEM((2, t, 2 * dv), F32)]
    if not fast:
        scratch = [pltpu.VMEM((2, t, 1), F32)] + scratch
    return pl.pallas_call(
        functools.partial(_diff_fast_kernel if fast else _diff_attn_kernel, lambda_init),
        grid_spec=pltpu.PrefetchScalarGridSpec(
            num_scalar_prefetch=2,
            grid=(batch, n_heads, int(qt.shape[0])),
            in_specs=[pl.BlockSpec((1, 1, 2, t, LANES), lambda b, h, p, qt, kt: (b, h, 0, qt[p], 0)),
                      pl.BlockSpec((1, 1, t, LANES), lambda b, h, p, qt, kt: (b, h, kt[p], 0)),
                      pl.BlockSpec((1, 1, t, 2 * dv), lambda b, h, p, qt, kt: (b, h, kt[p], 0)),
                      pl.BlockSpec((8, LANES), lambda b, h, p, qt, kt: (0, 0)),
                      pl.BlockSpec((1, dv), lambda b, h, p, qt, kt: (0, 0))],
            out_specs=pl.BlockSpec((1, t, dv), lambda b, h, p, qt, kt: (b, qt[p], h)),
            scratch_shapes=scratch),
        out_shape=jax.ShapeDtypeStruct((batch, seq, n_heads * dv), BF16),
        compiler_params=pltpu.CompilerParams(
            dimension_semantics=("parallel", "parallel", "arbitrary"), vmem_limit_bytes=VMEM_LIMIT),
    )(qt, kt, qm, k2, va, lamp, sub)


def _store_token_tiles(ref, value):
    t, width = value.shape
    s = width // LANES
    for j in range(s):
        ref[pl.ds(j, t, stride=s), :] = value[:, j * LANES:(j + 1) * LANES]


def _load_token_tiles(ref, first_row, t, s):
    return jnp.concatenate([ref[pl.ds(first_row + j, t, stride=s), :] for j in range(s)], axis=-1)


def _router_kernel(h_ref, g_ref, whi_ref, wlo_ref, xn_ref, route_ref):
    x = h_ref[...]
    tm = x.shape[0]
    xn = x * lax.rsqrt(jnp.mean(x * x, axis=-1, keepdims=True) + EPS) * g_ref[...]
    _store_token_tiles(xn_ref, xn)
    xh = xn.astype(BF16)
    xl = (xn - xh.astype(F32)).astype(BF16)
    d = lambda a, b: jnp.dot(a, b[...], preferred_element_type=F32)
    logits = d(xh, whi_ref) + (d(xl, whi_ref) + d(xh, wlo_ref))
    lane = lax.broadcasted_iota(jnp.int32, (tm, LANES), 1)
    neg = jnp.full((tm, LANES), -jnp.inf, F32)
    big = jnp.full((tm, LANES), LANES, jnp.int32)

    def top1(vals):
        m = jnp.max(vals, axis=-1, keepdims=True)
        idx = jnp.min(jnp.where(vals == m, lane, big), axis=-1, keepdims=True)
        return m, idx

    grp_logits = jnp.where(lane < N_GROUPS, logits, neg)
    mg, gidx = top1(grp_logits)
    p_g = 1.0 / jnp.sum(jnp.exp(grp_logits - mg), axis=-1, keepdims=True)
    e_lane = lane - N_GROUPS
    in_grp = (e_lane >= gidx * EXPERTS_PER_GROUP) & (e_lane < (gidx + 1) * EXPERTS_PER_GROUP)
    sel = jnp.where(in_grp, logits, neg)
    m1, i1 = top1(sel)
    m2, i2 = top1(jnp.where(lane == i1, neg, sel))
    r = jnp.exp(m2 - m1)
    w1 = p_g / (1.0 + r)
    w2 = p_g * r / (1.0 + r)
    zero = jnp.zeros((tm, LANES), F32)
    route_ref[...] = jnp.where(lane == 0, (i1 - N_GROUPS).astype(F32),
                     jnp.where(lane == 1, (i2 - N_GROUPS).astype(F32),
                     jnp.where(lane == 2, w1, jnp.where(lane == 3, w2, zero))))


def _router(h, gain, w_group, w_expert):
    n, d = h.shape
    tm = ROUTER_ROWS
    wr = jnp.zeros((d, LANES), F32)
    wr = wr.at[:, :N_GROUPS].set(w_group)
    wr = wr.at[:, N_GROUPS:N_GROUPS + N_EXPERTS].set(
        jnp.transpose(w_expert, (1, 0, 2)).reshape(d, N_EXPERTS))
    w_hi = wr.astype(BF16)
    w_lo = (wr - w_hi.astype(F32)).astype(BF16)
    return pl.pallas_call(
        _router_kernel,
        grid=(n // tm,),
        in_specs=[pl.BlockSpec((tm, d), lambda i: (i, 0)),
                  pl.BlockSpec((1, d), lambda i: (0, 0)),
                  pl.BlockSpec((d, LANES), lambda i: (0, 0)),
                  pl.BlockSpec((d, LANES), lambda i: (0, 0))],
        out_specs=[pl.BlockSpec((tm * d // LANES, LANES), lambda i: (i, 0)),
                   pl.BlockSpec((tm, LANES), lambda i: (i, 0))],
        out_shape=[jax.ShapeDtypeStruct((n * d // LANES, LANES), F32),
                   jax.ShapeDtypeStruct((n, LANES), F32)],
        compiler_params=pltpu.CompilerParams(
            dimension_semantics=("parallel",), vmem_limit_bytes=VMEM_LIMIT),
    )(h, gain.reshape(1, d), w_hi, w_lo)


def _dispatch_tables(expert_ids, tm):
    n = expert_ids.shape[0]
    pairs = expert_ids.reshape(-1)
    n_pairs = pairs.shape[0]
    n_tiles = n_pairs // tm + N_EXPERTS
    onehot = (pairs[:, None] == jnp.arange(N_EXPERTS, dtype=jnp.int32)[None, :]).astype(jnp.int32)
    csum = jnp.cumsum(onehot, axis=0)
    rank = jnp.sum((csum - onehot) * onehot, axis=1)
    counts = csum[-1]
    padded = ((counts + tm - 1) // tm) * tm
    ends = jnp.cumsum(padded)
    starts = ends - padded
    pos = jnp.sum(onehot * starts[None, :], axis=1) + rank
    tile_start = jnp.arange(n_tiles, dtype=jnp.int32) * tm
    tile_expert = jnp.minimum(jnp.sum(tile_start[:, None] >= ends[None, :], axis=1),
                              N_EXPERTS - 1).astype(jnp.int32)
    n_valid = (ends[-1] // tm).astype(jnp.int32).reshape(1)
    pad_start = (starts + counts).astype(jnp.int32)
    pad_count = (padded - counts).astype(jnp.int32)
    return pos.reshape(n, 2).astype(jnp.int32), tile_expert, n_valid, n_tiles, pad_start, pad_count


def _token_copy(src_hbm, tok, dst_ref, r, sem):
    src = src_hbm.at[pl.ds(pl.multiple_of(tok * TILE_ROWS, TILE_ROWS), TILE_ROWS)]
    first = r * TILE_ROWS if isinstance(r, int) else pl.multiple_of(r * TILE_ROWS, TILE_ROWS)
    return pltpu.make_async_copy(src, dst_ref.at[pl.ds(first, TILE_ROWS)], sem)


def _gather_tokens(src_hbm, idx_ref, dst_ref, sem, n_tokens):
    def body(r, carry):
        _token_copy(src_hbm, idx_ref[0, 0, r], dst_ref, r, sem).start()
        return carry
    lax.fori_loop(0, n_tokens, body, 0, unroll=8)


def _wait_tokens(src_hbm, dst_ref, sem):
    pltpu.make_async_copy(src_hbm.at[pl.ds(0, dst_ref.shape[0])], dst_ref, sem).wait()


def _dispatch_kernel(ps_ref, pc_ref, nv_ref, pos_ref, x_hbm, zero_hbm, zero_blk_hbm, xs_hbm, sem, pad_sem):
    i = pl.program_id(0)
    tokens = pos_ref.shape[2] // 2

    def slot_tile(slot):
        return xs_hbm.at[pl.ds(pl.multiple_of(slot * TILE_ROWS, TILE_ROWS), TILE_ROWS)]

    for r in range(2 * tokens):
        tok = i * tokens + r % tokens
        src = x_hbm.at[pl.ds(pl.multiple_of(tok * TILE_ROWS, TILE_ROWS), TILE_ROWS)]
        pltpu.make_async_copy(src, slot_tile(pos_ref[0, 0, r]), sem).start(priority=r % 2)
    rows = 2 * tokens * TILE_ROWS
    pltpu.make_async_copy(x_hbm.at[pl.ds(0, rows)], xs_hbm.at[pl.ds(0, rows)], sem).wait()

    @pl.when(i == pl.num_programs(0) - 1)
    def _():
        for e in range(ps_ref.shape[0]):
            first = ps_ref[e]
            count = pc_ref[e]

            def fill(j, carry):
                pltpu.make_async_copy(zero_hbm, slot_tile(first + j), pad_sem).start()
                return carry

            def drain(j, carry):
                pltpu.make_async_copy(zero_hbm, slot_tile(first), pad_sem).wait()
                return carry

            lax.fori_loop(0, count, fill, 0)
            lax.fori_loop(0, count, drain, 0)

        block_rows = zero_blk_hbm.shape[0]
        n_blocks = xs_hbm.shape[0] // block_rows

        def block(t):
            return xs_hbm.at[pl.ds(pl.multiple_of(t * block_rows, block_rows), block_rows)]

        def fill_block(t, carry):
            pltpu.make_async_copy(zero_blk_hbm, block(t), pad_sem).start()
            return carry

        def drain_block(t, carry):
            pltpu.make_async_copy(zero_blk_hbm, block(t), pad_sem).wait()
            return carry

        lax.fori_loop(nv_ref[0], n_blocks, fill_block, 0)
        lax.fori_loop(nv_ref[0], n_blocks, drain_block, 0)


def _moe_dispatch(xn, pos, pad_start, pad_count, n_valid, n_tiles, tm):
    n = pos.shape[0]
    tb = COMBINE_ROWS
    steps = n // tb
    pos_tab = jnp.transpose(pos.reshape(steps, tb, 2), (0, 2, 1)).reshape(steps, 1, 2 * tb)
    zero = jnp.zeros((TILE_ROWS, LANES), F32)
    zero_blk = jnp.zeros((tm * TILE_ROWS, LANES), F32)
    any_spec = pl.BlockSpec(memory_space=pl.ANY)
    return pl.pallas_call(
        _dispatch_kernel,
        grid_spec=pltpu.PrefetchScalarGridSpec(
            num_scalar_prefetch=3,
            grid=(steps,),
            in_specs=[pl.BlockSpec((1, 1, 2 * tb), lambda i, ps, pc, nv: (i, 0, 0), memory_space=pltpu.SMEM),
                      any_spec, any_spec, any_spec],
            out_specs=any_spec,
            scratch_shapes=[pltpu.SemaphoreType.DMA, pltpu.SemaphoreType.DMA]),
        out_shape=jax.ShapeDtypeStruct((n_tiles * tm * TILE_ROWS, LANES), F32),
        compiler_params=pltpu.CompilerParams(
            dimension_semantics=("arbitrary",), vmem_limit_bytes=VMEM_LIMIT),
    )(pad_start, pad_count, n_valid, pos_tab, xn, zero, zero_blk)


def _moe_kernel(te_ref, nv_ref, x_ref, wg_ref, wu_ref, wd_ref, o_ref, wg_b, wu_b, wd_b):
    i = pl.program_id(0)
    tm = x_ref.shape[0] // TILE_ROWS
    n_valid = nv_ref[0]
    new_expert = jnp.logical_or(i == 0, te_ref[i] != te_ref[jnp.maximum(i - 1, 0)])

    @pl.when(jnp.logical_and(i < n_valid, new_expert))
    def _():
        wg_b[...] = wg_ref[0].astype(BF16)
        wu_b[...] = wu_ref[0].astype(BF16)
        wd_b[...] = wd_ref[0].astype(BF16)

    @pl.when(i < n_valid)
    def _():
        x = _load_token_tiles(x_ref, 0, tm, TILE_ROWS).astype(BF16)
        g = jnp.dot(x, wg_b[...], preferred_element_type=F32)
        u = jnp.dot(x, wu_b[...], preferred_element_type=F32)
        hid = (g * _sigmoid(g) * u).astype(BF16)
        _store_token_tiles(o_ref, jnp.dot(hid, wd_b[...], preferred_element_type=F32))

    @pl.when(i >= n_valid)
    def _():
        o_ref[...] = jnp.zeros_like(o_ref)


def _moe_experts(xs, tile_expert, n_valid, tm, w_gate, w_up, w_down):
    n_exp, d, f = w_gate.shape
    assert d == TILE_ROWS * LANES
    n_tiles = xs.shape[0] // (tm * TILE_ROWS)
    x_index = lambda i, te, nv: (jnp.minimum(i, nv[0] - 1), 0)
    return pl.pallas_call(
        _moe_kernel,
        grid_spec=pltpu.PrefetchScalarGridSpec(
            num_scalar_prefetch=2,
            grid=(n_tiles,),
            in_specs=[
                pl.BlockSpec((tm * TILE_ROWS, LANES), x_index),
                pl.BlockSpec((1, d, f), lambda i, te, nv: (te[i], 0, 0)),
                pl.BlockSpec((1, d, f), lambda i, te, nv: (te[i], 0, 0)),
                pl.BlockSpec((1, f, d), lambda i, te, nv: (te[i], 0, 0))],
            out_specs=pl.BlockSpec((tm * TILE_ROWS, LANES), lambda i, te, nv: (i, 0)),
            scratch_shapes=[pltpu.VMEM((d, f), BF16), pltpu.VMEM((d, f), BF16), pltpu.VMEM((f, d), BF16)]),
        out_shape=jax.ShapeDtypeStruct((n_tiles * tm * TILE_ROWS, LANES), F32),
        compiler_params=pltpu.CompilerParams(
            dimension_semantics=("arbitrary",), vmem_limit_bytes=VMEM_LIMIT),
    )(tile_expert, n_valid, xs, w_gate, w_up, w_down)


def _combine_kernel(pos_ref, pos_next_ref, h_ref, route_ref, y_hbm, o_ref, ybuf, sems):
    i = pl.program_id(0)
    n_steps = pl.num_programs(0)
    tokens = ybuf.shape[1] // TILE_ROWS
    slot = i % 2

    @pl.when(i == 0)
    def _():
        _gather_tokens(y_hbm, pos_ref, ybuf.at[0], sems.at[0], tokens)

    @pl.when(i + 1 < n_steps)
    def _():
        for r in range(tokens):
            _token_copy(y_hbm, pos_next_ref[0, 0, r], ybuf.at[1 - slot], r,
                        sems.at[1 - slot]).start(priority=r % 2)

    _wait_tokens(y_hbm, ybuf.at[slot], sems.at[slot])
    tc = tokens // 2
    w = route_ref[...]
    first = _load_token_tiles(ybuf.at[slot], 0, tc, TILE_ROWS)
    second = _load_token_tiles(ybuf.at[slot], tc * TILE_ROWS, tc, TILE_ROWS)
    o_ref[...] = h_ref[...] + w[:, 2:3] * first + w[:, 3:4] * second


def _moe_combine(h, route, pos, y_sorted):
    n, d = h.shape
    tc = COMBINE_ROWS
    steps = n // tc
    pos_tab = jnp.transpose(pos.reshape(steps, tc, 2), (0, 2, 1)).reshape(steps, 1, 2 * tc)
    return pl.pallas_call(
        _combine_kernel,
        grid=(steps,),
        in_specs=[pl.BlockSpec((1, 1, 2 * tc), lambda i: (i, 0, 0), memory_space=pltpu.SMEM),
                  pl.BlockSpec((1, 1, 2 * tc), lambda i: (jnp.minimum(i + 1, steps - 1), 0, 0),
                               memory_space=pltpu.SMEM),
                  pl.BlockSpec((tc, d), lambda i: (i, 0)),
                  pl.BlockSpec((tc, LANES), lambda i: (i, 0)),
                  pl.BlockSpec(memory_space=pl.ANY)],
        out_specs=pl.BlockSpec((tc, d), lambda i: (i, 0)),
        out_shape=jax.ShapeDtypeStruct((n, d), F32),
        scratch_shapes=[pltpu.VMEM((2, 2 * tc * TILE_ROWS, LANES), F32), pltpu.SemaphoreType.DMA((2,))],
        compiler_params=pltpu.CompilerParams(
            dimension_semantics=("arbitrary",), vmem_limit_bytes=VMEM_LIMIT),
    )(pos_tab, pos_tab, h, route, y_sorted)


def _moe_layer(h, layer, gain, w_group, w_expert, w_gate, w_up, w_down):
    d = h.shape[1]
    f = w_gate.shape[-1]
    xn, route = _router(h, gain, w_group, w_expert)
    expert_ids = route[:, :2].astype(jnp.int32)
    pos, tile_expert, n_valid, n_tiles, pad_start, pad_count = _dispatch_tables(expert_ids, MOE_TM)
    xs = _moe_dispatch(xn, pos, pad_start, pad_count, n_valid, n_tiles, MOE_TM)
    y_sorted = _moe_experts(xs, tile_expert + layer * N_EXPERTS, n_valid, MOE_TM,
                            w_gate.reshape(-1, d, f), w_up.reshape(-1, d, f), w_down.reshape(-1, f, d))
    return _moe_combine(h, route, pos, y_sorted)


def _even_layer(h, batch, seq, gain, w_in, w_out, lb, f_bias, out_norm, q_norm, k_norm):
    d = h.shape[1]
    n_main = w_in.shape[1] - f_bias.shape[0]
    w_main = w_in[:, :n_main].astype(BF16)
    w_gate = jnp.zeros((d, LANES), F32).at[:, :f_bias.shape[0]].set(w_in[:, n_main:])
    proj, gates = _norm_proj(h, gain, w_main, w_gate)
    o_a = _hgrn2(proj, lb, out_norm, batch, seq)
    qa, ka, va = _fox_prep(proj, gates, f_bias, q_norm, k_norm, batch, seq, col0=4)
    o_b = lax.cond(_logit_bound(q_norm, k_norm) <= LOGIT_BOUND_MAX,
                   functools.partial(_fox_attention, fast=True),
                   functools.partial(_fox_attention, fast=False), qa, ka, va).reshape(batch * seq, -1)
    wo = w_out.astype(BF16)
    ka_dim = o_a.shape[1]
    return _proj_residual([(o_a, wo[:ka_dim]), (o_b, wo[ka_dim:])], h)


def _odd_layer(h, positions, batch, seq, gain, w_in, w_out, q_norm, k_norm, lam_params, subln, lambda_init):
    proj = _norm_proj(h, gain, w_in.astype(BF16))
    qm, k2, va = _diff_prep(proj, positions, q_norm, k_norm, batch, seq)
    attn = lambda fast: functools.partial(_diff_attention, lam_params=lam_params, subln=subln,
                                          lambda_init=lambda_init, fast=fast)
    o = lax.cond(_logit_bound(q_norm, k_norm) <= LOGIT_BOUND_MAX,
                 attn(True), attn(False), qm, k2, va).reshape(batch * seq, -1)
    return _proj_residual([(o, w_out.astype(BF16))], h)


def kernel(x, positions, hgrn_lb_logits, norm_mix, norm_ffn, even_w_in, even_w_out, fox_f_bias,
           hgrn_out_norm, fox_q_norm, fox_k_norm, odd_w_in, odd_w_out, diff_q_norm, diff_k_norm,
           diff_lambda_q1, diff_lambda_k1, diff_lambda_q2, diff_lambda_k2, diff_subln,
           moe_router_group, moe_router_expert, moe_w_gate, moe_w_up, moe_w_down):
    batch, seq, d = x.shape
    depth = norm_mix.shape[0]
    lower_bounds = jnp.cumsum(jax.nn.softmax(hgrn_lb_logits.astype(F32), axis=0), axis=0)
    h = x.reshape(batch * seq, d)
    for layer in range(depth):
        j = layer // 2
        if layer % 2 == 0:
            h = _even_layer(h, batch, seq, norm_mix[layer], even_w_in[j], even_w_out[j], lower_bounds[j],
                            fox_f_bias[j], hgrn_out_norm[j], fox_q_norm[j], fox_k_norm[j])
        else:
            lambda_init = 0.8 - 0.6 * math.exp(-0.3 * layer)
            lam_params = jnp.stack([diff_lambda_q1[j], diff_lambda_k1[j],
                                    diff_lambda_q2[j], diff_lambda_k2[j]]).astype(F32)
            h = _odd_layer(h, positions, batch, seq, norm_mix[layer], odd_w_in[j], odd_w_out[j],
                           diff_q_norm[j], diff_k_norm[j], lam_params, diff_subln[j], lambda_init)
        h = _moe_layer(h, layer, norm_ffn[layer], moe_router_group[layer], moe_router_expert[layer],
                       moe_w_gate, moe_w_up, moe_w_down)
    return h.reshape(batch, seq, d)
```

```python
import functools
import math

import numpy as np
import jax
import jax.numpy as jnp
from jax import lax
from jax.experimental import pallas as pl
from jax.experimental.pallas import tpu as pltpu

F32 = jnp.float32
BF16 = jnp.bfloat16

EPS = 1e-6
ROPE_THETA = 10000.0
CHUNK = 64
HEAD_DIM = 64
N_GROUPS = 4
EXPERTS_PER_GROUP = 8
N_EXPERTS = N_GROUPS * EXPERTS_PER_GROUP
LANES = 128
TILE_ROWS = 8

HGRN_CHUNK = 64
HGRN_ROWS = 256
ATTN_TILE = 512
ATTN_FAST_TILE = 2048
ATTN_FAST_SUB = 512
LOGIT_BOUND_MAX = 60.0
LOG2E = math.log2(math.e)
PREP_ROWS = 256
PROJ_TM = 256
ROUTER_ROWS = 256
MOE_TM = 256
COMBINE_ROWS = 256
VMEM_LIMIT = 56 * 1024 * 1024


def _split3(x):
    hi = x.astype(BF16)
    r1 = x - hi.astype(F32)
    mid = r1.astype(BF16)
    lo = (r1 - mid.astype(F32)).astype(BF16)
    return hi, mid, lo


def _dot3(const_bf16, x):
    hi, mid, lo = _split3(x)
    d = lambda b: jnp.dot(const_bf16, b, preferred_element_type=F32)
    return d(hi) + d(mid) + d(lo)


def _dot3_stacked(const3_bf16, x):
    return jnp.dot(const3_bf16, jnp.concatenate(_split3(x), axis=0), preferred_element_type=F32)


def _dot3_rhs(x, const_bf16):
    hi, mid, lo = _split3(x)
    d = lambda a: jnp.dot(a, const_bf16, preferred_element_type=F32)
    return d(hi) + d(mid) + d(lo)


def _dot2_rhs(x, const_bf16):
    hi = x.astype(BF16)
    lo = (x - hi.astype(F32)).astype(BF16)
    d = lambda a: jnp.dot(a, const_bf16, preferred_element_type=F32)
    return d(hi) + d(lo)


def _dot_nt(a, b):
    return lax.dot_general(a, b, (((1,), (1,)), ((), ())), preferred_element_type=F32)


def _dot_tn(a, b):
    return lax.dot_general(a, b, (((0,), (0,)), ((), ())), preferred_element_type=F32)


def _sigmoid(x):
    return 1.0 / (1.0 + jnp.exp(-x))


def _norm_proj_kernel(has_aux, x_ref, g_ref, w_ref, *rest):
    x = x_ref[...]
    ms = jnp.mean(x * x, axis=-1, keepdims=True)
    xn = x * lax.rsqrt(ms + EPS) * g_ref[...]
    xb = xn.astype(BF16)
    rest[-2 if has_aux else -1][...] = jnp.dot(xb, w_ref[...], preferred_element_type=F32).astype(BF16)
    if has_aux:
        whi_ref, wlo_ref, _, oaux_ref = rest
        xl = (xn - xb.astype(F32)).astype(BF16)
        d = lambda a, b: jnp.dot(a, b[...], preferred_element_type=F32)
        oaux_ref[...] = d(xb, whi_ref) + (d(xl, whi_ref) + d(xb, wlo_ref))


def _norm_proj(x, gain, w, w_aux=None):
    n, d = x.shape
    m = w.shape[1]
    tm = PROJ_TM
    has_aux = w_aux is not None
    in_specs = [pl.BlockSpec((tm, d), lambda i: (i, 0)),
                pl.BlockSpec((1, d), lambda i: (0, 0)),
                pl.BlockSpec((d, m), lambda i: (0, 0))]
    out_specs = [pl.BlockSpec((tm, m), lambda i: (i, 0))]
    out_shape = [jax.ShapeDtypeStruct((n, m), BF16)]
    args = [x, gain.reshape(1, d), w]
    if has_aux:
        w_hi = w_aux.astype(BF16)
        w_lo = (w_aux - w_hi.astype(F32)).astype(BF16)
        in_specs += [pl.BlockSpec((d, LANES), lambda i: (0, 0))] * 2
        out_specs.append(pl.BlockSpec((tm, LANES), lambda i: (i, 0)))
        out_shape.append(jax.ShapeDtypeStruct((n, LANES), F32))
        args += [w_hi, w_lo]
    res = pl.pallas_call(
        functools.partial(_norm_proj_kernel, has_aux),
        grid=(n // tm,),
        in_specs=in_specs, out_specs=out_specs, out_shape=out_shape,
        compiler_params=pltpu.CompilerParams(
            dimension_semantics=("parallel",), vmem_limit_bytes=VMEM_LIMIT),
    )(*args)
    return res if has_aux else res[0]


def _proj_res_kernel(n_in, *refs):
    h_ref = refs[2 * n_in]
    o_ref = refs[2 * n_in + 1]
    acc = h_ref[...]
    for t in range(n_in):
        acc = acc + jnp.dot(refs[2 * t][...], refs[2 * t + 1][...], preferred_element_type=F32)
    o_ref[...] = acc


def _proj_residual(pairs, h):
    n, d = h.shape
    tm = PROJ_TM
    in_specs, args = [], []
    for a, w in pairs:
        k = a.shape[1]
        in_specs += [pl.BlockSpec((tm, k), lambda i: (i, 0)),
                     pl.BlockSpec((k, d), lambda i: (0, 0))]
        args += [a, w]
    in_specs.append(pl.BlockSpec((tm, d), lambda i: (i, 0)))
    args.append(h)
    return pl.pallas_call(
        functools.partial(_proj_res_kernel, len(pairs)),
        grid=(n // tm,),
        in_specs=in_specs,
        out_specs=pl.BlockSpec((tm, d), lambda i: (i, 0)),
        out_shape=jax.ShapeDtypeStruct((n, d), F32),
        compiler_params=pltpu.CompilerParams(
            dimension_semantics=("parallel",), vmem_limit_bytes=VMEM_LIMIT),
    )(*args)


_HGRN_LEVELS = (64, 32, 16)
_HGRN_DIAG = 8


def _hgrn_constants():
    c = HGRN_CHUNK
    idx = np.arange(c)
    low = (idx[None, :] <= idx[:, None]).astype(np.float64)

    def ref_rows(r):
        return (idx[None, :] <= r[:, None]).astype(np.float64)

    blocks = [low, ref_rows(np.full(c, c - 1)) - low]
    masks = []
    for b in _HGRN_LEVELS:
        start = (idx // b) * b
        upper = (idx - start) >= b // 2
        ref = start + b // 2 - 1
        blocks.append(low - ref_rows(np.where(upper, ref, idx)))
        blocks.append(ref_rows(np.where(upper, idx, ref)) - low)
        same = (idx[:, None] // b) == (idx[None, :] // b)
        masks.append(same & upper[:, None] & ~upper[None, :])
    ref = (idx // _HGRN_DIAG) * _HGRN_DIAG + _HGRN_DIAG // 2 - 1
    blocks.append(low - ref_rows(ref))
    blocks.append(ref_rows(ref) - low)
    same = (idx[:, None] // _HGRN_DIAG) == (idx[None, :] // _HGRN_DIAG)
    masks.append(same & (idx[None, :] <= idx[:, None]))
    dst = np.concatenate(blocks, axis=0)
    return dst.astype(np.float32), np.stack(masks).astype(np.float32)


def _hgrn_kernel(q_ref, f_ref, i_ref, g_ref, lb_ref, gn_ref, dst_ref, mask_ref, bd_ref, grp_ref,
                 o_ref, st_ref):
    c = HGRN_CHUNK
    n_batch, n_pairs = st_ref.shape[:2]
    n_lvl = mask_ref.shape[0]

    @pl.when(pl.program_id(0) == 0)
    def _():
        st_ref[...] = jnp.zeros_like(st_ref)

    lb = lb_ref[...]
    gn = gn_ref[...]
    dst = dst_ref[...]
    bd = bd_ref[...]
    grp = grp_ref[...]
    low = lax.broadcasted_iota(jnp.int32, (c, LANES), 1) < HEAD_DIM

    def stack(x):
        return jnp.concatenate([jnp.where(low, x, jnp.zeros_like(x)), jnp.where(low, jnp.zeros_like(x), x)],
                               axis=0)

    for ch, b in [(ch, b) for ch in range(q_ref.shape[1] // c) for b in range(n_batch)]:
        rows = pl.ds(ch * c, c)
        q = q_ref[b, rows, :].astype(F32)
        qf = q * _sigmoid(q)
        f = lb + (1.0 - lb) * _sigmoid(f_ref[b, rows, :].astype(F32))
        kk = 1.0 - f
        ex = jnp.exp(_dot3_stacked(dst, jnp.log(f)))
        v = i_ref[b, rows, :].astype(BF16)
        g = g_ref[b, rows, :].astype(F32)
        gate = g * _sigmoid(g)
        q_in = (qf * ex[0:c]).astype(BF16)
        k_st = (kk * ex[c:2 * c]).astype(BF16)
        dec = ex[c - 1:c]
        q_l = [(qf * ex[(2 + 2 * l) * c:(3 + 2 * l) * c]).astype(BF16) for l in range(n_lvl)]
        k_l = [(kk * ex[(3 + 2 * l) * c:(4 + 2 * l) * c]).astype(BF16) for l in range(n_lvl)]
        outs = []
        for p in range(n_pairs):
            ps = slice(p * LANES, (p + 1) * LANES)
            scores = mask_ref[0] * _dot_nt(stack(q_l[0][:, ps]), k_l[0][:, ps])
            for l in range(1, n_lvl):
                scores = scores + mask_ref[l] * _dot_nt(stack(q_l[l][:, ps]), k_l[l][:, ps])
            pv = jnp.dot(scores.astype(BF16), v[:, ps], preferred_element_type=F32)
            st = st_ref[b, p]
            o = jnp.where(low, pv[:c], pv[c:]) + _dot_nt(q_in[:, ps], st.astype(BF16))
            st_ref[b, p] = st * dec[:, ps] + bd * _dot_tn(v[:, ps], k_st[:, ps])
            outs.append(o * lax.rsqrt(_dot2_rhs(o * o, grp) + EPS) * gn)
        o_ref[b, rows, :] = (jnp.concatenate(outs, axis=-1) * gate).astype(o_ref.dtype)


def _hgrn2(proj, lb, out_norm, batch, seq):
    n = proj.shape[0]
    width = lb.shape[0]
    n_heads = width // HEAD_DIM
    rb = HGRN_ROWS
    spb = seq // rb
    dst, masks = _hgrn_constants()
    col = lambda j: pl.BlockSpec((batch, rb, width), lambda s, j=j: (0, s, j))
    full = lambda a: pl.BlockSpec(a.shape, lambda s: (0,) * a.ndim)
    lb2 = lb.reshape(1, width)
    gn = jnp.tile(out_norm, LANES // HEAD_DIM).reshape(1, LANES)
    dst = jnp.asarray(np.concatenate([dst, dst, dst], axis=1), BF16)
    masks = jnp.asarray(np.concatenate([masks, masks], axis=1), F32)
    lane = np.arange(LANES)
    bd = jnp.asarray((lane[:, None] // HEAD_DIM) == (lane[None, :] // HEAD_DIM), F32)
    grp = _group_mean_matrix()
    proj3 = proj.reshape(batch, seq, proj.shape[1])
    out = pl.pallas_call(
        _hgrn_kernel,
        grid=(spb,),
        in_specs=[col(0), col(1), col(2), col(3), full(lb2), full(gn), full(dst), full(masks),
                  full(bd), full(grp)],
        out_specs=pl.BlockSpec((batch, rb, width), lambda s: (0, s, 0)),
        out_shape=jax.ShapeDtypeStruct((batch, seq, width), BF16),
        scratch_shapes=[pltpu.VMEM((batch, n_heads // 2, LANES, LANES), F32)],
        compiler_params=pltpu.CompilerParams(
            dimension_semantics=("arbitrary",), vmem_limit_bytes=VMEM_LIMIT),
    )(proj3, proj3, proj3, proj3, lb2, gn, dst, masks, bd, grp)
    return out.reshape(n, width)


def _fox_prep_kernel(q_ref, k_ref, v_ref, gate_ref, bias_ref, gq_ref, gk_ref, tril_ref,
                     sq_ref, sk_ref, cq_ref, ck_ref, cv_ref, grp_ref,
                     qa_ref, ka_ref, va_ref, carry_ref):
    n_heads = qa_ref.shape[1]
    tm = q_ref.shape[0]

    @pl.when(pl.program_id(1) == 0)
    def _():
        carry_ref[...] = jnp.zeros_like(carry_ref)

    z = gate_ref[...] + bias_ref[...]
    ls = -(jnp.maximum(-z, 0.0) + jnp.log(1.0 + jnp.exp(-jnp.abs(z))))
    cum = _dot3(tril_ref[...], ls) + carry_ref[...]
    carry_ref[...] = cum[tm - 1:tm]
    cum = cum * LOG2E

    c3 = jnp.concatenate(_split3(cum), axis=-1)
    ext_q = jnp.dot(c3, sq_ref[...], preferred_element_type=F32) + cq_ref[...]
    ext_k = jnp.dot(c3, sk_ref[...], preferred_element_type=F32) + ck_ref[...]

    lane = lax.broadcasted_iota(jnp.int32, (tm, LANES), 1)
    low_half = lane < HEAD_DIM
    grp = grp_ref[...]
    scale = HEAD_DIM ** -0.5 * LOG2E
    for c in range(n_heads // 2):
        cols = slice(c * LANES, (c + 1) * LANES)
        q = q_ref[:, cols].astype(F32)
        k = k_ref[:, cols].astype(F32)
        v = v_ref[:, cols]
        qn = q * lax.rsqrt(_dot2_rhs(q * q, grp) + EPS) * gq_ref[...] * scale
        kn = k * lax.rsqrt(_dot2_rhs(k * k, grp) + EPS) * gk_ref[...]
        for par in range(2):
            h = 2 * c + par
            data = low_half if par == 0 else jnp.logical_not(low_half)
            ext = slice(h * LANES, (h + 1) * LANES)
            qa_ref[0, h] = jnp.where(data, qn, ext_q[:, ext]).astype(BF16)
            ka_ref[0, h] = jnp.where(data, kn, ext_k[:, ext]).astype(BF16)
            va_ref[0, h] = jnp.where(data, v, cv_ref[par:par + 1, :].astype(BF16))


def _fox_layout_constants(n_heads):
    sq = np.zeros((3 * LANES, n_heads * LANES), np.float32)
    sk = np.zeros((3 * LANES, n_heads * LANES), np.float32)
    cq = np.zeros((1, n_heads * LANES), np.float32)
    ck = np.zeros((1, n_heads * LANES), np.float32)
    cv = np.zeros((2, LANES), np.float32)
    for h in range(n_heads):
        x0 = h * LANES + (HEAD_DIM if h % 2 == 0 else 0)
        for t in range(3):
            sq[t * LANES + h, x0 + t] = 1.0
            sk[t * LANES + h, x0 + 3 + t] = -1.0
        cq[0, x0 + 3:x0 + 6] = 1.0
        ck[0, x0:x0 + 3] = 1.0
    cv[0, HEAD_DIM] = 1.0
    cv[1, 0] = 1.0
    return (jnp.asarray(sq, BF16), jnp.asarray(sk, BF16), jnp.asarray(cq), jnp.asarray(ck), jnp.asarray(cv))


def _group_mean_matrix():
    lane = np.arange(LANES)
    return jnp.asarray(((lane[:, None] // HEAD_DIM) == (lane[None, :] // HEAD_DIM)) / HEAD_DIM, BF16)


def _fox_prep(proj, gates, f_bias, q_norm, k_norm, batch, seq, col0):
    width = 512
    n_heads = width // HEAD_DIM
    tm = PREP_ROWS
    spb = seq // tm
    col = lambda j: pl.BlockSpec((tm, width), lambda b, s, j=j: (b * spb + s, col0 + j))
    full = lambda a: pl.BlockSpec(a.shape, lambda b, s: (0,) * a.ndim)
    bias = jnp.zeros((1, LANES), F32).at[0, :n_heads].set(f_bias)
    gq = jnp.tile(q_norm, LANES // HEAD_DIM).reshape(1, LANES)
    gk = jnp.tile(k_norm, LANES // HEAD_DIM).reshape(1, LANES)
    tril = jnp.asarray(np.tril(np.ones((tm, tm), np.float32)), BF16)
    consts = _fox_layout_constants(n_heads) + (_group_mean_matrix(),)
    out = jax.ShapeDtypeStruct((batch, n_heads, seq, LANES), BF16)
    ospec = pl.BlockSpec((1, n_heads, tm, LANES), lambda b, s: (b, 0, s, 0))
    return pl.pallas_call(
        _fox_prep_kernel,
        grid=(batch, spb),
        in_specs=[col(0), col(1), col(2),
                  pl.BlockSpec((tm, LANES), lambda b, s: (b * spb + s, 0)),
                  full(bias), full(gq), full(gk), full(tril)] + [full(a) for a in consts],
        out_specs=[ospec, ospec, ospec],
        out_shape=[out, out, out],
        scratch_shapes=[pltpu.VMEM((1, LANES), F32)],
        compiler_params=pltpu.CompilerParams(
            dimension_semantics=("parallel", "arbitrary"), vmem_limit_bytes=VMEM_LIMIT),
    )(proj, proj, proj, gates, bias, gq, gk, tril, *consts)


def _tri_tables(nq):
    qi = [q for q in range(nq) for _ in range(q + 1)]
    ki = [k for q in range(nq) for k in range(q + 1)]
    return jnp.asarray(qi, jnp.int32), jnp.asarray(ki, jnp.int32)


def _fox_attn_kernel(qt_ref, kt_ref, q_ref, k_ref, v_ref, o_ref, m_ref, acc_ref):
    p_idx = pl.program_id(2)
    qi = qt_ref[p_idx]
    ki = kt_ref[p_idx]
    hp = q_ref.shape[1]
    t = q_ref.shape[2]

    @pl.when(ki == 0)
    def _():
        m_ref[...] = jnp.full_like(m_ref, -jnp.inf)
        acc_ref[...] = jnp.zeros_like(acc_ref)

    def step(masked):
        for h in range(hp):
            s = _dot_nt(q_ref[0, h], k_ref[0, h])
            if masked:
                row = lax.broadcasted_iota(jnp.int32, (t, t), 0)
                colm = lax.broadcasted_iota(jnp.int32, (t, t), 1)
                s = jnp.where(colm <= row, s, -jnp.inf)
            m_old = m_ref[h]
            m_new = jnp.maximum(m_old, jnp.max(s, axis=-1, keepdims=True))
            p = jnp.exp2(s - m_new)
            acc_ref[h] = (jnp.exp2(m_old - m_new) * acc_ref[h]
                          + jnp.dot(p.astype(BF16), v_ref[0, h], preferred_element_type=F32))
            m_ref[h] = m_new

    @pl.when(ki < qi)
    def _():
        step(False)

    @pl.when(ki == qi)
    def _():
        step(True)
        _fox_finalize(acc_ref, o_ref)


def _fox_finalize(acc_ref, o_ref):
    a0 = acc_ref[0]
    a1 = acc_ref[1]
    lane = lax.broadcasted_iota(jnp.int32, a0.shape, 1)
    o_ref[0] = jnp.where(lane < HEAD_DIM, a0 / a0[:, HEAD_DIM:HEAD_DIM + 1], a1 / a1[:, 0:1]).astype(o_ref.dtype)


def _tile_plan(n_sub, diagonal):
    plan = []
    for qb in range(n_sub):
        if not diagonal:
            plan.append((qb, 0, n_sub, False))
        else:
            if qb > 0:
                plan.append((qb, 0, qb, False))
            plan.append((qb, qb, qb + 1, True))
    return plan


def _fox_fast_kernel(qt_ref, kt_ref, q_ref, k_ref, v_ref, o_ref, acc_ref):
    p_idx = pl.program_id(2)
    qi = qt_ref[p_idx]
    ki = kt_ref[p_idx]
    hp = q_ref.shape[1]
    sb = ATTN_FAST_SUB
    n_sub = q_ref.shape[2] // sb

    @pl.when(ki == 0)
    def _():
        acc_ref[...] = jnp.zeros_like(acc_ref)

    def tile(diagonal):
        for h in range(hp):
            for qb, k0, k1, masked in _tile_plan(n_sub, diagonal):
                rows = pl.ds(qb * sb, sb)
                cols = pl.ds(k0 * sb, (k1 - k0) * sb)
                s = _dot_nt(q_ref[0, h, rows, :], k_ref[0, h, cols, :])
                if masked:
                    row = lax.broadcasted_iota(jnp.int32, (sb, sb), 0)
                    colm = lax.broadcasted_iota(jnp.int32, (sb, sb), 1)
                    s = jnp.where(colm <= row, s, -jnp.inf)
                p = jnp.exp2(s).astype(BF16)
                acc_ref[h, rows, :] += jnp.dot(p, v_ref[0, h, cols, :], preferred_element_type=F32)

    @pl.when(ki < qi)
    def _():
        tile(False)

    @pl.when(ki == qi)
    def _():
        tile(True)
        _fox_finalize(acc_ref, o_ref)


def _fox_attention(qa, ka, va, fast):
    batch, n_heads, seq, _ = qa.shape
    t = ATTN_FAST_TILE if fast else ATTN_TILE
    hp = 2
    nq = seq // t
    qt, kt = _tri_tables(nq)
    qspec = pl.BlockSpec((1, hp, t, LANES), lambda b, g, p, qt, kt: (b, g, qt[p], 0))
    kspec = pl.BlockSpec((1, hp, t, LANES), lambda b, g, p, qt, kt: (b, g, kt[p], 0))
    scratch = [pltpu.VMEM((hp, t, LANES), F32)]
    if not fast:
        scratch = [pltpu.VMEM((hp, t, 1), F32)] + scratch
    return pl.pallas_call(
        _fox_fast_kernel if fast else _fox_attn_kernel,
        grid_spec=pltpu.PrefetchScalarGridSpec(
            num_scalar_prefetch=2,
            grid=(batch, n_heads // hp, int(qt.shape[0])),
            in_specs=[qspec, kspec, kspec],
            out_specs=pl.BlockSpec((1, t, hp * HEAD_DIM), lambda b, g, p, qt, kt: (b, qt[p], g)),
            scratch_shapes=scratch),
        out_shape=jax.ShapeDtypeStruct((batch, seq, n_heads * HEAD_DIM), BF16),
        compiler_params=pltpu.CompilerParams(
            dimension_semantics=("parallel", "parallel", "arbitrary"), vmem_limit_bytes=VMEM_LIMIT),
    )(qt, kt, qa, ka, va)


def _logit_bound(q_gain, k_gain):
    return HEAD_DIM ** 0.5 * jnp.max(jnp.abs(q_gain)) * jnp.max(jnp.abs(k_gain))


def _diff_prep_kernel(q_ref, k_ref, v_ref, pos_ref, invf_ref, gq_ref, gk_ref, grp_ref, sel_ref,
                      qm_ref, k2_ref, va_ref, cs_ref, sn_ref):
    n_heads = k2_ref.shape[1]
    tm = q_ref.shape[0]
    ang = pos_ref[...].astype(F32) * invf_ref[...]
    cs_c = jnp.cos(ang)
    sn_c = jnp.sin(ang)
    per_row = LANES // (HEAD_DIM // 2)
    for j in range(per_row):
        cs_ref[pl.ds(j, tm // per_row, stride=per_row), :] = _dot3_rhs(cs_c, sel_ref[j])
        sn_ref[pl.ds(j, tm // per_row, stride=per_row), :] = _dot3_rhs(sn_c, sel_ref[j])
    lane = lax.broadcasted_iota(jnp.int32, (tm, LANES), 1)
    first = (lane % HEAD_DIM) < (HEAD_DIM // 2)
    cs = cs_ref[...]
    sn = sn_ref[...]
    sn = jnp.where(first, -sn, sn)
    grp = grp_ref[...]
    scale = HEAD_DIM ** -0.5 * LOG2E
    zero = jnp.zeros((tm, LANES), F32)
    onecol = jnp.where(lane == 0, 1.0, 0.0).astype(BF16)

    def norm_rope(x, gain):
        ms = _dot2_rhs(x * x, grp)
        y = x * lax.rsqrt(ms + EPS) * gain
        yr = jnp.where(first, pltpu.roll(y, LANES - HEAD_DIM // 2, 1), pltpu.roll(y, HEAD_DIM // 2, 1))
        return y * cs + yr * sn

    for h in range(n_heads):
        cols = slice(h * LANES, (h + 1) * LANES)
        qr = norm_rope(q_ref[:, cols].astype(F32), gq_ref[...]) * scale
        kr = norm_rope(k_ref[:, cols].astype(F32), gk_ref[...])
        qm_ref[0, h, 0] = jnp.where(lane < HEAD_DIM, qr, zero).astype(BF16)
        qm_ref[0, h, 1] = jnp.where(lane < HEAD_DIM, zero, qr).astype(BF16)
        k2_ref[0, h] = kr.astype(BF16)
        va_ref[0, h] = jnp.concatenate([v_ref[:, cols].astype(BF16), onecol], axis=-1)


def _diff_prep(proj, positions, q_norm, k_norm, batch, seq):
    n = proj.shape[0]
    width = proj.shape[1] // 3
    n_heads = width // LANES
    tm = PREP_ROWS
    spb = seq // tm
    col = lambda j: pl.BlockSpec((tm, width), lambda b, s, j=j: (b * spb + s, j))
    full = lambda a: pl.BlockSpec(a.shape, lambda b, s: (0,) * a.ndim)
    half = HEAD_DIM // 2
    inv_freq = ROPE_THETA ** (-jnp.arange(half, dtype=F32) / half)
    invf = jnp.tile(inv_freq, LANES // half).reshape(1, LANES)
    gq = jnp.tile(q_norm, LANES // HEAD_DIM).reshape(1, LANES)
    gk = jnp.tile(k_norm, LANES // HEAD_DIM).reshape(1, LANES)
    grp = _group_mean_matrix()
    per_row = LANES // half
    pos = jnp.repeat(positions.reshape(n // per_row, per_row).astype(jnp.int32), half, axis=1)
    lane = np.arange(LANES)
    sel = jnp.asarray(np.stack([(lane[:, None] == j * half + lane[None, :] % half) for j in range(per_row)]),
                      BF16)
    return pl.pallas_call(
        _diff_prep_kernel,
        grid=(batch, spb),
        in_specs=[col(0), col(1), col(2),
                  pl.BlockSpec((tm // per_row, LANES), lambda b, s: (b * spb + s, 0)),
                  full(invf), full(gq), full(gk), full(grp), full(sel)],
        out_specs=[pl.BlockSpec((1, n_heads, 2, tm, LANES), lambda b, s: (b, 0, 0, s, 0)),
                   pl.BlockSpec((1, n_heads, tm, LANES), lambda b, s: (b, 0, s, 0)),
                   pl.BlockSpec((1, n_heads, tm, 2 * LANES), lambda b, s: (b, 0, s, 0))],
        out_shape=[jax.ShapeDtypeStruct((batch, n_heads, 2, seq, LANES), BF16),
                   jax.ShapeDtypeStruct((batch, n_heads, seq, LANES), BF16),
                   jax.ShapeDtypeStruct((batch, n_heads, seq, 2 * LANES), BF16)],
        scratch_shapes=[pltpu.VMEM((tm, LANES), F32), pltpu.VMEM((tm, LANES), F32)],
        compiler_params=pltpu.CompilerParams(
            dimension_semantics=("parallel", "parallel"), vmem_limit_bytes=VMEM_LIMIT),
    )(proj, proj, proj, pos, invf, gq, gk, grp, sel)


def _diff_attn_kernel(lambda_init, qt_ref, kt_ref, q_ref, k_ref, v_ref, lam_ref, sub_ref,
                      o_ref, m_ref, acc_ref):
    p_idx = pl.program_id(2)
    qi = qt_ref[p_idx]
    ki = kt_ref[p_idx]
    t = k_ref.shape[2]
    dv = o_ref.shape[2]

    @pl.when(ki == 0)
    def _():
        m_ref[...] = jnp.full_like(m_ref, -jnp.inf)
        acc_ref[...] = jnp.zeros_like(acc_ref)

    def step(masked):
        for m in range(2):
            s = _dot_nt(q_ref[0, 0, m], k_ref[0, 0])
            if masked:
                row = lax.broadcasted_iota(jnp.int32, (t, t), 0) // CHUNK
                colm = lax.broadcasted_iota(jnp.int32, (t, t), 1) // CHUNK
                s = jnp.where(colm <= row, s, -jnp.inf)
            m_old = m_ref[m]
            m_new = jnp.maximum(m_old, jnp.max(s, axis=-1, keepdims=True))
            p = jnp.exp2(s - m_new)
            acc_ref[m] = (jnp.exp2(m_old - m_new) * acc_ref[m]
                          + jnp.dot(p.astype(BF16), v_ref[0, 0], preferred_element_type=F32))
            m_ref[m] = m_new

    @pl.when(ki < qi)
    def _():
        step(False)

    @pl.when(ki == qi)
    def _():
        step(True)
        _diff_finalize(lambda_init, acc_ref, lam_ref, sub_ref, o_ref)


def _diff_finalize(lambda_init, acc_ref, lam_ref, sub_ref, o_ref):
    dv = o_ref.shape[2]
    lp = lam_ref[...]
    lam = (jnp.exp(jnp.sum(lp[0:1] * lp[1:2], axis=-1, keepdims=True))
           - jnp.exp(jnp.sum(lp[2:3] * lp[3:4], axis=-1, keepdims=True)) + lambda_init)
    a0 = acc_ref[0]
    a1 = acc_ref[1]
    o = a0[:, :dv] / a0[:, dv:dv + 1] - lam * (a1[:, :dv] / a1[:, dv:dv + 1])
    ms = jnp.mean(o * o, axis=-1, keepdims=True)
    o_ref[0] = ((o * lax.rsqrt(ms + EPS) * sub_ref[...]) * (1.0 - lambda_init)).astype(o_ref.dtype)


def _diff_fast_kernel(lambda_init, qt_ref, kt_ref, q_ref, k_ref, v_ref, lam_ref, sub_ref, o_ref, acc_ref):
    p_idx = pl.program_id(2)
    qi = qt_ref[p_idx]
    ki = kt_ref[p_idx]
    sb = ATTN_FAST_SUB
    n_sub = k_ref.shape[2] // sb

    @pl.when(ki == 0)
    def _():
        acc_ref[...] = jnp.zeros_like(acc_ref)

    def tile(diagonal):
        for m in range(2):
            for qb, k0, k1, masked in _tile_plan(n_sub, diagonal):
                rows = pl.ds(qb * sb, sb)
                cols = pl.ds(k0 * sb, (k1 - k0) * sb)
                s = _dot_nt(q_ref[0, 0, m, rows, :], k_ref[0, 0, cols, :])
                if masked:
                    row = lax.broadcasted_iota(jnp.int32, (sb, sb), 0) // CHUNK
                    colm = lax.broadcasted_iota(jnp.int32, (sb, sb), 1) // CHUNK
                    s = jnp.where(colm <= row, s, -jnp.inf)
                p = jnp.exp2(s).astype(BF16)
                acc_ref[m, rows, :] += jnp.dot(p, v_ref[0, 0, cols, :], preferred_element_type=F32)

    @pl.when(ki < qi)
    def _():
        tile(False)

    @pl.when(ki == qi)
    def _():
        tile(True)
        _diff_finalize(lambda_init, acc_ref, lam_ref, sub_ref, o_ref)


def _diff_attention(qm, k2, va, lam_params, subln, lambda_init, fast):
    batch, n_heads, seq, _ = k2.shape
    dv = va.shape[3] // 2
    t = ATTN_FAST_TILE if fast else ATTN_TILE
    nq = seq // t
    qt, kt = _tri_tables(nq)
    lamp = jnp.zeros((8, LANES), F32).at[:4, :HEAD_DIM].set(lam_params)
    sub = subln.reshape(1, dv)
    scratch = [pltpu.VMEM((2, t, 2 * dv), F32)]
    if not fast:
        scratch = [pltpu.VMEM((2, t, 1), F32)] + scratch
    return pl.pallas_call(
        functools.partial(_diff_fast_kernel if fast else _diff_attn_kernel, lambda_init),
        grid_spec=pltpu.PrefetchScalarGridSpec(
            num_scalar_prefetch=2,
            grid=(batch, n_heads, int(qt.shape[0])),
            in_specs=[pl.BlockSpec((1, 1, 2, t, LANES), lambda b, h, p, qt, kt: (b, h, 0, qt[p], 0)),
                      pl.BlockSpec((1, 1, t, LANES), lambda b, h, p, qt, kt: (b, h, kt[p], 0)),
                      pl.BlockSpec((1, 1, t, 2 * dv), lambda b, h, p, qt, kt: (b, h, kt[p], 0)),
                      pl.BlockSpec((8, LANES), lambda b, h, p, qt, kt: (0, 0)),
                      pl.BlockSpec((1, dv), lambda b, h, p, qt, kt: (0, 0))],
            out_specs=pl.BlockSpec((1, t, dv), lambda b, h, p, qt, kt: (b, qt[p], h)),
            scratch_shapes=scratch),
        out_shape=jax.ShapeDtypeStruct((batch, seq, n_heads * dv), BF16),
        compiler_params=pltpu.CompilerParams(
            dimension_semantics=("parallel", "parallel", "arbitrary"), vmem_limit_bytes=VMEM_LIMIT),
    )(qt, kt, qm, k2, va, lamp, sub)


def _store_token_tiles(ref, value):
    t, width = value.shape
    s = width // LANES
    for j in range(s):
        ref[pl.ds(j, t, stride=s), :] = value[:, j * LANES:(j + 1) * LANES]


def _load_token_tiles(ref, first_row, t, s):
    return jnp.concatenate([ref[pl.ds(first_row + j, t, stride=s), :] for j in range(s)], axis=-1)


def _router_kernel(h_ref, g_ref, whi_ref, wlo_ref, xn_ref, route_ref):
    x = h_ref[...]
    tm = x.shape[0]
    xn = x * lax.rsqrt(jnp.mean(x * x, axis=-1, keepdims=True) + EPS) * g_ref[...]
    _store_token_tiles(xn_ref, xn)
    xh = xn.astype(BF16)
    xl = (xn - xh.astype(F32)).astype(BF16)
    d = lambda a, b: jnp.dot(a, b[...], preferred_element_type=F32)
    logits = d(xh, whi_ref) + (d(xl, whi_ref) + d(xh, wlo_ref))
    lane = lax.broadcasted_iota(jnp.int32, (tm, LANES), 1)
    neg = jnp.full((tm, LANES), -jnp.inf, F32)
    big = jnp.full((tm, LANES), LANES, jnp.int32)

    def top1(vals):
        m = jnp.max(vals, axis=-1, keepdims=True)
        idx = jnp.min(jnp.where(vals == m, lane, big), axis=-1, keepdims=True)
        return m, idx

    grp_logits = jnp.where(lane < N_GROUPS, logits, neg)
    mg, gidx = top1(grp_logits)
    p_g = 1.0 / jnp.sum(jnp.exp(grp_logits - mg), axis=-1, keepdims=True)
    e_lane = lane - N_GROUPS
    in_grp = (e_lane >= gidx * EXPERTS_PER_GROUP) & (e_lane < (gidx + 1) * EXPERTS_PER_GROUP)
    sel = jnp.where(in_grp, logits, neg)
    m1, i1 = top1(sel)
    m2, i2 = top1(jnp.where(lane == i1, neg, sel))
    r = jnp.exp(m2 - m1)
    w1 = p_g / (1.0 + r)
    w2 = p_g * r / (1.0 + r)
    zero = jnp.zeros((tm, LANES), F32)
    route_ref[...] = jnp.where(lane == 0, (i1 - N_GROUPS).astype(F32),
                     jnp.where(lane == 1, (i2 - N_GROUPS).astype(F32),
                     jnp.where(lane == 2, w1, jnp.where(lane == 3, w2, zero))))


def _router(h, gain, w_group, w_expert):
    n, d = h.shape
    tm = ROUTER_ROWS
    wr = jnp.zeros((d, LANES), F32)
    wr = wr.at[:, :N_GROUPS].set(w_group)
    wr = wr.at[:, N_GROUPS:N_GROUPS + N_EXPERTS].set(
        jnp.transpose(w_expert, (1, 0, 2)).reshape(d, N_EXPERTS))
    w_hi = wr.astype(BF16)
    w_lo = (wr - w_hi.astype(F32)).astype(BF16)
    return pl.pallas_call(
        _router_kernel,
        grid=(n // tm,),
        in_specs=[pl.BlockSpec((tm, d), lambda i: (i, 0)),
                  pl.BlockSpec((1, d), lambda i: (0, 0)),
                  pl.BlockSpec((d, LANES), lambda i: (0, 0)),
                  pl.BlockSpec((d, LANES), lambda i: (0, 0))],
        out_specs=[pl.BlockSpec((tm * d // LANES, LANES), lambda i: (i, 0)),
                   pl.BlockSpec((tm, LANES), lambda i: (i, 0))],
        out_shape=[jax.ShapeDtypeStruct((n * d // LANES, LANES), F32),
                   jax.ShapeDtypeStruct((n, LANES), F32)],
        compiler_params=pltpu.CompilerParams(
            dimension_semantics=("parallel",), vmem_limit_bytes=VMEM_LIMIT),
    )(h, gain.reshape(1, d), w_hi, w_lo)


def _dispatch_tables(expert_ids, tm):
    n = expert_ids.shape[0]
    pairs = expert_ids.reshape(-1)
    n_pairs = pairs.shape[0]
    n_tiles = n_pairs // tm + N_EXPERTS
    onehot = (pairs[:, None] == jnp.arange(N_EXPERTS, dtype=jnp.int32)[None, :]).astype(jnp.int32)
    csum = jnp.cumsum(onehot, axis=0)
    rank = jnp.sum((csum - onehot) * onehot, axis=1)
    counts = csum[-1]
    padded = ((counts + tm - 1) // tm) * tm
    ends = jnp.cumsum(padded)
    starts = ends - padded
    pos = jnp.sum(onehot * starts[None, :], axis=1) + rank
    tile_start = jnp.arange(n_tiles, dtype=jnp.int32) * tm
    tile_expert = jnp.minimum(jnp.sum(tile_start[:, None] >= ends[None, :], axis=1),
                              N_EXPERTS - 1).astype(jnp.int32)
    n_valid = (ends[-1] // tm).astype(jnp.int32).reshape(1)
    pad_start = (starts + counts).astype(jnp.int32)
    pad_count = (padded - counts).astype(jnp.int32)
    return pos.reshape(n, 2).astype(jnp.int32), tile_expert, n_valid, n_tiles, pad_start, pad_count


def _token_copy(src_hbm, tok, dst_ref, r, sem):
    src = src_hbm.at[pl.ds(pl.multiple_of(tok * TILE_ROWS, TILE_ROWS), TILE_ROWS)]
    first = r * TILE_ROWS if isinstance(r, int) else pl.multiple_of(r * TILE_ROWS, TILE_ROWS)
    return pltpu.make_async_copy(src, dst_ref.at[pl.ds(first, TILE_ROWS)], sem)


def _gather_tokens(src_hbm, idx_ref, dst_ref, sem, n_tokens):
    def body(r, carry):
        _token_copy(src_hbm, idx_ref[0, 0, r], dst_ref, r, sem).start()
        return carry
    lax.fori_loop(0, n_tokens, body, 0, unroll=8)


def _wait_tokens(src_hbm, dst_ref, sem):
    pltpu.make_async_copy(src_hbm.at[pl.ds(0, dst_ref.shape[0])], dst_ref, sem).wait()


def _dispatch_kernel(ps_ref, pc_ref, nv_ref, pos_ref, x_ref, xs_hbm, zero_blk, sem, pad_sem):
    i = pl.program_id(0)
    tokens = pos_ref.shape[2] // 2

    def slot_tile(slot):
        return xs_hbm.at[pl.ds(pl.multiple_of(slot * TILE_ROWS, TILE_ROWS), TILE_ROWS)]

    for r in range(2 * tokens):
        src = x_ref.at[pl.ds((r % tokens) * TILE_ROWS, TILE_ROWS)]
        pltpu.make_async_copy(src, slot_tile(pos_ref[0, 0, r]), sem).start(priority=r % 2)
    rows = 2 * tokens * TILE_ROWS
    pltpu.make_async_copy(xs_hbm.at[pl.ds(0, rows)], xs_hbm.at[pl.ds(0, rows)], sem).wait()

    @pl.when(i == pl.num_programs(0) - 1)
    def _():
        zero_blk[...] = jnp.zeros_like(zero_blk)
        zero_tile = zero_blk.at[pl.ds(0, TILE_ROWS)]
        for e in range(ps_ref.shape[0]):
            first = ps_ref[e]
            count = pc_ref[e]

            def fill(j, carry):
                pltpu.make_async_copy(zero_tile, slot_tile(first + j), pad_sem).start()
                return carry

            def drain(j, carry):
                pltpu.make_async_copy(zero_tile, slot_tile(first), pad_sem).wait()
                return carry

            lax.fori_loop(0, count, fill, 0)
            lax.fori_loop(0, count, drain, 0)

        block_rows = zero_blk.shape[0]
        n_blocks = xs_hbm.shape[0] // block_rows

        def block(t):
            return xs_hbm.at[pl.ds(pl.multiple_of(t * block_rows, block_rows), block_rows)]

        def fill_block(t, carry):
            pltpu.make_async_copy(zero_blk, block(t), pad_sem).start()
            return carry

        def drain_block(t, carry):
            pltpu.make_async_copy(zero_blk, block(t), pad_sem).wait()
            return carry

        lax.fori_loop(nv_ref[0], n_blocks, fill_block, 0)
        lax.fori_loop(nv_ref[0], n_blocks, drain_block, 0)


def _moe_dispatch(xn, pos, pad_start, pad_count, n_valid, n_tiles, tm):
    n = pos.shape[0]
    tb = COMBINE_ROWS
    steps = n // tb
    pos_tab = jnp.transpose(pos.reshape(steps, tb, 2), (0, 2, 1)).reshape(steps, 1, 2 * tb)
    return pl.pallas_call(
        _dispatch_kernel,
        grid_spec=pltpu.PrefetchScalarGridSpec(
            num_scalar_prefetch=3,
            grid=(steps,),
            in_specs=[pl.BlockSpec((1, 1, 2 * tb), lambda i, ps, pc, nv: (i, 0, 0), memory_space=pltpu.SMEM),
                      pl.BlockSpec((tb * TILE_ROWS, LANES), lambda i, ps, pc, nv: (i, 0))],
            out_specs=pl.BlockSpec(memory_space=pl.ANY),
            scratch_shapes=[pltpu.VMEM((tm * TILE_ROWS, LANES), F32),
                            pltpu.SemaphoreType.DMA, pltpu.SemaphoreType.DMA]),
        out_shape=jax.ShapeDtypeStruct((n_tiles * tm * TILE_ROWS, LANES), F32),
        compiler_params=pltpu.CompilerParams(
            dimension_semantics=("arbitrary",), vmem_limit_bytes=VMEM_LIMIT),
    )(pad_start, pad_count, n_valid, pos_tab, xn)


def _moe_kernel(te_ref, nv_ref, x_ref, wg_ref, wu_ref, wd_ref, o_ref, wg_b, wu_b, wd_b):
    i = pl.program_id(0)
    tm = x_ref.shape[0] // TILE_ROWS
    n_valid = nv_ref[0]
    new_expert = jnp.logical_or(i == 0, te_ref[i] != te_ref[jnp.maximum(i - 1, 0)])

    @pl.when(jnp.logical_and(i < n_valid, new_expert))
    def _():
        wg_b[...] = wg_ref[0].astype(BF16)
        wu_b[...] = wu_ref[0].astype(BF16)
        wd_b[...] = wd_ref[0].astype(BF16)

    @pl.when(i < n_valid)
    def _():
        x = _load_token_tiles(x_ref, 0, tm, TILE_ROWS).astype(BF16)
        g = jnp.dot(x, wg_b[...], preferred_element_type=F32)
        u = jnp.dot(x, wu_b[...], preferred_element_type=F32)
        hid = (g * _sigmoid(g) * u).astype(BF16)
        _store_token_tiles(o_ref, jnp.dot(hid, wd_b[...], preferred_element_type=F32))

    @pl.when(i >= n_valid)
    def _():
        o_ref[...] = jnp.zeros_like(o_ref)


def _moe_experts(xs, tile_expert, n_valid, tm, w_gate, w_up, w_down):
    n_exp, d, f = w_gate.shape
    assert d == TILE_ROWS * LANES
    n_tiles = xs.shape[0] // (tm * TILE_ROWS)
    x_index = lambda i, te, nv: (jnp.minimum(i, nv[0] - 1), 0)
    return pl.pallas_call(
        _moe_kernel,
        grid_spec=pltpu.PrefetchScalarGridSpec(
            num_scalar_prefetch=2,
            grid=(n_tiles,),
            in_specs=[
                pl.BlockSpec((tm * TILE_ROWS, LANES), x_index),
                pl.BlockSpec((1, d, f), lambda i, te, nv: (te[i], 0, 0)),
                pl.BlockSpec((1, d, f), lambda i, te, nv: (te[i], 0, 0)),
                pl.BlockSpec((1, f, d), lambda i, te, nv: (te[i], 0, 0))],
            out_specs=pl.BlockSpec((tm * TILE_ROWS, LANES), lambda i, te, nv: (i, 0)),
            scratch_shapes=[pltpu.VMEM((d, f), BF16), pltpu.VMEM((d, f), BF16), pltpu.VMEM((f, d), BF16)]),
        out_shape=jax.ShapeDtypeStruct((n_tiles * tm * TILE_ROWS, LANES), F32),
        compiler_params=pltpu.CompilerParams(
            dimension_semantics=("arbitrary",), vmem_limit_bytes=VMEM_LIMIT),
    )(tile_expert, n_valid, xs, w_gate, w_up, w_down)


def _combine_kernel(pos_ref, pos_next_ref, h_ref, route_ref, y_hbm, o_ref, ybuf, sems):
    i = pl.program_id(0)
    n_steps = pl.num_programs(0)
    tokens = ybuf.shape[1] // TILE_ROWS
    slot = i % 2

    @pl.when(i == 0)
    def _():
        _gather_tokens(y_hbm, pos_ref, ybuf.at[0], sems.at[0], tokens)

    @pl.when(i + 1 < n_steps)
    def _():
        for r in range(tokens):
            _token_copy(y_hbm, pos_next_ref[0, 0, r], ybuf.at[1 - slot], r,
                        sems.at[1 - slot]).start(priority=r % 2)

    _wait_tokens(y_hbm, ybuf.at[slot], sems.at[slot])
    tc = tokens // 2
    w = route_ref[...]
    first = _load_token_tiles(ybuf.at[slot], 0, tc, TILE_ROWS)
    second = _load_token_tiles(ybuf.at[slot], tc * TILE_ROWS, tc, TILE_ROWS)
    o_ref[...] = h_ref[...] + w[:, 2:3] * first + w[:, 3:4] * second


def _moe_combine(h, route, pos, y_sorted):
    n, d = h.shape
    tc = COMBINE_ROWS
    steps = n // tc
    pos_tab = jnp.transpose(pos.reshape(steps, tc, 2), (0, 2, 1)).reshape(steps, 1, 2 * tc)
    return pl.pallas_call(
        _combine_kernel,
        grid=(steps,),
        in_specs=[pl.BlockSpec((1, 1, 2 * tc), lambda i: (i, 0, 0), memory_space=pltpu.SMEM),
                  pl.BlockSpec((1, 1, 2 * tc), lambda i: (jnp.minimum(i + 1, steps - 1), 0, 0),
                               memory_space=pltpu.SMEM),
                  pl.BlockSpec((tc, d), lambda i: (i, 0)),
                  pl.BlockSpec((tc, LANES), lambda i: (i, 0)),
                  pl.BlockSpec(memory_space=pl.ANY)],
        out_specs=pl.BlockSpec((tc, d), lambda i: (i, 0)),
        out_shape=jax.ShapeDtypeStruct((n, d), F32),
        scratch_shapes=[pltpu.VMEM((2, 2 * tc * TILE_ROWS, LANES), F32), pltpu.SemaphoreType.DMA((2,))],
        compiler_params=pltpu.CompilerParams(
            dimension_semantics=("arbitrary",), vmem_limit_bytes=VMEM_LIMIT),
    )(pos_tab, pos_tab, h, route, y_sorted)


def _moe_layer(h, layer, gain, w_group, w_expert, w_gate, w_up, w_down):
    d = h.shape[1]
    f = w_gate.shape[-1]
    xn, route = _router(h, gain, w_group, w_expert)
    expert_ids = route[:, :2].astype(jnp.int32)
    pos, tile_expert, n_valid, n_tiles, pad_start, pad_count = _dispatch_tables(expert_ids, MOE_TM)
    xs = _moe_dispatch(xn, pos, pad_start, pad_count, n_valid, n_tiles, MOE_TM)
    y_sorted = _moe_experts(xs, tile_expert + layer * N_EXPERTS, n_valid, MOE_TM,
                            w_gate.reshape(-1, d, f), w_up.reshape(-1, d, f), w_down.reshape(-1, f, d))
    return _moe_combine(h, route, pos, y_sorted)


def _even_layer(h, batch, seq, gain, w_in, w_out, lb, f_bias, out_norm, q_norm, k_norm):
    d = h.shape[1]
    n_main = w_in.shape[1] - f_bias.shape[0]
    w_main = w_in[:, :n_main].astype(BF16)
    w_gate = jnp.zeros((d, LANES), F32).at[:, :f_bias.shape[0]].set(w_in[:, n_main:])
    proj, gates = _norm_proj(h, gain, w_main, w_gate)
    o_a = _hgrn2(proj, lb, out_norm, batch, seq)
    qa, ka, va = _fox_prep(proj, gates, f_bias, q_norm, k_norm, batch, seq, col0=4)
    o_b = lax.cond(_logit_bound(q_norm, k_norm) <= LOGIT_BOUND_MAX,
                   functools.partial(_fox_attention, fast=True),
                   functools.partial(_fox_attention, fast=False), qa, ka, va).reshape(batch * seq, -1)
    wo = w_out.astype(BF16)
    ka_dim = o_a.shape[1]
    return _proj_residual([(o_a, wo[:ka_dim]), (o_b, wo[ka_dim:])], h)


def _odd_layer(h, positions, batch, seq, gain, w_in, w_out, q_norm, k_norm, lam_params, subln, lambda_init):
    proj = _norm_proj(h, gain, w_in.astype(BF16))
    qm, k2, va = _diff_prep(proj, positions, q_norm, k_norm, batch, seq)
    attn = lambda fast: functools.partial(_diff_attention, lam_params=lam_params, subln=subln,
                                          lambda_init=lambda_init, fast=fast)
    o = lax.cond(_logit_bound(q_norm, k_norm) <= LOGIT_BOUND_MAX,
                 attn(True), attn(False), qm, k2, va).reshape(batch * seq, -1)
    return _proj_residual([(o, w_out.astype(BF16))], h)


def kernel(x, positions, hgrn_lb_logits, norm_mix, norm_ffn, even_w_in, even_w_out, fox_f_bias,
           hgrn_out_norm, fox_q_norm, fox_k_norm, odd_w_in, odd_w_out, diff_q_norm, diff_k_norm,
           diff_lambda_q1, diff_lambda_k1, diff_lambda_q2, diff_lambda_k2, diff_subln,
           moe_router_group, moe_router_expert, moe_w_gate, moe_w_up, moe_w_down):
    batch, seq, d = x.shape
    depth = norm_mix.shape[0]
    lower_bounds = jnp.cumsum(jax.nn.softmax(hgrn_lb_logits.astype(F32), axis=0), axis=0)
    h = x.reshape(batch * seq, d)
    for layer in range(depth):
        j = layer // 2
        if layer % 2 == 0:
            h = _even_layer(h, batch, seq, norm_mix[layer], even_w_in[j], even_w_out[j], lower_bounds[j],
                            fox_f_bias[j], hgrn_out_norm[j], fox_q_norm[j], fox_k_norm[j])
        else:
            lambda_init = 0.8 - 0.6 * math.exp(-0.3 * layer)
            lam_params = jnp.stack([diff_lambda_q1[j], diff_lambda_k1[j],
                                    diff_lambda_q2[j], diff_lambda_k2[j]]).astype(F32)
            h = _odd_layer(h, positions, batch, seq, norm_mix[layer], odd_w_in[j], odd_w_out[j],
                           diff_q_norm[j], diff_k_norm[j], lam_params, diff_subln[j], lambda_init)
        h = _moe_layer(h, layer, norm_ffn[layer], moe_router_group[layer], moe_router_expert[layer],
                       moe_w_gate, moe_w_up, moe_w_down)
    return h.reshape(batch, seq, d)
```

```python
import functools
import math

import numpy as np
import jax
import jax.numpy as jnp
from jax import lax
from jax.experimental import pallas as pl
from jax.experimental.pallas import tpu as pltpu

F32 = jnp.float32
BF16 = jnp.bfloat16

EPS = 1e-6
ROPE_THETA = 10000.0
CHUNK = 64
HEAD_DIM = 64
N_GROUPS = 4
EXPERTS_PER_GROUP = 8
N_EXPERTS = N_GROUPS * EXPERTS_PER_GROUP
LANES = 128
TILE_ROWS = 8

HGRN_CHUNK = 64
HGRN_ROWS = 256
ATTN_TILE = 512
ATTN_FAST_TILE = 2048
ATTN_FAST_SUB = 512
LOGIT_BOUND_MAX = 60.0
LOG2E = math.log2(math.e)
PREP_ROWS = 256
PROJ_TM = 512
ROUTER_ROWS = 256
MOE_TM = 512
COMBINE_ROWS = 512
VMEM_LIMIT = 56 * 1024 * 1024


def _split3(x):
    hi = x.astype(BF16)
    r1 = x - hi.astype(F32)
    mid = r1.astype(BF16)
    lo = (r1 - mid.astype(F32)).astype(BF16)
    return hi, mid, lo


def _dot3(const_bf16, x):
    hi, mid, lo = _split3(x)
    d = lambda b: jnp.dot(const_bf16, b, preferred_element_type=F32)
    return d(hi) + d(mid) + d(lo)


def _dot3_stacked(const3_bf16, x):
    return jnp.dot(const3_bf16, jnp.concatenate(_split3(x), axis=0), preferred_element_type=F32)


def _dot3_rhs(x, const_bf16):
    hi, mid, lo = _split3(x)
    d = lambda a: jnp.dot(a, const_bf16, preferred_element_type=F32)
    return d(hi) + d(mid) + d(lo)


def _dot2_rhs(x, const_bf16):
    hi = x.astype(BF16)
    lo = (x - hi.astype(F32)).astype(BF16)
    d = lambda a: jnp.dot(a, const_bf16, preferred_element_type=F32)
    return d(hi) + d(lo)


def _dot_nt(a, b):
    return lax.dot_general(a, b, (((1,), (1,)), ((), ())), preferred_element_type=F32)


def _dot_tn(a, b):
    return lax.dot_general(a, b, (((0,), (0,)), ((), ())), preferred_element_type=F32)


def _sigmoid(x):
    return 1.0 / (1.0 + jnp.exp(-x))


def _norm_proj_kernel(has_aux, x_ref, g_ref, w_ref, *rest):
    x = x_ref[...]
    ms = jnp.mean(x * x, axis=-1, keepdims=True)
    xn = x * lax.rsqrt(ms + EPS) * g_ref[...]
    xb = xn.astype(BF16)
    rest[-2 if has_aux else -1][...] = jnp.dot(xb, w_ref[...], preferred_element_type=F32).astype(BF16)
    if has_aux:
        whi_ref, wlo_ref, _, oaux_ref = rest
        xl = (xn - xb.astype(F32)).astype(BF16)
        d = lambda a, b: jnp.dot(a, b[...], preferred_element_type=F32)
        oaux_ref[...] = d(xb, whi_ref) + (d(xl, whi_ref) + d(xb, wlo_ref))


def _norm_proj(x, gain, w, w_aux=None):
    n, d = x.shape
    m = w.shape[1]
    tm = PROJ_TM
    has_aux = w_aux is not None
    in_specs = [pl.BlockSpec((tm, d), lambda i: (i, 0)),
                pl.BlockSpec((1, d), lambda i: (0, 0)),
                pl.BlockSpec((d, m), lambda i: (0, 0))]
    out_specs = [pl.BlockSpec((tm, m), lambda i: (i, 0))]
    out_shape = [jax.ShapeDtypeStruct((n, m), BF16)]
    args = [x, gain.reshape(1, d), w]
    if has_aux:
        w_hi = w_aux.astype(BF16)
        w_lo = (w_aux - w_hi.astype(F32)).astype(BF16)
        in_specs += [pl.BlockSpec((d, LANES), lambda i: (0, 0))] * 2
        out_specs.append(pl.BlockSpec((tm, LANES), lambda i: (i, 0)))
        out_shape.append(jax.ShapeDtypeStruct((n, LANES), F32))
        args += [w_hi, w_lo]
    res = pl.pallas_call(
        functools.partial(_norm_proj_kernel, has_aux),
        grid=(n // tm,),
        in_specs=in_specs, out_specs=out_specs, out_shape=out_shape,
        compiler_params=pltpu.CompilerParams(
            dimension_semantics=("parallel",), vmem_limit_bytes=VMEM_LIMIT),
    )(*args)
    return res if has_aux else res[0]


def _proj_res_kernel(n_in, *refs):
    h_ref = refs[2 * n_in]
    o_ref = refs[2 * n_in + 1]
    acc = h_ref[...]
    for t in range(n_in):
        acc = acc + jnp.dot(refs[2 * t][...], refs[2 * t + 1][...], preferred_element_type=F32)
    o_ref[...] = acc


def _proj_residual(pairs, h):
    n, d = h.shape
    tm = PROJ_TM
    in_specs, args = [], []
    for a, w in pairs:
        k = a.shape[1]
        in_specs += [pl.BlockSpec((tm, k), lambda i: (i, 0)),
                     pl.BlockSpec((k, d), lambda i: (0, 0))]
        args += [a, w]
    in_specs.append(pl.BlockSpec((tm, d), lambda i: (i, 0)))
    args.append(h)
    return pl.pallas_call(
        functools.partial(_proj_res_kernel, len(pairs)),
        grid=(n // tm,),
        in_specs=in_specs,
        out_specs=pl.BlockSpec((tm, d), lambda i: (i, 0)),
        out_shape=jax.ShapeDtypeStruct((n, d), F32),
        compiler_params=pltpu.CompilerParams(
            dimension_semantics=("parallel",), vmem_limit_bytes=VMEM_LIMIT),
    )(*args)


_HGRN_LEVELS = (64, 32, 16)
_HGRN_DIAG = 8


def _hgrn_constants():
    c = HGRN_CHUNK
    idx = np.arange(c)
    low = (idx[None, :] <= idx[:, None]).astype(np.float64)

    def ref_rows(r):
        return (idx[None, :] <= r[:, None]).astype(np.float64)

    blocks = [low, ref_rows(np.full(c, c - 1)) - low]
    masks = []
    for b in _HGRN_LEVELS:
        start = (idx // b) * b
        upper = (idx - start) >= b // 2
        ref = start + b // 2 - 1
        blocks.append(low - ref_rows(np.where(upper, ref, idx)))
        blocks.append(ref_rows(np.where(upper, idx, ref)) - low)
        same = (idx[:, None] // b) == (idx[None, :] // b)
        masks.append(same & upper[:, None] & ~upper[None, :])
    ref = (idx // _HGRN_DIAG) * _HGRN_DIAG + _HGRN_DIAG // 2 - 1
    blocks.append(low - ref_rows(ref))
    blocks.append(ref_rows(ref) - low)
    same = (idx[:, None] // _HGRN_DIAG) == (idx[None, :] // _HGRN_DIAG)
    masks.append(same & (idx[None, :] <= idx[:, None]))
    dst = np.concatenate(blocks, axis=0)
    return dst.astype(np.float32), np.stack(masks).astype(np.float32)


def _hgrn_kernel(q_ref, f_ref, i_ref, g_ref, lb_ref, gn_ref, dst_ref, mask_ref, bd_ref, grp_ref,
                 o_ref, st_ref):
    c = HGRN_CHUNK
    n_batch, n_pairs = st_ref.shape[:2]
    n_lvl = mask_ref.shape[0]

    @pl.when(pl.program_id(0) == 0)
    def _():
        st_ref[...] = jnp.zeros_like(st_ref)

    lb = lb_ref[...]
    gn = gn_ref[...]
    dst = dst_ref[...]
    bd = bd_ref[...]
    grp = grp_ref[...]
    low = lax.broadcasted_iota(jnp.int32, (c, LANES), 1) < HEAD_DIM

    def stack(x):
        return jnp.concatenate([jnp.where(low, x, jnp.zeros_like(x)), jnp.where(low, jnp.zeros_like(x), x)],
                               axis=0)

    for ch, b in [(ch, b) for ch in range(q_ref.shape[1] // c) for b in range(n_batch)]:
        rows = pl.ds(ch * c, c)
        q = q_ref[b, rows, :].astype(F32)
        qf = q * _sigmoid(q)
        f = lb + (1.0 - lb) * _sigmoid(f_ref[b, rows, :].astype(F32))
        kk = 1.0 - f
        ex = jnp.exp(_dot3_stacked(dst, jnp.log(f)))
        v = i_ref[b, rows, :].astype(BF16)
        g = g_ref[b, rows, :].astype(F32)
        gate = g * _sigmoid(g)
        q_in = (qf * ex[0:c]).astype(BF16)
        k_st = (kk * ex[c:2 * c]).astype(BF16)
        dec = ex[c - 1:c]
        q_l = [(qf * ex[(2 + 2 * l) * c:(3 + 2 * l) * c]).astype(BF16) for l in range(n_lvl)]
        k_l = [(kk * ex[(3 + 2 * l) * c:(4 + 2 * l) * c]).astype(BF16) for l in range(n_lvl)]
        outs = []
        for p in range(n_pairs):
            ps = slice(p * LANES, (p + 1) * LANES)
            scores = mask_ref[0] * _dot_nt(stack(q_l[0][:, ps]), k_l[0][:, ps])
            for l in range(1, n_lvl):
                scores = scores + mask_ref[l] * _dot_nt(stack(q_l[l][:, ps]), k_l[l][:, ps])
            pv = jnp.dot(scores.astype(BF16), v[:, ps], preferred_element_type=F32)
            st = st_ref[b, p]
            o = jnp.where(low, pv[:c], pv[c:]) + _dot_nt(q_in[:, ps], st.astype(BF16))
            st_ref[b, p] = st * dec[:, ps] + bd * _dot_tn(v[:, ps], k_st[:, ps])
            outs.append(o * lax.rsqrt(_dot2_rhs(o * o, grp) + EPS) * gn)
        o_ref[b, rows, :] = (jnp.concatenate(outs, axis=-1) * gate).astype(o_ref.dtype)


def _hgrn2(proj, lb, out_norm, batch, seq):
    n = proj.shape[0]
    width = lb.shape[0]
    n_heads = width // HEAD_DIM
    rb = HGRN_ROWS
    spb = seq // rb
    dst, masks = _hgrn_constants()
    col = lambda j: pl.BlockSpec((batch, rb, width), lambda s, j=j: (0, s, j))
    full = lambda a: pl.BlockSpec(a.shape, lambda s: (0,) * a.ndim)
    lb2 = lb.reshape(1, width)
    gn = jnp.tile(out_norm, LANES // HEAD_DIM).reshape(1, LANES)
    dst = jnp.asarray(np.concatenate([dst, dst, dst], axis=1), BF16)
    masks = jnp.asarray(np.concatenate([masks, masks], axis=1), F32)
    lane = np.arange(LANES)
    bd = jnp.asarray((lane[:, None] // HEAD_DIM) == (lane[None, :] // HEAD_DIM), F32)
    grp = _group_mean_matrix()
    proj3 = proj.reshape(batch, seq, proj.shape[1])
    out = pl.pallas_call(
        _hgrn_kernel,
        grid=(spb,),
        in_specs=[col(0), col(1), col(2), col(3), full(lb2), full(gn), full(dst), full(masks),
                  full(bd), full(grp)],
        out_specs=pl.BlockSpec((batch, rb, width), lambda s: (0, s, 0)),
        out_shape=jax.ShapeDtypeStruct((batch, seq, width), BF16),
        scratch_shapes=[pltpu.VMEM((batch, n_heads // 2, LANES, LANES), F32)],
        compiler_params=pltpu.CompilerParams(
            dimension_semantics=("arbitrary",), vmem_limit_bytes=VMEM_LIMIT),
    )(proj3, proj3, proj3, proj3, lb2, gn, dst, masks, bd, grp)
    return out.reshape(n, width)


def _fox_prep_kernel(q_ref, k_ref, v_ref, gate_ref, bias_ref, gq_ref, gk_ref, tril_ref,
                     sq_ref, sk_ref, cq_ref, ck_ref, cv_ref, grp_ref,
                     qa_ref, ka_ref, va_ref, carry_ref):
    n_heads = qa_ref.shape[1]
    tm = q_ref.shape[0]

    @pl.when(pl.program_id(1) == 0)
    def _():
        carry_ref[...] = jnp.zeros_like(carry_ref)

    z = gate_ref[...] + bias_ref[...]
    ls = -(jnp.maximum(-z, 0.0) + jnp.log(1.0 + jnp.exp(-jnp.abs(z))))
    cum = _dot3(tril_ref[...], ls) + carry_ref[...]
    carry_ref[...] = cum[tm - 1:tm]
    cum = cum * LOG2E

    c3 = jnp.concatenate(_split3(cum), axis=-1)
    ext_q = jnp.dot(c3, sq_ref[...], preferred_element_type=F32) + cq_ref[...]
    ext_k = jnp.dot(c3, sk_ref[...], preferred_element_type=F32) + ck_ref[...]

    lane = lax.broadcasted_iota(jnp.int32, (tm, LANES), 1)
    low_half = lane < HEAD_DIM
    grp = grp_ref[...]
    scale = HEAD_DIM ** -0.5 * LOG2E
    for c in range(n_heads // 2):
        cols = slice(c * LANES, (c + 1) * LANES)
        q = q_ref[:, cols].astype(F32)
        k = k_ref[:, cols].astype(F32)
        v = v_ref[:, cols]
        qn = q * lax.rsqrt(_dot2_rhs(q * q, grp) + EPS) * gq_ref[...] * scale
        kn = k * lax.rsqrt(_dot2_rhs(k * k, grp) + EPS) * gk_ref[...]
        for par in range(2):
            h = 2 * c + par
            data = low_half if par == 0 else jnp.logical_not(low_half)
            ext = slice(h * LANES, (h + 1) * LANES)
            qa_ref[0, h] = jnp.where(data, qn, ext_q[:, ext]).astype(BF16)
            ka_ref[0, h] = jnp.where(data, kn, ext_k[:, ext]).astype(BF16)
            va_ref[0, h] = jnp.where(data, v, cv_ref[par:par + 1, :].astype(BF16))


def _fox_layout_constants(n_heads):
    sq = np.zeros((3 * LANES, n_heads * LANES), np.float32)
    sk = np.zeros((3 * LANES, n_heads * LANES), np.float32)
    cq = np.zeros((1, n_heads * LANES), np.float32)
    ck = np.zeros((1, n_heads * LANES), np.float32)
    cv = np.zeros((2, LANES), np.float32)
    for h in range(n_heads):
        x0 = h * LANES + (HEAD_DIM if h % 2 == 0 else 0)
        for t in range(3):
            sq[t * LANES + h, x0 + t] = 1.0
            sk[t * LANES + h, x0 + 3 + t] = -1.0
        cq[0, x0 + 3:x0 + 6] = 1.0
        ck[0, x0:x0 + 3] = 1.0
    cv[0, HEAD_DIM] = 1.0
    cv[1, 0] = 1.0
    return (jnp.asarray(sq, BF16), jnp.asarray(sk, BF16), jnp.asarray(cq), jnp.asarray(ck), jnp.asarray(cv))


def _group_mean_matrix():
    lane = np.arange(LANES)
    return jnp.asarray(((lane[:, None] // HEAD_DIM) == (lane[None, :] // HEAD_DIM)) / HEAD_DIM, BF16)


def _fox_prep(proj, gates, f_bias, q_norm, k_norm, batch, seq, col0):
    width = 512
    n_heads = width // HEAD_DIM
    tm = PREP_ROWS
    spb = seq // tm
    col = lambda j: pl.BlockSpec((tm, width), lambda b, s, j=j: (b * spb + s, col0 + j))
    full = lambda a: pl.BlockSpec(a.shape, lambda b, s: (0,) * a.ndim)
    bias = jnp.zeros((1, LANES), F32).at[0, :n_heads].set(f_bias)
    gq = jnp.tile(q_norm, LANES // HEAD_DIM).reshape(1, LANES)
    gk = jnp.tile(k_norm, LANES // HEAD_DIM).reshape(1, LANES)
    tril = jnp.asarray(np.tril(np.ones((tm, tm), np.float32)), BF16)
    consts = _fox_layout_constants(n_heads) + (_group_mean_matrix(),)
    out = jax.ShapeDtypeStruct((batch, n_heads, seq, LANES), BF16)
    ospec = pl.BlockSpec((1, n_heads, tm, LANES), lambda b, s: (b, 0, s, 0))
    return pl.pallas_call(
        _fox_prep_kernel,
        grid=(batch, spb),
        in_specs=[col(0), col(1), col(2),
                  pl.BlockSpec((tm, LANES), lambda b, s: (b * spb + s, 0)),
                  full(bias), full(gq), full(gk), full(tril)] + [full(a) for a in consts],
        out_specs=[ospec, ospec, ospec],
        out_shape=[out, out, out],
        scratch_shapes=[pltpu.VMEM((1, LANES), F32)],
        compiler_params=pltpu.CompilerParams(
            dimension_semantics=("parallel", "arbitrary"), vmem_limit_bytes=VMEM_LIMIT),
    )(proj, proj, proj, gates, bias, gq, gk, tril, *consts)


def _tri_tables(nq):
    qi = [q for q in range(nq) for _ in range(q + 1)]
    ki = [k for q in range(nq) for k in range(q + 1)]
    return jnp.asarray(qi, jnp.int32), jnp.asarray(ki, jnp.int32)


def _fox_attn_kernel(qt_ref, kt_ref, q_ref, k_ref, v_ref, o_ref, m_ref, acc_ref):
    p_idx = pl.program_id(2)
    qi = qt_ref[p_idx]
    ki = kt_ref[p_idx]
    hp = q_ref.shape[1]
    t = q_ref.shape[2]

    @pl.when(ki == 0)
    def _():
        m_ref[...] = jnp.full_like(m_ref, -jnp.inf)
        acc_ref[...] = jnp.zeros_like(acc_ref)

    def step(masked):
        for h in range(hp):
            s = _dot_nt(q_ref[0, h], k_ref[0, h])
            if masked:
                row = lax.broadcasted_iota(jnp.int32, (t, t), 0)
                colm = lax.broadcasted_iota(jnp.int32, (t, t), 1)
                s = jnp.where(colm <= row, s, -jnp.inf)
            m_old = m_ref[h]
            m_new = jnp.maximum(m_old, jnp.max(s, axis=-1, keepdims=True))
            p = jnp.exp2(s - m_new)
            acc_ref[h] = (jnp.exp2(m_old - m_new) * acc_ref[h]
                          + jnp.dot(p.astype(BF16), v_ref[0, h], preferred_element_type=F32))
            m_ref[h] = m_new

    @pl.when(ki < qi)
    def _():
        step(False)

    @pl.when(ki == qi)
    def _():
        step(True)
        _fox_finalize(acc_ref, o_ref)


def _fox_finalize(acc_ref, o_ref):
    a0 = acc_ref[0]
    a1 = acc_ref[1]
    lane = lax.broadcasted_iota(jnp.int32, a0.shape, 1)
    o_ref[0] = jnp.where(lane < HEAD_DIM, a0 / a0[:, HEAD_DIM:HEAD_DIM + 1], a1 / a1[:, 0:1]).astype(o_ref.dtype)


def _tile_plan(n_sub, diagonal):
    plan = []
    for qb in range(n_sub):
        if not diagonal:
            plan.append((qb, 0, n_sub, False))
        else:
            if qb > 0:
                plan.append((qb, 0, qb, False))
            plan.append((qb, qb, qb + 1, True))
    return plan


def _fox_fast_kernel(qt_ref, kt_ref, q_ref, k_ref, v_ref, o_ref, acc_ref):
    p_idx = pl.program_id(2)
    qi = qt_ref[p_idx]
    ki = kt_ref[p_idx]
    hp = q_ref.shape[1]
    sb = ATTN_FAST_SUB
    n_sub = q_ref.shape[2] // sb

    @pl.when(ki == 0)
    def _():
        acc_ref[...] = jnp.zeros_like(acc_ref)

    def tile(diagonal):
        for h in range(hp):
            for qb, k0, k1, masked in _tile_plan(n_sub, diagonal):
                rows = pl.ds(qb * sb, sb)
                cols = pl.ds(k0 * sb, (k1 - k0) * sb)
                s = _dot_nt(q_ref[0, h, rows, :], k_ref[0, h, cols, :])
                if masked:
                    row = lax.broadcasted_iota(jnp.int32, (sb, sb), 0)
                    colm = lax.broadcasted_iota(jnp.int32, (sb, sb), 1)
                    s = jnp.where(colm <= row, s, -jnp.inf)
                p = jnp.exp2(s).astype(BF16)
                acc_ref[h, rows, :] += jnp.dot(p, v_ref[0, h, cols, :], preferred_element_type=F32)

    @pl.when(ki < qi)
    def _():
        tile(False)

    @pl.when(ki == qi)
    def _():
        tile(True)
        _fox_finalize(acc_ref, o_ref)


def _fox_attention(qa, ka, va, fast):
    batch, n_heads, seq, _ = qa.shape
    t = ATTN_FAST_TILE if fast else ATTN_TILE
    hp = 2
    nq = seq // t
    qt, kt = _tri_tables(nq)
    qspec = pl.BlockSpec((1, hp, t, LANES), lambda b, g, p, qt, kt: (b, g, qt[p], 0))
    kspec = pl.BlockSpec((1, hp, t, LANES), lambda b, g, p, qt, kt: (b, g, kt[p], 0))
    scratch = [pltpu.VMEM((hp, t, LANES), F32)]
    if not fast:
        scratch = [pltpu.VMEM((hp, t, 1), F32)] + scratch
    return pl.pallas_call(
        _fox_fast_kernel if fast else _fox_attn_kernel,
        grid_spec=pltpu.PrefetchScalarGridSpec(
            num_scalar_prefetch=2,
            grid=(batch, n_heads // hp, int(qt.shape[0])),
            in_specs=[qspec, kspec, kspec],
            out_specs=pl.BlockSpec((1, t, hp * HEAD_DIM), lambda b, g, p, qt, kt: (b, qt[p], g)),
            scratch_shapes=scratch),
        out_shape=jax.ShapeDtypeStruct((batch, seq, n_heads * HEAD_DIM), BF16),
        compiler_params=pltpu.CompilerParams(
            dimension_semantics=("parallel", "parallel", "arbitrary"), vmem_limit_bytes=VMEM_LIMIT),
    )(qt, kt, qa, ka, va)


def _logit_bound(q_gain, k_gain):
    return HEAD_DIM ** 0.5 * jnp.max(jnp.abs(q_gain)) * jnp.max(jnp.abs(k_gain))


def _diff_prep_kernel(q_ref, k_ref, v_ref, pos_ref, invf_ref, gq_ref, gk_ref, grp_ref, sel_ref,
                      qm_ref, k2_ref, va_ref, cs_ref, sn_ref):
    n_heads = k2_ref.shape[1]
    tm = q_ref.shape[0]
    ang = pos_ref[...].astype(F32) * invf_ref[...]
    cs_c = jnp.cos(ang)
    sn_c = jnp.sin(ang)
    per_row = LANES // (HEAD_DIM // 2)
    for j in range(per_row):
        cs_ref[pl.ds(j, tm // per_row, stride=per_row), :] = _dot3_rhs(cs_c, sel_ref[j])
        sn_ref[pl.ds(j, tm // per_row, stride=per_row), :] = _dot3_rhs(sn_c, sel_ref[j])
    lane = lax.broadcasted_iota(jnp.int32, (tm, LANES), 1)
    first = (lane % HEAD_DIM) < (HEAD_DIM // 2)
    cs = cs_ref[...]
    sn = sn_ref[...]
    sn = jnp.where(first, -sn, sn)
    grp = grp_ref[...]
    scale = HEAD_DIM ** -0.5 * LOG2E
    zero = jnp.zeros((tm, LANES), F32)
    onecol = jnp.where(lane == 0, 1.0, 0.0).astype(BF16)

    def norm_rope(x, gain):
        ms = _dot2_rhs(x * x, grp)
        y = x * lax.rsqrt(ms + EPS) * gain
        yr = jnp.where(first, pltpu.roll(y, LANES - HEAD_DIM // 2, 1), pltpu.roll(y, HEAD_DIM // 2, 1))
        return y * cs + yr * sn

    for h in range(n_heads):
        cols = slice(h * LANES, (h + 1) * LANES)
        qr = norm_rope(q_ref[:, cols].astype(F32), gq_ref[...]) * scale
        kr = norm_rope(k_ref[:, cols].astype(F32), gk_ref[...])
        qm_ref[0, h, 0] = jnp.where(lane < HEAD_DIM, qr, zero).astype(BF16)
        qm_ref[0, h, 1] = jnp.where(lane < HEAD_DIM, zero, qr).astype(BF16)
        k2_ref[0, h] = kr.astype(BF16)
        va_ref[0, h] = jnp.concatenate([v_ref[:, cols].astype(BF16), onecol], axis=-1)


def _diff_prep(proj, positions, q_norm, k_norm, batch, seq):
    n = proj.shape[0]
    width = proj.shape[1] // 3
    n_heads = width // LANES
    tm = PREP_ROWS
    spb = seq // tm
    col = lambda j: pl.BlockSpec((tm, width), lambda b, s, j=j: (b * spb + s, j))
    full = lambda a: pl.BlockSpec(a.shape, lambda b, s: (0,) * a.ndim)
    half = HEAD_DIM // 2
    inv_freq = ROPE_THETA ** (-jnp.arange(half, dtype=F32) / half)
    invf = jnp.tile(inv_freq, LANES // half).reshape(1, LANES)
    gq = jnp.tile(q_norm, LANES // HEAD_DIM).reshape(1, LANES)
    gk = jnp.tile(k_norm, LANES // HEAD_DIM).reshape(1, LANES)
    grp = _group_mean_matrix()
    per_row = LANES // half
    pos = jnp.repeat(positions.reshape(n // per_row, per_row).astype(jnp.int32), half, axis=1)
    lane = np.arange(LANES)
    sel = jnp.asarray(np.stack([(lane[:, None] == j * half + lane[None, :] % half) for j in range(per_row)]),
                      BF16)
    return pl.pallas_call(
        _diff_prep_kernel,
        grid=(batch, spb),
        in_specs=[col(0), col(1), col(2),
                  pl.BlockSpec((tm // per_row, LANES), lambda b, s: (b * spb + s, 0)),
                  full(invf), full(gq), full(gk), full(grp), full(sel)],
        out_specs=[pl.BlockSpec((1, n_heads, 2, tm, LANES), lambda b, s: (b, 0, 0, s, 0)),
                   pl.BlockSpec((1, n_heads, tm, LANES), lambda b, s: (b, 0, s, 0)),
                   pl.BlockSpec((1, n_heads, tm, 2 * LANES), lambda b, s: (b, 0, s, 0))],
        out_shape=[jax.ShapeDtypeStruct((batch, n_heads, 2, seq, LANES), BF16),
                   jax.ShapeDtypeStruct((batch, n_heads, seq, LANES), BF16),
                   jax.ShapeDtypeStruct((batch, n_heads, seq, 2 * LANES), BF16)],
        scratch_shapes=[pltpu.VMEM((tm, LANES), F32), pltpu.VMEM((tm, LANES), F32)],
        compiler_params=pltpu.CompilerParams(
            dimension_semantics=("parallel", "parallel"), vmem_limit_bytes=VMEM_LIMIT),
    )(proj, proj, proj, pos, invf, gq, gk, grp, sel)


def _diff_attn_kernel(lambda_init, qt_ref, kt_ref, q_ref, k_ref, v_ref, lam_ref, sub_ref,
                      o_ref, m_ref, acc_ref):
    p_idx = pl.program_id(2)
    qi = qt_ref[p_idx]
    ki = kt_ref[p_idx]
    t = k_ref.shape[2]
    dv = o_ref.shape[2]

    @pl.when(ki == 0)
    def _():
        m_ref[...] = jnp.full_like(m_ref, -jnp.inf)
        acc_ref[...] = jnp.zeros_like(acc_ref)

    def step(masked):
        for m in range(2):
            s = _dot_nt(q_ref[0, 0, m], k_ref[0, 0])
            if masked:
                row = lax.broadcasted_iota(jnp.int32, (t, t), 0) // CHUNK
                colm = lax.broadcasted_iota(jnp.int32, (t, t), 1) // CHUNK
                s = jnp.where(colm <= row, s, -jnp.inf)
            m_old = m_ref[m]
            m_new = jnp.maximum(m_old, jnp.max(s, axis=-1, keepdims=True))
            p = jnp.exp2(s - m_new)
            acc_ref[m] = (jnp.exp2(m_old - m_new) * acc_ref[m]
                          + jnp.dot(p.astype(BF16), v_ref[0, 0], preferred_element_type=F32))
            m_ref[m] = m_new

    @pl.when(ki < qi)
    def _():
        step(False)

    @pl.when(ki == qi)
    def _():
        step(True)
        _diff_finalize(lambda_init, acc_ref, lam_ref, sub_ref, o_ref)


def _diff_finalize(lambda_init, acc_ref, lam_ref, sub_ref, o_ref):
    dv = o_ref.shape[2]
    lp = lam_ref[...]
    lam = (jnp.exp(jnp.sum(lp[0:1] * lp[1:2], axis=-1, keepdims=True))
           - jnp.exp(jnp.sum(lp[2:3] * lp[3:4], axis=-1, keepdims=True)) + lambda_init)
    a0 = acc_ref[0]
    a1 = acc_ref[1]
    o = a0[:, :dv] / a0[:, dv:dv + 1] - lam * (a1[:, :dv] / a1[:, dv:dv + 1])
    ms = jnp.mean(o * o, axis=-1, keepdims=True)
    o_ref[0] = ((o * lax.rsqrt(ms + EPS) * sub_ref[...]) * (1.0 - lambda_init)).astype(o_ref.dtype)


def _diff_fast_kernel(lambda_init, qt_ref, kt_ref, q_ref, k_ref, v_ref, lam_ref, sub_ref, o_ref, acc_ref):
    p_idx = pl.program_id(2)
    qi = qt_ref[p_idx]
    ki = kt_ref[p_idx]
    sb = ATTN_FAST_SUB
    n_sub = k_ref.shape[2] // sb

    @pl.when(ki == 0)
    def _():
        acc_ref[...] = jnp.zeros_like(acc_ref)

    def tile(diagonal):
        for m in range(2):
            for qb, k0, k1, masked in _tile_plan(n_sub, diagonal):
                rows = pl.ds(qb * sb, sb)
                cols = pl.ds(k0 * sb, (k1 - k0) * sb)
                s = _dot_nt(q_ref[0, 0, m, rows, :], k_ref[0, 0, cols, :])
                if masked:
                    row = lax.broadcasted_iota(jnp.int32, (sb, sb), 0) // CHUNK
                    colm = lax.broadcasted_iota(jnp.int32, (sb, sb), 1) // CHUNK
                    s = jnp.where(colm <= row, s, -jnp.inf)
                p = jnp.exp2(s).astype(BF16)
                acc_ref[m, rows, :] += jnp.dot(p, v_ref[0, 0, cols, :], preferred_element_type=F32)

    @pl.when(ki < qi)
    def _():
        tile(False)

    @pl.when(ki == qi)
    def _():
        tile(True)
        _diff_finalize(lambda_init, acc_ref, lam_ref, sub_ref, o_ref)


def _diff_attention(qm, k2, va, lam_params, subln, lambda_init, fast):
    batch, n_heads, seq, _ = k2.shape
    dv = va.shape[3] // 2
    t = ATTN_FAST_TILE if fast else ATTN_TILE
    nq = seq // t
    qt, kt = _tri_tables(nq)
    lamp = jnp.zeros((8, LANES), F32).at[:4, :HEAD_DIM].set(lam_params)
    sub = subln.reshape(1, dv)
    scratch = [pltpu.VMEM((2, t, 2 * dv), F32)]
    if not fast:
        scratch = [pltpu.VMEM((2, t, 1), F32)] + scratch
    return pl.pallas_call(
        functools.partial(_diff_fast_kernel if fast else _diff_attn_kernel, lambda_init),
        grid_spec=pltpu.PrefetchScalarGridSpec(
            num_scalar_prefetch=2,
            grid=(batch, n_heads, int(qt.shape[0])),
            in_specs=[pl.BlockSpec((1, 1, 2, t, LANES), lambda b, h, p, qt, kt: (b, h, 0, qt[p], 0)),
                      pl.BlockSpec((1, 1, t, LANES), lambda b, h, p, qt, kt: (b, h, kt[p], 0)),
                      pl.BlockSpec((1, 1, t, 2 * dv), lambda b, h, p, qt, kt: (b, h, kt[p], 0)),
                      pl.BlockSpec((8, LANES), lambda b, h, p, qt, kt: (0, 0)),
                      pl.BlockSpec((1, dv), lambda b, h, p, qt, kt: (0, 0))],
            out_specs=pl.BlockSpec((1, t, dv), lambda b, h, p, qt, kt: (b, qt[p], h)),
            scratch_shapes=scratch),
        out_shape=jax.ShapeDtypeStruct((batch, seq, n_heads * dv), BF16),
        compiler_params=pltpu.CompilerParams(
            dimension_semantics=("parallel", "parallel", "arbitrary"), vmem_limit_bytes=VMEM_LIMIT),
    )(qt, kt, qm, k2, va, lamp, sub)


def _store_token_tiles(ref, value):
    t, width = value.shape
    s = width // LANES
    for j in range(s):
        ref[pl.ds(j, t, stride=s), :] = value[:, j * LANES:(j + 1) * LANES]


def _load_token_tiles(ref, first_row, t, s):
    return jnp.concatenate([ref[pl.ds(first_row + j, t, stride=s), :] for j in range(s)], axis=-1)


def _router_kernel(h_ref, g_ref, whi_ref, wlo_ref, xn_ref, route_ref):
    x = h_ref[...]
    tm = x.shape[0]
    xn = x * lax.rsqrt(jnp.mean(x * x, axis=-1, keepdims=True) + EPS) * g_ref[...]
    _store_token_tiles(xn_ref, xn)
    xh = xn.astype(BF16)
    xl = (xn - xh.astype(F32)).astype(BF16)
    d = lambda a, b: jnp.dot(a, b[...], preferred_element_type=F32)
    logits = d(xh, whi_ref) + (d(xl, whi_ref) + d(xh, wlo_ref))
    lane = lax.broadcasted_iota(jnp.int32, (tm, LANES), 1)
    neg = jnp.full((tm, LANES), -jnp.inf, F32)
    big = jnp.full((tm, LANES), LANES, jnp.int32)

    def top1(vals):
        m = jnp.max(vals, axis=-1, keepdims=True)
        idx = jnp.min(jnp.where(vals == m, lane, big), axis=-1, keepdims=True)
        return m, idx

    grp_logits = jnp.where(lane < N_GROUPS, logits, neg)
    mg, gidx = top1(grp_logits)
    p_g = 1.0 / jnp.sum(jnp.exp(grp_logits - mg), axis=-1, keepdims=True)
    e_lane = lane - N_GROUPS
    in_grp = (e_lane >= gidx * EXPERTS_PER_GROUP) & (e_lane < (gidx + 1) * EXPERTS_PER_GROUP)
    sel = jnp.where(in_grp, logits, neg)
    m1, i1 = top1(sel)
    m2, i2 = top1(jnp.where(lane == i1, neg, sel))
    r = jnp.exp(m2 - m1)
    w1 = p_g / (1.0 + r)
    w2 = p_g * r / (1.0 + r)
    zero = jnp.zeros((tm, LANES), F32)
    route_ref[...] = jnp.where(lane == 0, (i1 - N_GROUPS).astype(F32),
                     jnp.where(lane == 1, (i2 - N_GROUPS).astype(F32),
                     jnp.where(lane == 2, w1, jnp.where(lane == 3, w2, zero))))


def _router(h, gain, w_group, w_expert):
    n, d = h.shape
    tm = ROUTER_ROWS
    wr = jnp.zeros((d, LANES), F32)
    wr = wr.at[:, :N_GROUPS].set(w_group)
    wr = wr.at[:, N_GROUPS:N_GROUPS + N_EXPERTS].set(
        jnp.transpose(w_expert, (1, 0, 2)).reshape(d, N_EXPERTS))
    w_hi = wr.astype(BF16)
    w_lo = (wr - w_hi.astype(F32)).astype(BF16)
    return pl.pallas_call(
        _router_kernel,
        grid=(n // tm,),
        in_specs=[pl.BlockSpec((tm, d), lambda i: (i, 0)),
                  pl.BlockSpec((1, d), lambda i: (0, 0)),
                  pl.BlockSpec((d, LANES), lambda i: (0, 0)),
                  pl.BlockSpec((d, LANES), lambda i: (0, 0))],
        out_specs=[pl.BlockSpec((tm * d // LANES, LANES), lambda i: (i, 0)),
                   pl.BlockSpec((tm, LANES), lambda i: (i, 0))],
        out_shape=[jax.ShapeDtypeStruct((n * d // LANES, LANES), F32),
                   jax.ShapeDtypeStruct((n, LANES), F32)],
        compiler_params=pltpu.CompilerParams(
            dimension_semantics=("parallel",), vmem_limit_bytes=VMEM_LIMIT),
    )(h, gain.reshape(1, d), w_hi, w_lo)


def _dispatch_tables(expert_ids, tm):
    n = expert_ids.shape[0]
    pairs = expert_ids.reshape(-1)
    n_pairs = pairs.shape[0]
    n_tiles = n_pairs // tm + N_EXPERTS
    onehot = (pairs[:, None] == jnp.arange(N_EXPERTS, dtype=jnp.int32)[None, :]).astype(jnp.int32)
    csum = jnp.cumsum(onehot, axis=0)
    rank = jnp.sum((csum - onehot) * onehot, axis=1)
    counts = csum[-1]
    padded = ((counts + tm - 1) // tm) * tm
    ends = jnp.cumsum(padded)
    starts = ends - padded
    pos = jnp.sum(onehot * starts[None, :], axis=1) + rank
    tile_start = jnp.arange(n_tiles, dtype=jnp.int32) * tm
    tile_expert = jnp.minimum(jnp.sum(tile_start[:, None] >= ends[None, :], axis=1),
                              N_EXPERTS - 1).astype(jnp.int32)
    n_valid = (ends[-1] // tm).astype(jnp.int32).reshape(1)
    pad_start = (starts + counts).astype(jnp.int32)
    pad_count = (padded - counts).astype(jnp.int32)
    return pos.reshape(n, 2).astype(jnp.int32), tile_expert, n_valid, n_tiles, pad_start, pad_count


def _token_copy(src_hbm, tok, dst_ref, r, sem):
    src = src_hbm.at[pl.ds(pl.multiple_of(tok * TILE_ROWS, TILE_ROWS), TILE_ROWS)]
    first = r * TILE_ROWS if isinstance(r, int) else pl.multiple_of(r * TILE_ROWS, TILE_ROWS)
    return pltpu.make_async_copy(src, dst_ref.at[pl.ds(first, TILE_ROWS)], sem)


def _gather_tokens(src_hbm, idx_ref, dst_ref, sem, n_tokens):
    def body(r, carry):
        _token_copy(src_hbm, idx_ref[0, 0, r], dst_ref, r, sem).start()
        return carry
    lax.fori_loop(0, n_tokens, body, 0, unroll=8)


def _wait_tokens(src_hbm, dst_ref, sem):
    pltpu.make_async_copy(src_hbm.at[pl.ds(0, dst_ref.shape[0])], dst_ref, sem).wait()


def _dispatch_kernel(ps_ref, pc_ref, nv_ref, pos_ref, x_ref, xs_hbm, zero_blk, sem, pad_sem):
    i = pl.program_id(0)
    tokens = pos_ref.shape[2] // 2

    def slot_tile(slot):
        return xs_hbm.at[pl.ds(pl.multiple_of(slot * TILE_ROWS, TILE_ROWS), TILE_ROWS)]

    for r in range(2 * tokens):
        src = x_ref.at[pl.ds((r % tokens) * TILE_ROWS, TILE_ROWS)]
        pltpu.make_async_copy(src, slot_tile(pos_ref[0, 0, r]), sem).start(priority=r % 2)
    rows = 2 * tokens * TILE_ROWS
    pltpu.make_async_copy(xs_hbm.at[pl.ds(0, rows)], xs_hbm.at[pl.ds(0, rows)], sem).wait()

    @pl.when(i == pl.num_programs(0) - 1)
    def _():
        zero_blk[...] = jnp.zeros_like(zero_blk)
        zero_tile = zero_blk.at[pl.ds(0, TILE_ROWS)]
        for e in range(ps_ref.shape[0]):
            first = ps_ref[e]
            count = pc_ref[e]

            def fill(j, carry):
                pltpu.make_async_copy(zero_tile, slot_tile(first + j), pad_sem).start()
                return carry

            def drain(j, carry):
                pltpu.make_async_copy(zero_tile, slot_tile(first), pad_sem).wait()
                return carry

            lax.fori_loop(0, count, fill, 0)
            lax.fori_loop(0, count, drain, 0)

        block_rows = zero_blk.shape[0]
        n_blocks = xs_hbm.shape[0] // block_rows

        def block(t):
            return xs_hbm.at[pl.ds(pl.multiple_of(t * block_rows, block_rows), block_rows)]

        def fill_block(t, carry):
            pltpu.make_async_copy(zero_blk, block(t), pad_sem).start()
            return carry

        def drain_block(t, carry):
            pltpu.make_async_copy(zero_blk, block(t), pad_sem).wait()
            return carry

        lax.fori_loop(nv_ref[0], n_blocks, fill_block, 0)
        lax.fori_loop(nv_ref[0], n_blocks, drain_block, 0)


def _moe_dispatch(xn, pos, pad_start, pad_count, n_valid, n_tiles, tm):
    n = pos.shape[0]
    tb = COMBINE_ROWS
    steps = n // tb
    pos_tab = jnp.transpose(pos.reshape(steps, tb, 2), (0, 2, 1)).reshape(steps, 1, 2 * tb)
    return pl.pallas_call(
        _dispatch_kernel,
        grid_spec=pltpu.PrefetchScalarGridSpec(
            num_scalar_prefetch=3,
            grid=(steps,),
            in_specs=[pl.BlockSpec((1, 1, 2 * tb), lambda i, ps, pc, nv: (i, 0, 0), memory_space=pltpu.SMEM),
                      pl.BlockSpec((tb * TILE_ROWS, LANES), lambda i, ps, pc, nv: (i, 0))],
            out_specs=pl.BlockSpec(memory_space=pl.ANY),
            scratch_shapes=[pltpu.VMEM((tm * TILE_ROWS, LANES), F32),
                            pltpu.SemaphoreType.DMA, pltpu.SemaphoreType.DMA]),
        out_shape=jax.ShapeDtypeStruct((n_tiles * tm * TILE_ROWS, LANES), F32),
        compiler_params=pltpu.CompilerParams(
            dimension_semantics=("arbitrary",), vmem_limit_bytes=VMEM_LIMIT),
    )(pad_start, pad_count, n_valid, pos_tab, xn)


def _moe_kernel(te_ref, nv_ref, x_ref, wg_ref, wu_ref, wd_ref, o_ref, wg_b, wu_b, wd_b):
    i = pl.program_id(0)
    tm = x_ref.shape[0] // TILE_ROWS
    n_valid = nv_ref[0]
    new_expert = jnp.logical_or(i == 0, te_ref[i] != te_ref[jnp.maximum(i - 1, 0)])

    @pl.when(jnp.logical_and(i < n_valid, new_expert))
    def _():
        wg_b[...] = wg_ref[0].astype(BF16)
        wu_b[...] = wu_ref[0].astype(BF16)
        wd_b[...] = wd_ref[0].astype(BF16)

    @pl.when(i < n_valid)
    def _():
        x = _load_token_tiles(x_ref, 0, tm, TILE_ROWS).astype(BF16)
        g = jnp.dot(x, wg_b[...], preferred_element_type=F32)
        u = jnp.dot(x, wu_b[...], preferred_element_type=F32)
        hid = (g * _sigmoid(g) * u).astype(BF16)
        _store_token_tiles(o_ref, jnp.dot(hid, wd_b[...], preferred_element_type=F32))

    @pl.when(i >= n_valid)
    def _():
        o_ref[...] = jnp.zeros_like(o_ref)


def _moe_experts(xs, tile_expert, n_valid, tm, w_gate, w_up, w_down):
    n_exp, d, f = w_gate.shape
    assert d == TILE_ROWS * LANES
    n_tiles = xs.shape[0] // (tm * TILE_ROWS)
    x_index = lambda i, te, nv: (jnp.minimum(i, nv[0] - 1), 0)
    return pl.pallas_call(
        _moe_kernel,
        grid_spec=pltpu.PrefetchScalarGridSpec(
            num_scalar_prefetch=2,
            grid=(n_tiles,),
            in_specs=[
                pl.BlockSpec((tm * TILE_ROWS, LANES), x_index),
                pl.BlockSpec((1, d, f), lambda i, te, nv: (te[i], 0, 0)),
                pl.BlockSpec((1, d, f), lambda i, te, nv: (te[i], 0, 0)),
                pl.BlockSpec((1, f, d), lambda i, te, nv: (te[i], 0, 0))],
            out_specs=pl.BlockSpec((tm * TILE_ROWS, LANES), lambda i, te, nv: (i, 0)),
            scratch_shapes=[pltpu.VMEM((d, f), BF16), pltpu.VMEM((d, f), BF16), pltpu.VMEM((f, d), BF16)]),
        out_shape=jax.ShapeDtypeStruct((n_tiles * tm * TILE_ROWS, LANES), F32),
        compiler_params=pltpu.CompilerParams(
            dimension_semantics=("arbitrary",), vmem_limit_bytes=VMEM_LIMIT),
    )(tile_expert, n_valid, xs, w_gate, w_up, w_down)


def _combine_kernel(pos_ref, pos_next_ref, h_ref, route_ref, y_hbm, o_ref, ybuf, sems):
    i = pl.program_id(0)
    n_steps = pl.num_programs(0)
    tokens = ybuf.shape[1] // TILE_ROWS
    slot = i % 2

    @pl.when(i == 0)
    def _():
        _gather_tokens(y_hbm, pos_ref, ybuf.at[0], sems.at[0], tokens)

    @pl.when(i + 1 < n_steps)
    def _():
        for r in range(tokens):
            _token_copy(y_hbm, pos_next_ref[0, 0, r], ybuf.at[1 - slot], r,
                        sems.at[1 - slot]).start(priority=r % 2)

    _wait_tokens(y_hbm, ybuf.at[slot], sems.at[slot])
    tc = tokens // 2
    w = route_ref[...]
    first = _load_token_tiles(ybuf.at[slot], 0, tc, TILE_ROWS)
    second = _load_token_tiles(ybuf.at[slot], tc * TILE_ROWS, tc, TILE_ROWS)
    o_ref[...] = h_ref[...] + w[:, 2:3] * first + w[:, 3:4] * second


def _moe_combine(h, route, pos, y_sorted):
    n, d = h.shape
    tc = COMBINE_ROWS
    steps = n // tc
    pos_tab = jnp.transpose(pos.reshape(steps, tc, 2), (0, 2, 1)).reshape(steps, 1, 2 * tc)
    return pl.pallas_call(
        _combine_kernel,
        grid=(steps,),
        in_specs=[pl.BlockSpec((1, 1, 2 * tc), lambda i: (i, 0, 0), memory_space=pltpu.SMEM),
                  pl.BlockSpec((1, 1, 2 * tc), lambda i: (jnp.minimum(i + 1, steps - 1), 0, 0),
                               memory_space=pltpu.SMEM),
                  pl.BlockSpec((tc, d), lambda i: (i, 0)),
                  pl.BlockSpec((tc, LANES), lambda i: (i, 0)),
                  pl.BlockSpec(memory_space=pl.ANY)],
        out_specs=pl.BlockSpec((tc, d), lambda i: (i, 0)),
        out_shape=jax.ShapeDtypeStruct((n, d), F32),
        scratch_shapes=[pltpu.VMEM((2, 2 * tc * TILE_ROWS, LANES), F32), pltpu.SemaphoreType.DMA((2,))],
        compiler_params=pltpu.CompilerParams(
            dimension_semantics=("arbitrary",), vmem_limit_bytes=VMEM_LIMIT),
    )(pos_tab, pos_tab, h, route, y_sorted)


def _moe_layer(h, layer, gain, w_group, w_expert, w_gate, w_up, w_down):
    d = h.shape[1]
    f = w_gate.shape[-1]
    xn, route = _router(h, gain, w_group, w_expert)
    expert_ids = route[:, :2].astype(jnp.int32)
    pos, tile_expert, n_valid, n_tiles, pad_start, pad_count = _dispatch_tables(expert_ids, MOE_TM)
    xs = _moe_dispatch(xn, pos, pad_start, pad_count, n_valid, n_tiles, MOE_TM)
    y_sorted = _moe_experts(xs, tile_expert + layer * N_EXPERTS, n_valid, MOE_TM,
                            w_gate.reshape(-1, d, f), w_up.reshape(-1, d, f), w_down.reshape(-1, f, d))
    return _moe_combine(h, route, pos, y_sorted)


def _even_layer(h, batch, seq, gain, w_in, w_out, lb, f_bias, out_norm, q_norm, k_norm):
    d = h.shape[1]
    n_main = w_in.shape[1] - f_bias.shape[0]
    w_main = w_in[:, :n_main].astype(BF16)
    w_gate = jnp.zeros((d, LANES), F32).at[:, :f_bias.shape[0]].set(w_in[:, n_main:])
    proj, gates = _norm_proj(h, gain, w_main, w_gate)
    o_a = _hgrn2(proj, lb, out_norm, batch, seq)
    qa, ka, va = _fox_prep(proj, gates, f_bias, q_norm, k_norm, batch, seq, col0=4)
    o_b = lax.cond(_logit_bound(q_norm, k_norm) <= LOGIT_BOUND_MAX,
                   functools.partial(_fox_attention, fast=True),
                   functools.partial(_fox_attention, fast=False), qa, ka, va).reshape(batch * seq, -1)
    wo = w_out.astype(BF16)
    ka_dim = o_a.shape[1]
    return _proj_residual([(o_a, wo[:ka_dim]), (o_b, wo[ka_dim:])], h)


def _odd_layer(h, positions, batch, seq, gain, w_in, w_out, q_norm, k_norm, lam_params, subln, lambda_init):
    proj = _norm_proj(h, gain, w_in.astype(BF16))
    qm, k2, va = _diff_prep(proj, positions, q_norm, k_norm, batch, seq)
    attn = lambda fast: functools.partial(_diff_attention, lam_params=lam_params, subln=subln,
                                          lambda_init=lambda_init, fast=fast)
    o = lax.cond(_logit_bound(q_norm, k_norm) <= LOGIT_BOUND_MAX,
                 attn(True), attn(False), qm, k2, va).reshape(batch * seq, -1)
    return _proj_residual([(o, w_out.astype(BF16))], h)


def kernel(x, positions, hgrn_lb_logits, norm_mix, norm_ffn, even_w_in, even_w_out, fox_f_bias,
           hgrn_out_norm, fox_q_norm, fox_k_norm, odd_w_in, odd_w_out, diff_q_norm, diff_k_norm,
           diff_lambda_q1, diff_lambda_k1, diff_lambda_q2, diff_lambda_k2, diff_subln,
           moe_router_group, moe_router_expert, moe_w_gate, moe_w_up, moe_w_down):
    batch, seq, d = x.shape
    depth = norm_mix.shape[0]
    lower_bounds = jnp.cumsum(jax.nn.softmax(hgrn_lb_logits.astype(F32), axis=0), axis=0)
    h = x.reshape(batch * seq, d)
    for layer in range(depth):
        j = layer // 2
        if layer % 2 == 0:
            h = _even_layer(h, batch, seq, norm_mix[layer], even_w_in[j], even_w_out[j], lower_bounds[j],
                            fox_f_bias[j], hgrn_out_norm[j], fox_q_norm[j], fox_k_norm[j])
        else:
            lambda_init = 0.8 - 0.6 * math.exp(-0.3 * layer)
            lam_params = jnp.stack([diff_lambda_q1[j], diff_lambda_k1[j],
                                    diff_lambda_q2[j], diff_lambda_k2[j]]).astype(F32)
            h = _odd_layer(h, positions, batch, seq, norm_mix[layer], odd_w_in[j], odd_w_out[j],
                           diff_q_norm[j], diff_k_norm[j], lam_params, diff_subln[j], lambda_init)
        h = _moe_layer(h, layer, norm_ffn[layer], moe_router_group[layer], moe_router_expert[layer],
                       moe_w_gate, moe_w_up, moe_w_down)
    return h.reshape(batch, seq, d)
```

```python
import functools
import math

import numpy as np
import jax
import jax.numpy as jnp
from jax import lax
from jax.experimental import pallas as pl
from jax.experimental.pallas import tpu as pltpu

F32 = jnp.float32
BF16 = jnp.bfloat16

EPS = 1e-6
ROPE_THETA = 10000.0
CHUNK = 64
HEAD_DIM = 64
N_GROUPS = 4
EXPERTS_PER_GROUP = 8
N_EXPERTS = N_GROUPS * EXPERTS_PER_GROUP
LANES = 128
TILE_ROWS = 8

HGRN_CHUNK = 64
HGRN_ROWS = 256
ATTN_TILE = 512
ATTN_FAST_TILE = 2048
ATTN_FAST_SUB = 512
LOGIT_BOUND_MAX = 60.0
LOG2E = math.log2(math.e)
PREP_ROWS = 256
PROJ_TM = 512
ROUTER_ROWS = 256
MOE_TM = 512
COMBINE_ROWS = 256
VMEM_LIMIT = 56 * 1024 * 1024


def _split3(x):
    hi = x.astype(BF16)
    r1 = x - hi.astype(F32)
    mid = r1.astype(BF16)
    lo = (r1 - mid.astype(F32)).astype(BF16)
    return hi, mid, lo


def _dot3(const_bf16, x):
    hi, mid, lo = _split3(x)
    d = lambda b: jnp.dot(const_bf16, b, preferred_element_type=F32)
    return d(hi) + d(mid) + d(lo)


def _dot3_stacked(const3_bf16, x):
    return jnp.dot(const3_bf16, jnp.concatenate(_split3(x), axis=0), preferred_element_type=F32)


def _dot3_rhs(x, const_bf16):
    hi, mid, lo = _split3(x)
    d = lambda a: jnp.dot(a, const_bf16, preferred_element_type=F32)
    return d(hi) + d(mid) + d(lo)


def _dot2_rhs(x, const_bf16):
    hi = x.astype(BF16)
    lo = (x - hi.astype(F32)).astype(BF16)
    d = lambda a: jnp.dot(a, const_bf16, preferred_element_type=F32)
    return d(hi) + d(lo)


def _dot_nt(a, b):
    return lax.dot_general(a, b, (((1,), (1,)), ((), ())), preferred_element_type=F32)


def _dot_tn(a, b):
    return lax.dot_general(a, b, (((0,), (0,)), ((), ())), preferred_element_type=F32)


def _sigmoid(x):
    return 1.0 / (1.0 + jnp.exp(-x))


def _norm_proj_kernel(has_aux, x_ref, g_ref, w_ref, *rest):
    x = x_ref[...]
    ms = jnp.mean(x * x, axis=-1, keepdims=True)
    xn = x * lax.rsqrt(ms + EPS) * g_ref[...]
    xb = xn.astype(BF16)
    rest[-2 if has_aux else -1][...] = jnp.dot(xb, w_ref[...], preferred_element_type=F32).astype(BF16)
    if has_aux:
        whi_ref, wlo_ref, _, oaux_ref = rest
        xl = (xn - xb.astype(F32)).astype(BF16)
        d = lambda a, b: jnp.dot(a, b[...], preferred_element_type=F32)
        oaux_ref[...] = d(xb, whi_ref) + (d(xl, whi_ref) + d(xb, wlo_ref))


def _norm_proj(x, gain, w, w_aux=None):
    n, d = x.shape
    m = w.shape[1]
    tm = PROJ_TM
    has_aux = w_aux is not None
    in_specs = [pl.BlockSpec((tm, d), lambda i: (i, 0)),
                pl.BlockSpec((1, d), lambda i: (0, 0)),
                pl.BlockSpec((d, m), lambda i: (0, 0))]
    out_specs = [pl.BlockSpec((tm, m), lambda i: (i, 0))]
    out_shape = [jax.ShapeDtypeStruct((n, m), BF16)]
    args = [x, gain.reshape(1, d), w]
    if has_aux:
        w_hi = w_aux.astype(BF16)
        w_lo = (w_aux - w_hi.astype(F32)).astype(BF16)
        in_specs += [pl.BlockSpec((d, LANES), lambda i: (0, 0))] * 2
        out_specs.append(pl.BlockSpec((tm, LANES), lambda i: (i, 0)))
        out_shape.append(jax.ShapeDtypeStruct((n, LANES), F32))
        args += [w_hi, w_lo]
    res = pl.pallas_call(
        functools.partial(_norm_proj_kernel, has_aux),
        grid=(n // tm,),
        in_specs=in_specs, out_specs=out_specs, out_shape=out_shape,
        compiler_params=pltpu.CompilerParams(
            dimension_semantics=("parallel",), vmem_limit_bytes=VMEM_LIMIT),
    )(*args)
    return res if has_aux else res[0]


def _proj_res_kernel(n_in, *refs):
    h_ref = refs[2 * n_in]
    o_ref = refs[2 * n_in + 1]
    acc = h_ref[...]
    for t in range(n_in):
        acc = acc + jnp.dot(refs[2 * t][...], refs[2 * t + 1][...], preferred_element_type=F32)
    o_ref[...] = acc


def _proj_residual(pairs, h):
    n, d = h.shape
    tm = PROJ_TM
    in_specs, args = [], []
    for a, w in pairs:
        k = a.shape[1]
        in_specs += [pl.BlockSpec((tm, k), lambda i: (i, 0)),
                     pl.BlockSpec((k, d), lambda i: (0, 0))]
        args += [a, w]
    in_specs.append(pl.BlockSpec((tm, d), lambda i: (i, 0)))
    args.append(h)
    return pl.pallas_call(
        functools.partial(_proj_res_kernel, len(pairs)),
        grid=(n // tm,),
        in_specs=in_specs,
        out_specs=pl.BlockSpec((tm, d), lambda i: (i, 0)),
        out_shape=jax.ShapeDtypeStruct((n, d), F32),
        compiler_params=pltpu.CompilerParams(
            dimension_semantics=("parallel",), vmem_limit_bytes=VMEM_LIMIT),
    )(*args)


_HGRN_LEVELS = (64, 32, 16)
_HGRN_DIAG = 8


def _hgrn_constants():
    c = HGRN_CHUNK
    idx = np.arange(c)
    low = (idx[None, :] <= idx[:, None]).astype(np.float64)

    def ref_rows(r):
        return (idx[None, :] <= r[:, None]).astype(np.float64)

    blocks = [low, ref_rows(np.full(c, c - 1)) - low]
    masks = []
    for b in _HGRN_LEVELS:
        start = (idx // b) * b
        upper = (idx - start) >= b // 2
        ref = start + b // 2 - 1
        blocks.append(low - ref_rows(np.where(upper, ref, idx)))
        blocks.append(ref_rows(np.where(upper, idx, ref)) - low)
        same = (idx[:, None] // b) == (idx[None, :] // b)
        masks.append(same & upper[:, None] & ~upper[None, :])
    ref = (idx // _HGRN_DIAG) * _HGRN_DIAG + _HGRN_DIAG // 2 - 1
    blocks.append(low - ref_rows(ref))
    blocks.append(ref_rows(ref) - low)
    same = (idx[:, None] // _HGRN_DIAG) == (idx[None, :] // _HGRN_DIAG)
    masks.append(same & (idx[None, :] <= idx[:, None]))
    dst = np.concatenate(blocks, axis=0)
    return dst.astype(np.float32), np.stack(masks).astype(np.float32)


def _hgrn_kernel(q_ref, f_ref, i_ref, g_ref, lb_ref, gn_ref, dst_ref, mask_ref, bd_ref, grp_ref,
                 o_ref, st_ref):
    c = HGRN_CHUNK
    n_batch, n_pairs = st_ref.shape[:2]
    n_lvl = mask_ref.shape[0]

    @pl.when(pl.program_id(0) == 0)
    def _():
        st_ref[...] = jnp.zeros_like(st_ref)

    lb = lb_ref[...]
    gn = gn_ref[...]
    dst = dst_ref[...]
    bd = bd_ref[...]
    grp = grp_ref[...]
    low = lax.broadcasted_iota(jnp.int32, (c, LANES), 1) < HEAD_DIM

    def stack(x):
        return jnp.concatenate([jnp.where(low, x, jnp.zeros_like(x)), jnp.where(low, jnp.zeros_like(x), x)],
                               axis=0)

    for ch, b in [(ch, b) for ch in range(q_ref.shape[1] // c) for b in range(n_batch)]:
        rows = pl.ds(ch * c, c)
        q = q_ref[b, rows, :].astype(F32)
        qf = q * _sigmoid(q)
        f = lb + (1.0 - lb) * _sigmoid(f_ref[b, rows, :].astype(F32))
        kk = 1.0 - f
        ex = jnp.exp(_dot3_stacked(dst, jnp.log(f)))
        v = i_ref[b, rows, :].astype(BF16)
        g = g_ref[b, rows, :].astype(F32)
        gate = g * _sigmoid(g)
        q_in = (qf * ex[0:c]).astype(BF16)
        k_st = (kk * ex[c:2 * c]).astype(BF16)
        dec = ex[c - 1:c]
        q_l = [(qf * ex[(2 + 2 * l) * c:(3 + 2 * l) * c]).astype(BF16) for l in range(n_lvl)]
        k_l = [(kk * ex[(3 + 2 * l) * c:(4 + 2 * l) * c]).astype(BF16) for l in range(n_lvl)]
        outs = []
        for p in range(n_pairs):
            ps = slice(p * LANES, (p + 1) * LANES)
            scores = mask_ref[0] * _dot_nt(stack(q_l[0][:, ps]), k_l[0][:, ps])
            for l in range(1, n_lvl):
                scores = scores + mask_ref[l] * _dot_nt(stack(q_l[l][:, ps]), k_l[l][:, ps])
            pv = jnp.dot(scores.astype(BF16), v[:, ps], preferred_element_type=F32)
            st = st_ref[b, p]
            o = jnp.where(low, pv[:c], pv[c:]) + _dot_nt(q_in[:, ps], st.astype(BF16))
            st_ref[b, p] = st * dec[:, ps] + bd * _dot_tn(v[:, ps], k_st[:, ps])
            outs.append(o * lax.rsqrt(_dot2_rhs(o * o, grp) + EPS) * gn)
        o_ref[b, rows, :] = (jnp.concatenate(outs, axis=-1) * gate).astype(o_ref.dtype)


def _hgrn2(proj, lb, out_norm, batch, seq):
    n = proj.shape[0]
    width = lb.shape[0]
    n_heads = width // HEAD_DIM
    rb = HGRN_ROWS
    spb = seq // rb
    dst, masks = _hgrn_constants()
    col = lambda j: pl.BlockSpec((batch, rb, width), lambda s, j=j: (0, s, j))
    full = lambda a: pl.BlockSpec(a.shape, lambda s: (0,) * a.ndim)
    lb2 = lb.reshape(1, width)
    gn = jnp.tile(out_norm, LANES // HEAD_DIM).reshape(1, LANES)
    dst = jnp.asarray(np.concatenate([dst, dst, dst], axis=1), BF16)
    masks = jnp.asarray(np.concatenate([masks, masks], axis=1), F32)
    lane = np.arange(LANES)
    bd = jnp.asarray((lane[:, None] // HEAD_DIM) == (lane[None, :] // HEAD_DIM), F32)
    grp = _group_mean_matrix()
    proj3 = proj.reshape(batch, seq, proj.shape[1])
    out = pl.pallas_call(
        _hgrn_kernel,
        grid=(spb,),
        in_specs=[col(0), col(1), col(2), col(3), full(lb2), full(gn), full(dst), full(masks),
                  full(bd), full(grp)],
        out_specs=pl.BlockSpec((batch, rb, width), lambda s: (0, s, 0)),
        out_shape=jax.ShapeDtypeStruct((batch, seq, width), BF16),
        scratch_shapes=[pltpu.VMEM((batch, n_heads // 2, LANES, LANES), F32)],
        compiler_params=pltpu.CompilerParams(
            dimension_semantics=("arbitrary",), vmem_limit_bytes=VMEM_LIMIT),
    )(proj3, proj3, proj3, proj3, lb2, gn, dst, masks, bd, grp)
    return out.reshape(n, width)


def _fox_prep_kernel(q_ref, k_ref, v_ref, gate_ref, bias_ref, gq_ref, gk_ref, tril_ref,
                     sq_ref, sk_ref, cq_ref, ck_ref, cv_ref, grp_ref,
                     qa_ref, ka_ref, va_ref, carry_ref):
    n_heads = qa_ref.shape[1]
    tm = q_ref.shape[0]

    @pl.when(pl.program_id(1) == 0)
    def _():
        carry_ref[...] = jnp.zeros_like(carry_ref)

    z = gate_ref[...] + bias_ref[...]
    ls = -(jnp.maximum(-z, 0.0) + jnp.log(1.0 + jnp.exp(-jnp.abs(z))))
    cum = _dot3(tril_ref[...], ls) + carry_ref[...]
    carry_ref[...] = cum[tm - 1:tm]
    cum = cum * LOG2E

    c3 = jnp.concatenate(_split3(cum), axis=-1)
    ext_q = jnp.dot(c3, sq_ref[...], preferred_element_type=F32) + cq_ref[...]
    ext_k = jnp.dot(c3, sk_ref[...], preferred_element_type=F32) + ck_ref[...]

    lane = lax.broadcasted_iota(jnp.int32, (tm, LANES), 1)
    low_half = lane < HEAD_DIM
    grp = grp_ref[...]
    scale = HEAD_DIM ** -0.5 * LOG2E
    for c in range(n_heads // 2):
        cols = slice(c * LANES, (c + 1) * LANES)
        q = q_ref[:, cols].astype(F32)
        k = k_ref[:, cols].astype(F32)
        v = v_ref[:, cols]
        qn = q * lax.rsqrt(_dot2_rhs(q * q, grp) + EPS) * gq_ref[...] * scale
        kn = k * lax.rsqrt(_dot2_rhs(k * k, grp) + EPS) * gk_ref[...]
        for par in range(2):
            h = 2 * c + par
            data = low_half if par == 0 else jnp.logical_not(low_half)
            ext = slice(h * LANES, (h + 1) * LANES)
            qa_ref[0, h] = jnp.where(data, qn, ext_q[:, ext]).astype(BF16)
            ka_ref[0, h] = jnp.where(data, kn, ext_k[:, ext]).astype(BF16)
            va_ref[0, h] = jnp.where(data, v, cv_ref[par:par + 1, :].astype(BF16))


def _fox_layout_constants(n_heads):
    sq = np.zeros((3 * LANES, n_heads * LANES), np.float32)
    sk = np.zeros((3 * LANES, n_heads * LANES), np.float32)
    cq = np.zeros((1, n_heads * LANES), np.float32)
    ck = np.zeros((1, n_heads * LANES), np.float32)
    cv = np.zeros((2, LANES), np.float32)
    for h in range(n_heads):
        x0 = h * LANES + (HEAD_DIM if h % 2 == 0 else 0)
        for t in range(3):
            sq[t * LANES + h, x0 + t] = 1.0
            sk[t * LANES + h, x0 + 3 + t] = -1.0
        cq[0, x0 + 3:x0 + 6] = 1.0
        ck[0, x0:x0 + 3] = 1.0
    cv[0, HEAD_DIM] = 1.0
    cv[1, 0] = 1.0
    return (jnp.asarray(sq, BF16), jnp.asarray(sk, BF16), jnp.asarray(cq), jnp.asarray(ck), jnp.asarray(cv))


def _group_mean_matrix():
    lane = np.arange(LANES)
    return jnp.asarray(((lane[:, None] // HEAD_DIM) == (lane[None, :] // HEAD_DIM)) / HEAD_DIM, BF16)


def _fox_prep(proj, gates, f_bias, q_norm, k_norm, batch, seq, col0):
    width = 512
    n_heads = width // HEAD_DIM
    tm = PREP_ROWS
    spb = seq // tm
    col = lambda j: pl.BlockSpec((tm, width), lambda b, s, j=j: (b * spb + s, col0 + j))
    full = lambda a: pl.BlockSpec(a.shape, lambda b, s: (0,) * a.ndim)
    bias = jnp.zeros((1, LANES), F32).at[0, :n_heads].set(f_bias)
    gq = jnp.tile(q_norm, LANES // HEAD_DIM).reshape(1, LANES)
    gk = jnp.tile(k_norm, LANES // HEAD_DIM).reshape(1, LANES)
    tril = jnp.asarray(np.tril(np.ones((tm, tm), np.float32)), BF16)
    consts = _fox_layout_constants(n_heads) + (_group_mean_matrix(),)
    out = jax.ShapeDtypeStruct((batch, n_heads, seq, LANES), BF16)
    ospec = pl.BlockSpec((1, n_heads, tm, LANES), lambda b, s: (b, 0, s, 0))
    return pl.pallas_call(
        _fox_prep_kernel,
        grid=(batch, spb),
        in_specs=[col(0), col(1), col(2),
                  pl.BlockSpec((tm, LANES), lambda b, s: (b * spb + s, 0)),
                  full(bias), full(gq), full(gk), full(tril)] + [full(a) for a in consts],
        out_specs=[ospec, ospec, ospec],
        out_shape=[out, out, out],
        scratch_shapes=[pltpu.VMEM((1, LANES), F32)],
        compiler_params=pltpu.CompilerParams(
            dimension_semantics=("parallel", "arbitrary"), vmem_limit_bytes=VMEM_LIMIT),
    )(proj, proj, proj, gates, bias, gq, gk, tril, *consts)


def _tri_tables(nq):
    qi = [q for q in range(nq) for _ in range(q + 1)]
    ki = [k for q in range(nq) for k in range(q + 1)]
    return jnp.asarray(qi, jnp.int32), jnp.asarray(ki, jnp.int32)


def _fox_attn_kernel(qt_ref, kt_ref, q_ref, k_ref, v_ref, o_ref, m_ref, acc_ref):
    p_idx = pl.program_id(2)
    qi = qt_ref[p_idx]
    ki = kt_ref[p_idx]
    hp = q_ref.shape[1]
    t = q_ref.shape[2]

    @pl.when(ki == 0)
    def _():
        m_ref[...] = jnp.full_like(m_ref, -jnp.inf)
        acc_ref[...] = jnp.zeros_like(acc_ref)

    def step(masked):
        for h in range(hp):
            s = _dot_nt(q_ref[0, h], k_ref[0, h])
            if masked:
                row = lax.broadcasted_iota(jnp.int32, (t, t), 0)
                colm = lax.broadcasted_iota(jnp.int32, (t, t), 1)
                s = jnp.where(colm <= row, s, -jnp.inf)
            m_old = m_ref[h]
            m_new = jnp.maximum(m_old, jnp.max(s, axis=-1, keepdims=True))
            p = jnp.exp2(s - m_new)
            acc_ref[h] = (jnp.exp2(m_old - m_new) * acc_ref[h]
                          + jnp.dot(p.astype(BF16), v_ref[0, h], preferred_element_type=F32))
            m_ref[h] = m_new

    @pl.when(ki < qi)
    def _():
        step(False)

    @pl.when(ki == qi)
    def _():
        step(True)
        _fox_finalize(acc_ref, o_ref)


def _fox_finalize(acc_ref, o_ref):
    a0 = acc_ref[0]
    a1 = acc_ref[1]
    lane = lax.broadcasted_iota(jnp.int32, a0.shape, 1)
    o_ref[0] = jnp.where(lane < HEAD_DIM, a0 / a0[:, HEAD_DIM:HEAD_DIM + 1], a1 / a1[:, 0:1]).astype(o_ref.dtype)


def _tile_plan(n_sub, diagonal):
    plan = []
    for qb in range(n_sub):
        if not diagonal:
            plan.append((qb, 0, n_sub, False))
        else:
            if qb > 0:
                plan.append((qb, 0, qb, False))
            plan.append((qb, qb, qb + 1, True))
    return plan


def _fox_fast_kernel(qt_ref, kt_ref, q_ref, k_ref, v_ref, o_ref, acc_ref):
    p_idx = pl.program_id(2)
    qi = qt_ref[p_idx]
    ki = kt_ref[p_idx]
    hp = q_ref.shape[1]
    sb = ATTN_FAST_SUB
    n_sub = q_ref.shape[2] // sb

    @pl.when(ki == 0)
    def _():
        acc_ref[...] = jnp.zeros_like(acc_ref)

    def tile(diagonal):
        for h in range(hp):
            for qb, k0, k1, masked in _tile_plan(n_sub, diagonal):
                rows = pl.ds(qb * sb, sb)
                cols = pl.ds(k0 * sb, (k1 - k0) * sb)
                s = _dot_nt(q_ref[0, h, rows, :], k_ref[0, h, cols, :])
                if masked:
                    row = lax.broadcasted_iota(jnp.int32, (sb, sb), 0)
                    colm = lax.broadcasted_iota(jnp.int32, (sb, sb), 1)
                    s = jnp.where(colm <= row, s, -jnp.inf)
                p = jnp.exp2(s).astype(BF16)
                acc_ref[h, rows, :] += jnp.dot(p, v_ref[0, h, cols, :], preferred_element_type=F32)

    @pl.when(ki < qi)
    def _():
        tile(False)

    @pl.when(ki == qi)
    def _():
        tile(True)
        _fox_finalize(acc_ref, o_ref)


def _fox_attention(qa, ka, va, fast):
    batch, n_heads, seq, _ = qa.shape
    t = ATTN_FAST_TILE if fast else ATTN_TILE
    hp = 2
    nq = seq // t
    qt, kt = _tri_tables(nq)
    qspec = pl.BlockSpec((1, hp, t, LANES), lambda b, g, p, qt, kt: (b, g, qt[p], 0))
    kspec = pl.BlockSpec((1, hp, t, LANES), lambda b, g, p, qt, kt: (b, g, kt[p], 0))
    scratch = [pltpu.VMEM((hp, t, LANES), F32)]
    if not fast:
        scratch = [pltpu.VMEM((hp, t, 1), F32)] + scratch
    return pl.pallas_call(
        _fox_fast_kernel if fast else _fox_attn_kernel,
        grid_spec=pltpu.PrefetchScalarGridSpec(
            num_scalar_prefetch=2,
            grid=(batch, n_heads // hp, int(qt.shape[0])),
            in_specs=[qspec, kspec, kspec],
            out_specs=pl.BlockSpec((1, t, hp * HEAD_DIM), lambda b, g, p, qt, kt: (b, qt[p], g)),
            scratch_shapes=scratch),
        out_shape=jax.ShapeDtypeStruct((batch, seq, n_heads * HEAD_DIM), BF16),
        compiler_params=pltpu.CompilerParams(
            dimension_semantics=("parallel", "parallel", "arbitrary"), vmem_limit_bytes=VMEM_LIMIT),
    )(qt, kt, qa, ka, va)


def _logit_bound(q_gain, k_gain):
    return HEAD_DIM ** 0.5 * jnp.max(jnp.abs(q_gain)) * jnp.max(jnp.abs(k_gain))


def _diff_prep_kernel(q_ref, k_ref, v_ref, pos_ref, invf_ref, gq_ref, gk_ref, grp_ref, sel_ref,
                      qm_ref, k2_ref, va_ref, cs_ref, sn_ref):
    n_heads = k2_ref.shape[1]
    tm = q_ref.shape[0]
    ang = pos_ref[...].astype(F32) * invf_ref[...]
    cs_c = jnp.cos(ang)
    sn_c = jnp.sin(ang)
    per_row = LANES // (HEAD_DIM // 2)
    for j in range(per_row):
        cs_ref[pl.ds(j, tm // per_row, stride=per_row), :] = _dot3_rhs(cs_c, sel_ref[j])
        sn_ref[pl.ds(j, tm // per_row, stride=per_row), :] = _dot3_rhs(sn_c, sel_ref[j])
    lane = lax.broadcasted_iota(jnp.int32, (tm, LANES), 1)
    first = (lane % HEAD_DIM) < (HEAD_DIM // 2)
    cs = cs_ref[...]
    sn = sn_ref[...]
    sn = jnp.where(first, -sn, sn)
    grp = grp_ref[...]
    scale = HEAD_DIM ** -0.5 * LOG2E
    zero = jnp.zeros((tm, LANES), F32)
    onecol = jnp.where(lane == 0, 1.0, 0.0).astype(BF16)

    def norm_rope(x, gain):
        ms = _dot2_rhs(x * x, grp)
        y = x * lax.rsqrt(ms + EPS) * gain
        yr = jnp.where(first, pltpu.roll(y, LANES - HEAD_DIM // 2, 1), pltpu.roll(y, HEAD_DIM // 2, 1))
        return y * cs + yr * sn

    for h in range(n_heads):
        cols = slice(h * LANES, (h + 1) * LANES)
        qr = norm_rope(q_ref[:, cols].astype(F32), gq_ref[...]) * scale
        kr = norm_rope(k_ref[:, cols].astype(F32), gk_ref[...])
        qm_ref[0, h, 0] = jnp.where(lane < HEAD_DIM, qr, zero).astype(BF16)
        qm_ref[0, h, 1] = jnp.where(lane < HEAD_DIM, zero, qr).astype(BF16)
        k2_ref[0, h] = kr.astype(BF16)
        va_ref[0, h] = jnp.concatenate([v_ref[:, cols].astype(BF16), onecol], axis=-1)


def _diff_prep(proj, positions, q_norm, k_norm, batch, seq):
    n = proj.shape[0]
    width = proj.shape[1] // 3
    n_heads = width // LANES
    tm = PREP_ROWS
    spb = seq // tm
    col = lambda j: pl.BlockSpec((tm, width), lambda b, s, j=j: (b * spb + s, j))
    full = lambda a: pl.BlockSpec(a.shape, lambda b, s: (0,) * a.ndim)
    half = HEAD_DIM // 2
    inv_freq = ROPE_THETA ** (-jnp.arange(half, dtype=F32) / half)
    invf = jnp.tile(inv_freq, LANES // half).reshape(1, LANES)
    gq = jnp.tile(q_norm, LANES // HEAD_DIM).reshape(1, LANES)
    gk = jnp.tile(k_norm, LANES // HEAD_DIM).reshape(1, LANES)
    grp = _group_mean_matrix()
    per_row = LANES // half
    pos = jnp.repeat(positions.reshape(n // per_row, per_row).astype(jnp.int32), half, axis=1)
    lane = np.arange(LANES)
    sel = jnp.asarray(np.stack([(lane[:, None] == j * half + lane[None, :] % half) for j in range(per_row)]),
                      BF16)
    return pl.pallas_call(
        _diff_prep_kernel,
        grid=(batch, spb),
        in_specs=[col(0), col(1), col(2),
                  pl.BlockSpec((tm // per_row, LANES), lambda b, s: (b * spb + s, 0)),
                  full(invf), full(gq), full(gk), full(grp), full(sel)],
        out_specs=[pl.BlockSpec((1, n_heads, 2, tm, LANES), lambda b, s: (b, 0, 0, s, 0)),
                   pl.BlockSpec((1, n_heads, tm, LANES), lambda b, s: (b, 0, s, 0)),
                   pl.BlockSpec((1, n_heads, tm, 2 * LANES), lambda b, s: (b, 0, s, 0))],
        out_shape=[jax.ShapeDtypeStruct((batch, n_heads, 2, seq, LANES), BF16),
                   jax.ShapeDtypeStruct((batch, n_heads, seq, LANES), BF16),
                   jax.ShapeDtypeStruct((batch, n_heads, seq, 2 * LANES), BF16)],
        scratch_shapes=[pltpu.VMEM((tm, LANES), F32), pltpu.VMEM((tm, LANES), F32)],
        compiler_params=pltpu.CompilerParams(
            dimension_semantics=("parallel", "parallel"), vmem_limit_bytes=VMEM_LIMIT),
    )(proj, proj, proj, pos, invf, gq, gk, grp, sel)


def _diff_attn_kernel(lambda_init, qt_ref, kt_ref, q_ref, k_ref, v_ref, lam_ref, sub_ref,
                      o_ref, m_ref, acc_ref):
    p_idx = pl.program_id(2)
    qi = qt_ref[p_idx]
    ki = kt_ref[p_idx]
    t = k_ref.shape[2]
    dv = o_ref.shape[2]

    @pl.when(ki == 0)
    def _():
        m_ref[...] = jnp.full_like(m_ref, -jnp.inf)
        acc_ref[...] = jnp.zeros_like(acc_ref)

    def step(masked):
        for m in range(2):
            s = _dot_nt(q_ref[0, 0, m], k_ref[0, 0])
            if masked:
                row = lax.broadcasted_iota(jnp.int32, (t, t), 0) // CHUNK
                colm = lax.broadcasted_iota(jnp.int32, (t, t), 1) // CHUNK
                s = jnp.where(colm <= row, s, -jnp.inf)
            m_old = m_ref[m]
            m_new = jnp.maximum(m_old, jnp.max(s, axis=-1, keepdims=True))
            p = jnp.exp2(s - m_new)
            acc_ref[m] = (jnp.exp2(m_old - m_new) * acc_ref[m]
                          + jnp.dot(p.astype(BF16), v_ref[0, 0], preferred_element_type=F32))
            m_ref[m] = m_new

    @pl.when(ki < qi)
    def _():
        step(False)

    @pl.when(ki == qi)
    def _():
        step(True)
        _diff_finalize(lambda_init, acc_ref, lam_ref, sub_ref, o_ref)


def _diff_finalize(lambda_init, acc_ref, lam_ref, sub_ref, o_ref):
    dv = o_ref.shape[2]
    lp = lam_ref[...]
    lam = (jnp.exp(jnp.sum(lp[0:1] * lp[1:2], axis=-1, keepdims=True))
           - jnp.exp(jnp.sum(lp[2:3] * lp[3:4], axis=-1, keepdims=True)) + lambda_init)
    a0 = acc_ref[0]
    a1 = acc_ref[1]
    o = a0[:, :dv] / a0[:, dv:dv + 1] - lam * (a1[:, :dv] / a1[:, dv:dv + 1])
    ms = jnp.mean(o * o, axis=-1, keepdims=True)
    o_ref[0] = ((o * lax.rsqrt(ms + EPS) * sub_ref[...]) * (1.0 - lambda_init)).astype(o_ref.dtype)


def _diff_fast_kernel(lambda_init, qt_ref, kt_ref, q_ref, k_ref, v_ref, lam_ref, sub_ref, o_ref, acc_ref):
    p_idx = pl.program_id(2)
    qi = qt_ref[p_idx]
    ki = kt_ref[p_idx]
    sb = ATTN_FAST_SUB
    n_sub = k_ref.shape[2] // sb

    @pl.when(ki == 0)
    def _():
        acc_ref[...] = jnp.zeros_like(acc_ref)

    def tile(diagonal):
        for m in range(2):
            for qb, k0, k1, masked in _tile_plan(n_sub, diagonal):
                rows = pl.ds(qb * sb, sb)
                cols = pl.ds(k0 * sb, (k1 - k0) * sb)
                s = _dot_nt(q_ref[0, 0, m, rows, :], k_ref[0, 0, cols, :])
                if masked:
                    row = lax.broadcasted_iota(jnp.int32, (sb, sb), 0) // CHUNK
                    colm = lax.broadcasted_iota(jnp.int32, (sb, sb), 1) // CHUNK
                    s = jnp.where(colm <= row, s, -jnp.inf)
                p = jnp.exp2(s).astype(BF16)
                acc_ref[m, rows, :] += jnp.dot(p, v_ref[0, 0, cols, :], preferred_element_type=F32)

    @pl.when(ki < qi)
    def _():
        tile(False)

    @pl.when(ki == qi)
    def _():
        tile(True)
        _diff_finalize(lambda_init, acc_ref, lam_ref, sub_ref, o_ref)


def _diff_attention(qm, k2, va, lam_params, subln, lambda_init, fast):
    batch, n_heads, seq, _ = k2.shape
    dv = va.shape[3] // 2
    t = ATTN_FAST_TILE if fast else ATTN_TILE
    nq = seq // t
    qt, kt = _tri_tables(nq)
    lamp = jnp.zeros((8, LANES), F32).at[:4, :HEAD_DIM].set(lam_params)
    sub = subln.reshape(1, dv)
    scratch = [pltpu.VMEM((2, t, 2 * dv), F32)]
    if not fast:
        scratch = [pltpu.VMEM((2, t, 1), F32)] + scratch
    return pl.pallas_call(
        functools.partial(_diff_fast_kernel if fast else _diff_attn_kernel, lambda_init),
        grid_spec=pltpu.PrefetchScalarGridSpec(
            num_scalar_prefetch=2,
            grid=(batch, n_heads, int(qt.shape[0])),
            in_specs=[pl.BlockSpec((1, 1, 2, t, LANES), lambda b, h, p, qt, kt: (b, h, 0, qt[p], 0)),
                      pl.BlockSpec((1, 1, t, LANES), lambda b, h, p, qt, kt: (b, h, kt[p], 0)),
                      pl.BlockSpec((1, 1, t, 2 * dv), lambda b, h, p, qt, kt: (b, h, kt[p], 0)),
                      pl.BlockSpec((8, LANES), lambda b, h, p, qt, kt: (0, 0)),
                      pl.BlockSpec((1, dv), lambda b, h, p, qt, kt: (0, 0))],
            out_specs=pl.BlockSpec((1, t, dv), lambda b, h, p, qt, kt: (b, qt[p], h)),
            scratch_shapes=scratch),
        out_shape=jax.ShapeDtypeStruct((batch, seq, n_heads * dv), BF16),
        compiler_params=pltpu.CompilerParams(
            dimension_semantics=("parallel", "parallel", "arbitrary"), vmem_limit_bytes=VMEM_LIMIT),
    )(qt, kt, qm, k2, va, lamp, sub)


def _store_token_tiles(ref, value):
    t, width = value.shape
    s = width // LANES
    for j in range(s):
        ref[pl.ds(j, t, stride=s), :] = value[:, j * LANES:(j + 1) * LANES]


def _load_token_tiles(ref, first_row, t, s):
    return jnp.concatenate([ref[pl.ds(first_row + j, t, stride=s), :] for j in range(s)], axis=-1)


def _router_kernel(h_ref, g_ref, whi_ref, wlo_ref, tri_ref, xn_ref, route_ref, cnt_ref, run_ref):
    @pl.when(pl.program_id(0) == 0)
    def _():
        run_ref[...] = jnp.zeros_like(run_ref)

    x = h_ref[...]
    tm = x.shape[0]
    xn = x * lax.rsqrt(jnp.mean(x * x, axis=-1, keepdims=True) + EPS) * g_ref[...]
    _store_token_tiles(xn_ref, xn)
    xh = xn.astype(BF16)
    xl = (xn - xh.astype(F32)).astype(BF16)
    d = lambda a, b: jnp.dot(a, b[...], preferred_element_type=F32)
    logits = d(xh, whi_ref) + (d(xl, whi_ref) + d(xh, wlo_ref))
    lane = lax.broadcasted_iota(jnp.int32, (tm, LANES), 1)
    neg = jnp.full((tm, LANES), -jnp.inf, F32)
    big = jnp.full((tm, LANES), LANES, jnp.int32)

    def top1(vals):
        m = jnp.max(vals, axis=-1, keepdims=True)
        idx = jnp.min(jnp.where(vals == m, lane, big), axis=-1, keepdims=True)
        return m, idx

    grp_logits = jnp.where(lane < N_GROUPS, logits, neg)
    mg, gidx = top1(grp_logits)
    p_g = 1.0 / jnp.sum(jnp.exp(grp_logits - mg), axis=-1, keepdims=True)
    e_lane = lane - N_GROUPS
    in_grp = (e_lane >= gidx * EXPERTS_PER_GROUP) & (e_lane < (gidx + 1) * EXPERTS_PER_GROUP)
    sel = jnp.where(in_grp, logits, neg)
    m1, i1 = top1(sel)
    m2, i2 = top1(jnp.where(lane == i1, neg, sel))
    r = jnp.exp(m2 - m1)
    w1 = p_g / (1.0 + r)
    w2 = p_g * r / (1.0 + r)
    zero = jnp.zeros((tm, LANES), F32)
    chosen = jnp.where((lane == i1) | (lane == i2), 1.0, 0.0)
    before = jnp.dot(tri_ref[...], chosen.astype(BF16), preferred_element_type=F32) + run_ref[...]
    rank1 = jnp.sum(jnp.where(lane == i1, before, zero), axis=-1, keepdims=True)
    rank2 = jnp.sum(jnp.where(lane == i2, before, zero), axis=-1, keepdims=True)
    run = run_ref[...] + jnp.sum(chosen, axis=0, keepdims=True)
    run_ref[...] = run
    cnt_ref[...] = jnp.broadcast_to(run, cnt_ref.shape)
    route_ref[...] = jnp.where(lane == 0, (i1 - N_GROUPS).astype(F32),
                     jnp.where(lane == 1, (i2 - N_GROUPS).astype(F32),
                     jnp.where(lane == 2, w1, jnp.where(lane == 3, w2,
                     jnp.where(lane == 4, rank1, jnp.where(lane == 5, rank2, zero))))))


def _router(h, gain, w_group, w_expert):
    n, d = h.shape
    tm = ROUTER_ROWS
    wr = jnp.zeros((d, LANES), F32)
    wr = wr.at[:, :N_GROUPS].set(w_group)
    wr = wr.at[:, N_GROUPS:N_GROUPS + N_EXPERTS].set(
        jnp.transpose(w_expert, (1, 0, 2)).reshape(d, N_EXPERTS))
    w_hi = wr.astype(BF16)
    w_lo = (wr - w_hi.astype(F32)).astype(BF16)
    tri = jnp.asarray(np.tril(np.ones((tm, tm), np.float32), -1), BF16)
    xn, route, counts = pl.pallas_call(
        _router_kernel,
        grid=(n // tm,),
        in_specs=[pl.BlockSpec((tm, d), lambda i: (i, 0)),
                  pl.BlockSpec((1, d), lambda i: (0, 0)),
                  pl.BlockSpec((d, LANES), lambda i: (0, 0)),
                  pl.BlockSpec((d, LANES), lambda i: (0, 0)),
                  pl.BlockSpec((tm, tm), lambda i: (0, 0))],
        out_specs=[pl.BlockSpec((tm * d // LANES, LANES), lambda i: (i, 0)),
                   pl.BlockSpec((tm, LANES), lambda i: (i, 0)),
                   pl.BlockSpec((TILE_ROWS, LANES), lambda i: (0, 0))],
        out_shape=[jax.ShapeDtypeStruct((n * d // LANES, LANES), F32),
                   jax.ShapeDtypeStruct((n, LANES), F32),
                   jax.ShapeDtypeStruct((TILE_ROWS, LANES), F32)],
        scratch_shapes=[pltpu.VMEM((1, LANES), F32)],
        compiler_params=pltpu.CompilerParams(
            dimension_semantics=("arbitrary",), vmem_limit_bytes=VMEM_LIMIT),
    )(h, gain.reshape(1, d), w_hi, w_lo, tri)
    return xn, route, counts[0, N_GROUPS:N_GROUPS + N_EXPERTS].astype(jnp.int32)


def _dispatch_tables(expert_ids, rank, counts, tm):
    n = expert_ids.shape[0]
    n_tiles = 2 * n // tm + N_EXPERTS
    padded = ((counts + tm - 1) // tm) * tm
    ends = jnp.cumsum(padded)
    starts = ends - padded
    pos = jnp.take(starts, expert_ids) + rank
    tile_start = jnp.arange(n_tiles, dtype=jnp.int32) * tm
    tile_expert = jnp.minimum(jnp.sum(tile_start[:, None] >= ends[None, :], axis=1),
                              N_EXPERTS - 1).astype(jnp.int32)
    n_valid = (ends[-1] // tm).astype(jnp.int32).reshape(1)
    pad_start = (starts + counts).astype(jnp.int32)
    pad_count = (padded - counts).astype(jnp.int32)
    return pos.reshape(n, 2).astype(jnp.int32), tile_expert, n_valid, n_tiles, pad_start, pad_count


def _token_copy(src_hbm, tok, dst_ref, r, sem):
    src = src_hbm.at[pl.ds(pl.multiple_of(tok * TILE_ROWS, TILE_ROWS), TILE_ROWS)]
    first = r * TILE_ROWS if isinstance(r, int) else pl.multiple_of(r * TILE_ROWS, TILE_ROWS)
    return pltpu.make_async_copy(src, dst_ref.at[pl.ds(first, TILE_ROWS)], sem)


def _gather_tokens(src_hbm, idx_ref, dst_ref, sem, n_tokens):
    def body(r, carry):
        _token_copy(src_hbm, idx_ref[0, 0, r], dst_ref, r, sem).start()
        return carry
    lax.fori_loop(0, n_tokens, body, 0, unroll=8)


def _wait_tokens(src_hbm, dst_ref, sem):
    pltpu.make_async_copy(src_hbm.at[pl.ds(0, dst_ref.shape[0])], dst_ref, sem).wait()


def _dispatch_kernel(ps_ref, pc_ref, nv_ref, pos_ref, x_ref, xs_hbm, zero_blk, sem, pad_sem):
    i = pl.program_id(0)
    tokens = pos_ref.shape[2] // 2

    def slot_tile(slot):
        return xs_hbm.at[pl.ds(pl.multiple_of(slot * TILE_ROWS, TILE_ROWS), TILE_ROWS)]

    for r in range(2 * tokens):
        src = x_ref.at[pl.ds((r % tokens) * TILE_ROWS, TILE_ROWS)]
        pltpu.make_async_copy(src, slot_tile(pos_ref[0, 0, r]), sem).start(priority=r % 2)
    rows = 2 * tokens * TILE_ROWS
    pltpu.make_async_copy(xs_hbm.at[pl.ds(0, rows)], xs_hbm.at[pl.ds(0, rows)], sem).wait()

    @pl.when(i == pl.num_programs(0) - 1)
    def _():
        zero_blk[...] = jnp.zeros_like(zero_blk)
        tile_slots = zero_blk.shape[0] // TILE_ROWS
        for e in range(ps_ref.shape[0]):
            first = ps_ref[e]
            count = pc_ref[e]
            for wait in (False, True):
                for bit in range(tile_slots.bit_length() - 1):
                    chunk = (1 << bit) * TILE_ROWS
                    start = (first + (count & ((1 << bit) - 1))) * TILE_ROWS
                    copy = pltpu.make_async_copy(
                        zero_blk.at[pl.ds(0, chunk)],
                        xs_hbm.at[pl.ds(pl.multiple_of(start, TILE_ROWS), chunk)], pad_sem)

                    @pl.when(((count >> bit) & 1) == 1)
                    def _(copy=copy, wait=wait):
                        copy.wait() if wait else copy.start()

        block_rows = zero_blk.shape[0]
        n_blocks = xs_hbm.shape[0] // block_rows

        def block(t):
            return xs_hbm.at[pl.ds(pl.multiple_of(t * block_rows, block_rows), block_rows)]

        def fill_block(t, carry):
            pltpu.make_async_copy(zero_blk, block(t), pad_sem).start()
            return carry

        def drain_block(t, carry):
            pltpu.make_async_copy(zero_blk, block(t), pad_sem).wait()
            return carry

        lax.fori_loop(nv_ref[0], n_blocks, fill_block, 0)
        lax.fori_loop(nv_ref[0], n_blocks, drain_block, 0)


def _moe_dispatch(xn, pos, pad_start, pad_count, n_valid, n_tiles, tm):
    n = pos.shape[0]
    tb = COMBINE_ROWS
    steps = n // tb
    pos_tab = jnp.transpose(pos.reshape(steps, tb, 2), (0, 2, 1)).reshape(steps, 1, 2 * tb)
    return pl.pallas_call(
        _dispatch_kernel,
        grid_spec=pltpu.PrefetchScalarGridSpec(
            num_scalar_prefetch=3,
            grid=(steps,),
            in_specs=[pl.BlockSpec((1, 1, 2 * tb), lambda i, ps, pc, nv: (i, 0, 0), memory_space=pltpu.SMEM),
                      pl.BlockSpec((tb * TILE_ROWS, LANES), lambda i, ps, pc, nv: (i, 0))],
            out_specs=pl.BlockSpec(memory_space=pl.ANY),
            scratch_shapes=[pltpu.VMEM((tm * TILE_ROWS, LANES), F32),
                            pltpu.SemaphoreType.DMA, pltpu.SemaphoreType.DMA]),
        out_shape=jax.ShapeDtypeStruct((n_tiles * tm * TILE_ROWS, LANES), F32),
        compiler_params=pltpu.CompilerParams(
            dimension_semantics=("arbitrary",), vmem_limit_bytes=VMEM_LIMIT),
    )(pad_start, pad_count, n_valid, pos_tab, xn)


def _moe_kernel(te_ref, nv_ref, x_ref, wg_ref, wu_ref, wd_ref, o_ref, wg_b, wu_b, wd_b):
    i = pl.program_id(0)
    tm = x_ref.shape[0] // TILE_ROWS
    n_valid = nv_ref[0]
    new_expert = jnp.logical_or(i == 0, te_ref[i] != te_ref[jnp.maximum(i - 1, 0)])

    @pl.when(jnp.logical_and(i < n_valid, new_expert))
    def _():
        wg_b[...] = wg_ref[0].astype(BF16)
        wu_b[...] = wu_ref[0].astype(BF16)
        wd_b[...] = wd_ref[0].astype(BF16)

    @pl.when(i < n_valid)
    def _():
        x = _load_token_tiles(x_ref, 0, tm, TILE_ROWS).astype(BF16)
        g = jnp.dot(x, wg_b[...], preferred_element_type=F32)
        u = jnp.dot(x, wu_b[...], preferred_element_type=F32)
        hid = (g * _sigmoid(g) * u).astype(BF16)
        _store_token_tiles(o_ref, jnp.dot(hid, wd_b[...], preferred_element_type=F32))

    @pl.when(i >= n_valid)
    def _():
        o_ref[...] = jnp.zeros_like(o_ref)


def _moe_experts(xs, tile_expert, n_valid, tm, w_gate, w_up, w_down):
    n_exp, d, f = w_gate.shape
    assert d == TILE_ROWS * LANES
    n_tiles = xs.shape[0] // (tm * TILE_ROWS)
    x_index = lambda i, te, nv: (jnp.minimum(i, nv[0] - 1), 0)
    return pl.pallas_call(
        _moe_kernel,
        grid_spec=pltpu.PrefetchScalarGridSpec(
            num_scalar_prefetch=2,
            grid=(n_tiles,),
            in_specs=[
                pl.BlockSpec((tm * TILE_ROWS, LANES), x_index),
                pl.BlockSpec((1, d, f), lambda i, te, nv: (te[i], 0, 0)),
                pl.BlockSpec((1, d, f), lambda i, te, nv: (te[i], 0, 0)),
                pl.BlockSpec((1, f, d), lambda i, te, nv: (te[i], 0, 0))],
            out_specs=pl.BlockSpec((tm * TILE_ROWS, LANES), lambda i, te, nv: (i, 0)),
            scratch_shapes=[pltpu.VMEM((d, f), BF16), pltpu.VMEM((d, f), BF16), pltpu.VMEM((f, d), BF16)]),
        out_shape=jax.ShapeDtypeStruct((n_tiles * tm * TILE_ROWS, LANES), F32),
        compiler_params=pltpu.CompilerParams(
            dimension_semantics=("arbitrary",), vmem_limit_bytes=VMEM_LIMIT),
    )(tile_expert, n_valid, xs, w_gate, w_up, w_down)


def _combine_kernel(pos_ref, pos_next_ref, h_ref, route_ref, y_hbm, o_ref, ybuf, sems):
    i = pl.program_id(0)
    n_steps = pl.num_programs(0)
    tokens = ybuf.shape[1] // TILE_ROWS
    slot = i % 2

    @pl.when(i == 0)
    def _():
        _gather_tokens(y_hbm, pos_ref, ybuf.at[0], sems.at[0], tokens)

    @pl.when(i + 1 < n_steps)
    def _():
        for r in range(tokens):
            _token_copy(y_hbm, pos_next_ref[0, 0, r], ybuf.at[1 - slot], r,
                        sems.at[1 - slot]).start(priority=r % 2)

    _wait_tokens(y_hbm, ybuf.at[slot], sems.at[slot])
    tc = tokens // 2
    w = route_ref[...]
    first = _load_token_tiles(ybuf.at[slot], 0, tc, TILE_ROWS)
    second = _load_token_tiles(ybuf.at[slot], tc * TILE_ROWS, tc, TILE_ROWS)
    o_ref[...] = h_ref[...] + w[:, 2:3] * first + w[:, 3:4] * second


def _moe_combine(h, route, pos, y_sorted):
    n, d = h.shape
    tc = COMBINE_ROWS
    steps = n // tc
    pos_tab = jnp.transpose(pos.reshape(steps, tc, 2), (0, 2, 1)).reshape(steps, 1, 2 * tc)
    return pl.pallas_call(
        _combine_kernel,
        grid=(steps,),
        in_specs=[pl.BlockSpec((1, 1, 2 * tc), lambda i: (i, 0, 0), memory_space=pltpu.SMEM),
                  pl.BlockSpec((1, 1, 2 * tc), lambda i: (jnp.minimum(i + 1, steps - 1), 0, 0),
                               memory_space=pltpu.SMEM),
                  pl.BlockSpec((tc, d), lambda i: (i, 0)),
                  pl.BlockSpec((tc, LANES), lambda i: (i, 0)),
                  pl.BlockSpec(memory_space=pl.ANY)],
        out_specs=pl.BlockSpec((tc, d), lambda i: (i, 0)),
        out_shape=jax.ShapeDtypeStruct((n, d), F32),
        scratch_shapes=[pltpu.VMEM((2, 2 * tc * TILE_ROWS, LANES), F32), pltpu.SemaphoreType.DMA((2,))],
        compiler_params=pltpu.CompilerParams(
            dimension_semantics=("arbitrary",), vmem_limit_bytes=VMEM_LIMIT),
    )(pos_tab, pos_tab, h, route, y_sorted)


def _moe_layer(h, layer, gain, w_group, w_expert, w_gate, w_up, w_down):
    d = h.shape[1]
    f = w_gate.shape[-1]
    xn, route, counts = _router(h, gain, w_group, w_expert)
    expert_ids = route[:, :2].astype(jnp.int32)
    rank = route[:, 4:6].astype(jnp.int32)
    pos, tile_expert, n_valid, n_tiles, pad_start, pad_count = _dispatch_tables(expert_ids, rank, counts,
                                                                                MOE_TM)
    xs = _moe_dispatch(xn, pos, pad_start, pad_count, n_valid, n_tiles, MOE_TM)
    y_sorted = _moe_experts(xs, tile_expert + layer * N_EXPERTS, n_valid, MOE_TM,
                            w_gate.reshape(-1, d, f), w_up.reshape(-1, d, f), w_down.reshape(-1, f, d))
    return _moe_combine(h, route, pos, y_sorted)


def _even_layer(h, batch, seq, gain, w_in, w_out, lb, f_bias, out_norm, q_norm, k_norm):
    d = h.shape[1]
    n_main = w_in.shape[1] - f_bias.shape[0]
    w_main = w_in[:, :n_main].astype(BF16)
    w_gate = jnp.zeros((d, LANES), F32).at[:, :f_bias.shape[0]].set(w_in[:, n_main:])
    proj, gates = _norm_proj(h, gain, w_main, w_gate)
    o_a = _hgrn2(proj, lb, out_norm, batch, seq)
    qa, ka, va = _fox_prep(proj, gates, f_bias, q_norm, k_norm, batch, seq, col0=4)
    o_b = lax.cond(_logit_bound(q_norm, k_norm) <= LOGIT_BOUND_MAX,
                   functools.partial(_fox_attention, fast=True),
                   functools.partial(_fox_attention, fast=False), qa, ka, va).reshape(batch * seq, -1)
    wo = w_out.astype(BF16)
    ka_dim = o_a.shape[1]
    return _proj_residual([(o_a, wo[:ka_dim]), (o_b, wo[ka_dim:])], h)


def _odd_layer(h, positions, batch, seq, gain, w_in, w_out, q_norm, k_norm, lam_params, subln, lambda_init):
    proj = _norm_proj(h, gain, w_in.astype(BF16))
    qm, k2, va = _diff_prep(proj, positions, q_norm, k_norm, batch, seq)
    attn = lambda fast: functools.partial(_diff_attention, lam_params=lam_params, subln=subln,
                                          lambda_init=lambda_init, fast=fast)
    o = lax.cond(_logit_bound(q_norm, k_norm) <= LOGIT_BOUND_MAX,
                 attn(True), attn(False), qm, k2, va).reshape(batch * seq, -1)
    return _proj_residual([(o, w_out.astype(BF16))], h)


def kernel(x, positions, hgrn_lb_logits, norm_mix, norm_ffn, even_w_in, even_w_out, fox_f_bias,
           hgrn_out_norm, fox_q_norm, fox_k_norm, odd_w_in, odd_w_out, diff_q_norm, diff_k_norm,
           diff_lambda_q1, diff_lambda_k1, diff_lambda_q2, diff_lambda_k2, diff_subln,
           moe_router_group, moe_router_expert, moe_w_gate, moe_w_up, moe_w_down):
    batch, seq, d = x.shape
    depth = norm_mix.shape[0]
    lower_bounds = jnp.cumsum(jax.nn.softmax(hgrn_lb_logits.astype(F32), axis=0), axis=0)
    h = x.reshape(batch * seq, d)
    for layer in range(depth):
        j = layer // 2
        if layer % 2 == 0:
            h = _even_layer(h, batch, seq, norm_mix[layer], even_w_in[j], even_w_out[j], lower_bounds[j],
                            fox_f_bias[j], hgrn_out_norm[j], fox_q_norm[j], fox_k_norm[j])
        else:
            lambda_init = 0.8 - 0.6 * math.exp(-0.3 * layer)
            lam_params = jnp.stack([diff_lambda_q1[j], diff_lambda_k1[j],
                                    diff_lambda_q2[j], diff_lambda_k2[j]]).astype(F32)
            h = _odd_layer(h, positions, batch, seq, norm_mix[layer], odd_w_in[j], odd_w_out[j],
                           diff_q_norm[j], diff_k_norm[j], lam_params, diff_subln[j], lambda_init)
        h = _moe_layer(h, layer, norm_ffn[layer], moe_router_group[layer], moe_router_expert[layer],
                       moe_w_gate, moe_w_up, moe_w_down)
    return h.reshape(batch, seq, d)
```

```python
import functools
import math

import numpy as np
import jax
import jax.numpy as jnp
from jax import lax
from jax.experimental import pallas as pl
from jax.experimental.pallas import tpu as pltpu

F32 = jnp.float32
BF16 = jnp.bfloat16

EPS = 1e-6
ROPE_THETA = 10000.0
CHUNK = 64
HEAD_DIM = 64
N_GROUPS = 4
EXPERTS_PER_GROUP = 8
N_EXPERTS = N_GROUPS * EXPERTS_PER_GROUP
LANES = 128
TILE_ROWS = 8

HGRN_CHUNK = 64
HGRN_ROWS = 256
ATTN_TILE = 512
ATTN_FAST_TILE = 2048
ATTN_FAST_SUB = 512
LOGIT_BOUND_MAX = 60.0
LOG2E = math.log2(math.e)
PREP_ROWS = 256
PROJ_TM = 512
ROUTER_ROWS = 256
MOE_TM = 512
COMBINE_ROWS = 256
VMEM_LIMIT = 56 * 1024 * 1024


def _split3(x):
    hi = x.astype(BF16)
    r1 = x - hi.astype(F32)
    mid = r1.astype(BF16)
    lo = (r1 - mid.astype(F32)).astype(BF16)
    return hi, mid, lo


def _dot3(const_bf16, x):
    hi, mid, lo = _split3(x)
    d = lambda b: jnp.dot(const_bf16, b, preferred_element_type=F32)
    return d(hi) + d(mid) + d(lo)


def _dot3_stacked(const3_bf16, x):
    return jnp.dot(const3_bf16, jnp.concatenate(_split3(x), axis=0), preferred_element_type=F32)


def _dot3_rhs(x, const_bf16):
    hi, mid, lo = _split3(x)
    d = lambda a: jnp.dot(a, const_bf16, preferred_element_type=F32)
    return d(hi) + d(mid) + d(lo)


def _dot2_rhs(x, const_bf16):
    hi = x.astype(BF16)
    lo = (x - hi.astype(F32)).astype(BF16)
    d = lambda a: jnp.dot(a, const_bf16, preferred_element_type=F32)
    return d(hi) + d(lo)


def _dot_nt(a, b):
    return lax.dot_general(a, b, (((1,), (1,)), ((), ())), preferred_element_type=F32)


def _dot_tn(a, b):
    return lax.dot_general(a, b, (((0,), (0,)), ((), ())), preferred_element_type=F32)


def _sigmoid(x):
    return 1.0 / (1.0 + jnp.exp(-x))


def _norm_proj_kernel(has_aux, x_ref, g_ref, w_ref, *rest):
    x = x_ref[...]
    ms = jnp.mean(x * x, axis=-1, keepdims=True)
    xn = x * lax.rsqrt(ms + EPS) * g_ref[...]
    xb = xn.astype(BF16)
    rest[-2 if has_aux else -1][...] = jnp.dot(xb, w_ref[...], preferred_element_type=F32).astype(BF16)
    if has_aux:
        whi_ref, wlo_ref, _, oaux_ref = rest
        xl = (xn - xb.astype(F32)).astype(BF16)
        d = lambda a, b: jnp.dot(a, b[...], preferred_element_type=F32)
        oaux_ref[...] = d(xb, whi_ref) + (d(xl, whi_ref) + d(xb, wlo_ref))


def _norm_proj(x, gain, w, w_aux=None):
    n, d = x.shape
    m = w.shape[1]
    tm = PROJ_TM
    has_aux = w_aux is not None
    in_specs = [pl.BlockSpec((tm, d), lambda i: (i, 0)),
                pl.BlockSpec((1, d), lambda i: (0, 0)),
                pl.BlockSpec((d, m), lambda i: (0, 0))]
    out_specs = [pl.BlockSpec((tm, m), lambda i: (i, 0))]
    out_shape = [jax.ShapeDtypeStruct((n, m), BF16)]
    args = [x, gain.reshape(1, d), w]
    if has_aux:
        w_hi = w_aux.astype(BF16)
        w_lo = (w_aux - w_hi.astype(F32)).astype(BF16)
        in_specs += [pl.BlockSpec((d, LANES), lambda i: (0, 0))] * 2
        out_specs.append(pl.BlockSpec((tm, LANES), lambda i: (i, 0)))
        out_shape.append(jax.ShapeDtypeStruct((n, LANES), F32))
        args += [w_hi, w_lo]
    res = pl.pallas_call(
        functools.partial(_norm_proj_kernel, has_aux),
        grid=(n // tm,),
        in_specs=in_specs, out_specs=out_specs, out_shape=out_shape,
        compiler_params=pltpu.CompilerParams(
            dimension_semantics=("parallel",), vmem_limit_bytes=VMEM_LIMIT),
    )(*args)
    return res if has_aux else res[0]


def _proj_res_kernel(n_in, *refs):
    h_ref = refs[2 * n_in]
    o_ref = refs[2 * n_in + 1]
    acc = h_ref[...]
    for t in range(n_in):
        acc = acc + jnp.dot(refs[2 * t][...], refs[2 * t + 1][...], preferred_element_type=F32)
    o_ref[...] = acc


def _proj_residual(pairs, h):
    n, d = h.shape
    tm = PROJ_TM
    in_specs, args = [], []
    for a, w in pairs:
        k = a.shape[1]
        in_specs += [pl.BlockSpec((tm, k), lambda i: (i, 0)),
                     pl.BlockSpec((k, d), lambda i: (0, 0))]
        args += [a, w]
    in_specs.append(pl.BlockSpec((tm, d), lambda i: (i, 0)))
    args.append(h)
    return pl.pallas_call(
        functools.partial(_proj_res_kernel, len(pairs)),
        grid=(n // tm,),
        in_specs=in_specs,
        out_specs=pl.BlockSpec((tm, d), lambda i: (i, 0)),
        out_shape=jax.ShapeDtypeStruct((n, d), F32),
        compiler_params=pltpu.CompilerParams(
            dimension_semantics=("parallel",), vmem_limit_bytes=VMEM_LIMIT),
    )(*args)


_HGRN_LEVELS = (64, 32, 16)
_HGRN_DIAG = 8


def _hgrn_constants():
    c = HGRN_CHUNK
    idx = np.arange(c)
    low = (idx[None, :] <= idx[:, None]).astype(np.float64)

    def ref_rows(r):
        return (idx[None, :] <= r[:, None]).astype(np.float64)

    blocks = [low, ref_rows(np.full(c, c - 1)) - low]
    masks = []
    for b in _HGRN_LEVELS:
        start = (idx // b) * b
        upper = (idx - start) >= b // 2
        ref = start + b // 2 - 1
        blocks.append(low - ref_rows(np.where(upper, ref, idx)))
        blocks.append(ref_rows(np.where(upper, idx, ref)) - low)
        same = (idx[:, None] // b) == (idx[None, :] // b)
        masks.append(same & upper[:, None] & ~upper[None, :])
    ref = (idx // _HGRN_DIAG) * _HGRN_DIAG + _HGRN_DIAG // 2 - 1
    blocks.append(low - ref_rows(ref))
    blocks.append(ref_rows(ref) - low)
    same = (idx[:, None] // _HGRN_DIAG) == (idx[None, :] // _HGRN_DIAG)
    masks.append(same & (idx[None, :] <= idx[:, None]))
    dst = np.concatenate(blocks, axis=0)
    return dst.astype(np.float32), np.stack(masks).astype(np.float32)


def _hgrn_kernel(q_ref, f_ref, i_ref, g_ref, lb_ref, gn_ref, dst_ref, mask_ref, bd_ref, grp_ref,
                 o_ref, st_ref):
    c = HGRN_CHUNK
    n_batch, n_pairs = st_ref.shape[:2]
    n_lvl = mask_ref.shape[0]

    @pl.when(pl.program_id(0) == 0)
    def _():
        st_ref[...] = jnp.zeros_like(st_ref)

    lb = lb_ref[...]
    gn = gn_ref[...]
    dst = dst_ref[...]
    bd = bd_ref[...]
    grp = grp_ref[...]
    low = lax.broadcasted_iota(jnp.int32, (c, LANES), 1) < HEAD_DIM

    def stack(x):
        return jnp.concatenate([jnp.where(low, x, jnp.zeros_like(x)), jnp.where(low, jnp.zeros_like(x), x)],
                               axis=0)

    for ch, b in [(ch, b) for ch in range(q_ref.shape[1] // c) for b in range(n_batch)]:
        rows = pl.ds(ch * c, c)
        q = q_ref[b, rows, :].astype(F32)
        qf = q * _sigmoid(q)
        f = lb + (1.0 - lb) * _sigmoid(f_ref[b, rows, :].astype(F32))
        kk = 1.0 - f
        ex = jnp.exp(_dot3_stacked(dst, jnp.log(f)))
        v = i_ref[b, rows, :].astype(BF16)
        g = g_ref[b, rows, :].astype(F32)
        gate = g * _sigmoid(g)
        q_in = (qf * ex[0:c]).astype(BF16)
        k_st = (kk * ex[c:2 * c]).astype(BF16)
        dec = ex[c - 1:c]
        q_l = [(qf * ex[(2 + 2 * l) * c:(3 + 2 * l) * c]).astype(BF16) for l in range(n_lvl)]
        k_l = [(kk * ex[(3 + 2 * l) * c:(4 + 2 * l) * c]).astype(BF16) for l in range(n_lvl)]
        outs = []
        for p in range(n_pairs):
            ps = slice(p * LANES, (p + 1) * LANES)
            scores = mask_ref[0] * _dot_nt(stack(q_l[0][:, ps]), k_l[0][:, ps])
            for l in range(1, n_lvl):
                scores = scores + mask_ref[l] * _dot_nt(stack(q_l[l][:, ps]), k_l[l][:, ps])
            pv = jnp.dot(scores.astype(BF16), v[:, ps], preferred_element_type=F32)
            st = st_ref[b, p]
            o = jnp.where(low, pv[:c], pv[c:]) + _dot_nt(q_in[:, ps], st.astype(BF16))
            st_ref[b, p] = st * dec[:, ps] + bd * _dot_tn(v[:, ps], k_st[:, ps])
            outs.append(o * lax.rsqrt(_dot2_rhs(o * o, grp) + EPS) * gn)
        o_ref[b, rows, :] = (jnp.concatenate(outs, axis=-1) * gate).astype(o_ref.dtype)


def _hgrn2(proj, lb, out_norm, batch, seq):
    n = proj.shape[0]
    width = lb.shape[0]
    n_heads = width // HEAD_DIM
    rb = HGRN_ROWS
    spb = seq // rb
    dst, masks = _hgrn_constants()
    col = lambda j: pl.BlockSpec((batch, rb, width), lambda s, j=j: (0, s, j))
    full = lambda a: pl.BlockSpec(a.shape, lambda s: (0,) * a.ndim)
    lb2 = lb.reshape(1, width)
    gn = jnp.tile(out_norm, LANES // HEAD_DIM).reshape(1, LANES)
    dst = jnp.asarray(np.concatenate([dst, dst, dst], axis=1), BF16)
    masks = jnp.asarray(np.concatenate([masks, masks], axis=1), F32)
    lane = np.arange(LANES)
    bd = jnp.asarray((lane[:, None] // HEAD_DIM) == (lane[None, :] // HEAD_DIM), F32)
    grp = _group_mean_matrix()
    proj3 = proj.reshape(batch, seq, proj.shape[1])
    out = pl.pallas_call(
        _hgrn_kernel,
        grid=(spb,),
        in_specs=[col(0), col(1), col(2), col(3), full(lb2), full(gn), full(dst), full(masks),
                  full(bd), full(grp)],
        out_specs=pl.BlockSpec((batch, rb, width), lambda s: (0, s, 0)),
        out_shape=jax.ShapeDtypeStruct((batch, seq, width), BF16),
        scratch_shapes=[pltpu.VMEM((batch, n_heads // 2, LANES, LANES), F32)],
        compiler_params=pltpu.CompilerParams(
            dimension_semantics=("arbitrary",), vmem_limit_bytes=VMEM_LIMIT),
    )(proj3, proj3, proj3, proj3, lb2, gn, dst, masks, bd, grp)
    return out.reshape(n, width)


def _fox_prep_kernel(q_ref, k_ref, v_ref, gate_ref, bias_ref, gq_ref, gk_ref, tril_ref,
                     sq_ref, sk_ref, cq_ref, ck_ref, cv_ref, grp_ref,
                     qa_ref, ka_ref, va_ref, carry_ref):
    n_heads = qa_ref.shape[1]
    tm = q_ref.shape[0]

    @pl.when(pl.program_id(1) == 0)
    def _():
        carry_ref[...] = jnp.zeros_like(carry_ref)

    z = gate_ref[...] + bias_ref[...]
    ls = -(jnp.maximum(-z, 0.0) + jnp.log(1.0 + jnp.exp(-jnp.abs(z))))
    cum = _dot3(tril_ref[...], ls) + carry_ref[...]
    carry_ref[...] = cum[tm - 1:tm]
    cum = cum * LOG2E

    c3 = jnp.concatenate(_split3(cum), axis=-1)
    ext_q = jnp.dot(c3, sq_ref[...], preferred_element_type=F32) + cq_ref[...]
    ext_k = jnp.dot(c3, sk_ref[...], preferred_element_type=F32) + ck_ref[...]

    lane = lax.broadcasted_iota(jnp.int32, (tm, LANES), 1)
    low_half = lane < HEAD_DIM
    grp = grp_ref[...]
    scale = HEAD_DIM ** -0.5 * LOG2E
    for c in range(n_heads // 2):
        cols = slice(c * LANES, (c + 1) * LANES)
        q = q_ref[:, cols].astype(F32)
        k = k_ref[:, cols].astype(F32)
        v = v_ref[:, cols]
        qn = q * lax.rsqrt(_dot2_rhs(q * q, grp) + EPS) * gq_ref[...] * scale
        kn = k * lax.rsqrt(_dot2_rhs(k * k, grp) + EPS) * gk_ref[...]
        for par in range(2):
            h = 2 * c + par
            data = low_half if par == 0 else jnp.logical_not(low_half)
            ext = slice(h * LANES, (h + 1) * LANES)
            qa_ref[0, h] = jnp.where(data, qn, ext_q[:, ext]).astype(BF16)
            ka_ref[0, h] = jnp.where(data, kn, ext_k[:, ext]).astype(BF16)
            va_ref[0, h] = jnp.where(data, v, cv_ref[par:par + 1, :].astype(BF16))


def _fox_layout_constants(n_heads):
    sq = np.zeros((3 * LANES, n_heads * LANES), np.float32)
    sk = np.zeros((3 * LANES, n_heads * LANES), np.float32)
    cq = np.zeros((1, n_heads * LANES), np.float32)
    ck = np.zeros((1, n_heads * LANES), np.float32)
    cv = np.zeros((2, LANES), np.float32)
    for h in range(n_heads):
        x0 = h * LANES + (HEAD_DIM if h % 2 == 0 else 0)
        for t in range(3):
            sq[t * LANES + h, x0 + t] = 1.0
            sk[t * LANES + h, x0 + 3 + t] = -1.0
        cq[0, x0 + 3:x0 + 6] = 1.0
        ck[0, x0:x0 + 3] = 1.0
    cv[0, HEAD_DIM] = 1.0
    cv[1, 0] = 1.0
    return (jnp.asarray(sq, BF16), jnp.asarray(sk, BF16), jnp.asarray(cq), jnp.asarray(ck), jnp.asarray(cv))


def _group_mean_matrix():
    lane = np.arange(LANES)
    return jnp.asarray(((lane[:, None] // HEAD_DIM) == (lane[None, :] // HEAD_DIM)) / HEAD_DIM, BF16)


def _fox_prep(proj, gates, f_bias, q_norm, k_norm, batch, seq, col0):
    width = 512
    n_heads = width // HEAD_DIM
    tm = PREP_ROWS
    spb = seq // tm
    col = lambda j: pl.BlockSpec((tm, width), lambda b, s, j=j: (b * spb + s, col0 + j))
    full = lambda a: pl.BlockSpec(a.shape, lambda b, s: (0,) * a.ndim)
    bias = jnp.zeros((1, LANES), F32).at[0, :n_heads].set(f_bias)
    gq = jnp.tile(q_norm, LANES // HEAD_DIM).reshape(1, LANES)
    gk = jnp.tile(k_norm, LANES // HEAD_DIM).reshape(1, LANES)
    tril = jnp.asarray(np.tril(np.ones((tm, tm), np.float32)), BF16)
    consts = _fox_layout_constants(n_heads) + (_group_mean_matrix(),)
    out = jax.ShapeDtypeStruct((batch, n_heads, seq, LANES), BF16)
    ospec = pl.BlockSpec((1, n_heads, tm, LANES), lambda b, s: (b, 0, s, 0))
    return pl.pallas_call(
        _fox_prep_kernel,
        grid=(batch, spb),
        in_specs=[col(0), col(1), col(2),
                  pl.BlockSpec((tm, LANES), lambda b, s: (b * spb + s, 0)),
                  full(bias), full(gq), full(gk), full(tril)] + [full(a) for a in consts],
        out_specs=[ospec, ospec, ospec],
        out_shape=[out, out, out],
        scratch_shapes=[pltpu.VMEM((1, LANES), F32)],
        compiler_params=pltpu.CompilerParams(
            dimension_semantics=("parallel", "arbitrary"), vmem_limit_bytes=VMEM_LIMIT),
    )(proj, proj, proj, gates, bias, gq, gk, tril, *consts)


def _tri_tables(nq):
    qi = [q for q in range(nq) for _ in range(q + 1)]
    ki = [k for q in range(nq) for k in range(q + 1)]
    return jnp.asarray(qi, jnp.int32), jnp.asarray(ki, jnp.int32)


def _fox_attn_kernel(qt_ref, kt_ref, q_ref, k_ref, v_ref, o_ref, m_ref, acc_ref):
    p_idx = pl.program_id(2)
    qi = qt_ref[p_idx]
    ki = kt_ref[p_idx]
    hp = q_ref.shape[1]
    t = q_ref.shape[2]

    @pl.when(ki == 0)
    def _():
        m_ref[...] = jnp.full_like(m_ref, -jnp.inf)
        acc_ref[...] = jnp.zeros_like(acc_ref)

    def step(masked):
        for h in range(hp):
            s = _dot_nt(q_ref[0, h], k_ref[0, h])
            if masked:
                row = lax.broadcasted_iota(jnp.int32, (t, t), 0)
                colm = lax.broadcasted_iota(jnp.int32, (t, t), 1)
                s = jnp.where(colm <= row, s, -jnp.inf)
            m_old = m_ref[h]
            m_new = jnp.maximum(m_old, jnp.max(s, axis=-1, keepdims=True))
            p = jnp.exp2(s - m_new)
            acc_ref[h] = (jnp.exp2(m_old - m_new) * acc_ref[h]
                          + jnp.dot(p.astype(BF16), v_ref[0, h], preferred_element_type=F32))
            m_ref[h] = m_new

    @pl.when(ki < qi)
    def _():
        step(False)

    @pl.when(ki == qi)
    def _():
        step(True)
        _fox_finalize(acc_ref, o_ref)


def _fox_finalize(acc_ref, o_ref):
    a0 = acc_ref[0]
    a1 = acc_ref[1]
    lane = lax.broadcasted_iota(jnp.int32, a0.shape, 1)
    o_ref[0] = jnp.where(lane < HEAD_DIM, a0 / a0[:, HEAD_DIM:HEAD_DIM + 1], a1 / a1[:, 0:1]).astype(o_ref.dtype)


def _tile_plan(n_sub, diagonal):
    plan = []
    for qb in range(n_sub):
        if not diagonal:
            plan.append((qb, 0, n_sub, False))
        else:
            if qb > 0:
                plan.append((qb, 0, qb, False))
            plan.append((qb, qb, qb + 1, True))
    return plan


def _fox_fast_kernel(qt_ref, kt_ref, q_ref, k_ref, v_ref, o_ref, acc_ref):
    p_idx = pl.program_id(2)
    qi = qt_ref[p_idx]
    ki = kt_ref[p_idx]
    hp = q_ref.shape[1]
    sb = ATTN_FAST_SUB
    n_sub = q_ref.shape[2] // sb

    @pl.when(ki == 0)
    def _():
        acc_ref[...] = jnp.zeros_like(acc_ref)

    def tile(diagonal):
        for h in range(hp):
            for qb, k0, k1, masked in _tile_plan(n_sub, diagonal):
                rows = pl.ds(qb * sb, sb)
                cols = pl.ds(k0 * sb, (k1 - k0) * sb)
                s = _dot_nt(q_ref[0, h, rows, :], k_ref[0, h, cols, :])
                if masked:
                    row = lax.broadcasted_iota(jnp.int32, (sb, sb), 0)
                    colm = lax.broadcasted_iota(jnp.int32, (sb, sb), 1)
                    s = jnp.where(colm <= row, s, -jnp.inf)
                p = jnp.exp2(s).astype(BF16)
                acc_ref[h, rows, :] += jnp.dot(p, v_ref[0, h, cols, :], preferred_element_type=F32)

    @pl.when(ki < qi)
    def _():
        tile(False)

    @pl.when(ki == qi)
    def _():
        tile(True)
        _fox_finalize(acc_ref, o_ref)


def _fox_attention(qa, ka, va, fast):
    batch, n_heads, seq, _ = qa.shape
    t = ATTN_FAST_TILE if fast else ATTN_TILE
    hp = 2
    nq = seq // t
    qt, kt = _tri_tables(nq)
    qspec = pl.BlockSpec((1, hp, t, LANES), lambda b, g, p, qt, kt: (b, g, qt[p], 0))
    kspec = pl.BlockSpec((1, hp, t, LANES), lambda b, g, p, qt, kt: (b, g, kt[p], 0))
    scratch = [pltpu.VMEM((hp, t, LANES), F32)]
    if not fast:
        scratch = [pltpu.VMEM((hp, t, 1), F32)] + scratch
    return pl.pallas_call(
        _fox_fast_kernel if fast else _fox_attn_kernel,
        grid_spec=pltpu.PrefetchScalarGridSpec(
            num_scalar_prefetch=2,
            grid=(batch, n_heads // hp, int(qt.shape[0])),
            in_specs=[qspec, kspec, kspec],
            out_specs=pl.BlockSpec((1, t, hp * HEAD_DIM), lambda b, g, p, qt, kt: (b, qt[p], g)),
            scratch_shapes=scratch),
        out_shape=jax.ShapeDtypeStruct((batch, seq, n_heads * HEAD_DIM), BF16),
        compiler_params=pltpu.CompilerParams(
            dimension_semantics=("parallel", "parallel", "arbitrary"), vmem_limit_bytes=VMEM_LIMIT),
    )(qt, kt, qa, ka, va)


def _logit_bound(q_gain, k_gain):
    return HEAD_DIM ** 0.5 * jnp.max(jnp.abs(q_gain)) * jnp.max(jnp.abs(k_gain))


def _diff_prep_kernel(q_ref, k_ref, v_ref, pos_ref, invf_ref, gq_ref, gk_ref, grp_ref, sel_ref,
                      qm_ref, k2_ref, va_ref, cs_ref, sn_ref):
    n_heads = k2_ref.shape[1]
    tm = q_ref.shape[0]
    ang = pos_ref[...].astype(F32) * invf_ref[...]
    cs_c = jnp.cos(ang)
    sn_c = jnp.sin(ang)
    per_row = LANES // (HEAD_DIM // 2)
    for j in range(per_row):
        cs_ref[pl.ds(j, tm // per_row, stride=per_row), :] = _dot3_rhs(cs_c, sel_ref[j])
        sn_ref[pl.ds(j, tm // per_row, stride=per_row), :] = _dot3_rhs(sn_c, sel_ref[j])
    lane = lax.broadcasted_iota(jnp.int32, (tm, LANES), 1)
    first = (lane % HEAD_DIM) < (HEAD_DIM // 2)
    cs = cs_ref[...]
    sn = sn_ref[...]
    sn = jnp.where(first, -sn, sn)
    grp = grp_ref[...]
    scale = HEAD_DIM ** -0.5 * LOG2E
    zero = jnp.zeros((tm, LANES), F32)
    onecol = jnp.where(lane == 0, 1.0, 0.0).astype(BF16)

    def norm_rope(x, gain):
        ms = _dot2_rhs(x * x, grp)
        y = x * lax.rsqrt(ms + EPS) * gain
        yr = jnp.where(first, pltpu.roll(y, LANES - HEAD_DIM // 2, 1), pltpu.roll(y, HEAD_DIM // 2, 1))
        return y * cs + yr * sn

    for h in range(n_heads):
        cols = slice(h * LANES, (h + 1) * LANES)
        qr = norm_rope(q_ref[:, cols].astype(F32), gq_ref[...]) * scale
        kr = norm_rope(k_ref[:, cols].astype(F32), gk_ref[...])
        qm_ref[0, h, 0] = jnp.where(lane < HEAD_DIM, qr, zero).astype(BF16)
        qm_ref[0, h, 1] = jnp.where(lane < HEAD_DIM, zero, qr).astype(BF16)
        k2_ref[0, h] = kr.astype(BF16)
        va_ref[0, h] = jnp.concatenate([v_ref[:, cols].astype(BF16), onecol], axis=-1)


def _diff_prep(proj, positions, q_norm, k_norm, batch, seq):
    n = proj.shape[0]
    width = proj.shape[1] // 3
    n_heads = width // LANES
    tm = PREP_ROWS
    spb = seq // tm
    col = lambda j: pl.BlockSpec((tm, width), lambda b, s, j=j: (b * spb + s, j))
    full = lambda a: pl.BlockSpec(a.shape, lambda b, s: (0,) * a.ndim)
    half = HEAD_DIM // 2
    inv_freq = ROPE_THETA ** (-jnp.arange(half, dtype=F32) / half)
    invf = jnp.tile(inv_freq, LANES // half).reshape(1, LANES)
    gq = jnp.tile(q_norm, LANES // HEAD_DIM).reshape(1, LANES)
    gk = jnp.tile(k_norm, LANES // HEAD_DIM).reshape(1, LANES)
    grp = _group_mean_matrix()
    per_row = LANES // half
    pos = jnp.repeat(positions.reshape(n // per_row, per_row).astype(jnp.int32), half, axis=1)
    lane = np.arange(LANES)
    sel = jnp.asarray(np.stack([(lane[:, None] == j * half + lane[None, :] % half) for j in range(per_row)]),
                      BF16)
    return pl.pallas_call(
        _diff_prep_kernel,
        grid=(batch, spb),
        in_specs=[col(0), col(1), col(2),
                  pl.BlockSpec((tm // per_row, LANES), lambda b, s: (b * spb + s, 0)),
                  full(invf), full(gq), full(gk), full(grp), full(sel)],
        out_specs=[pl.BlockSpec((1, n_heads, 2, tm, LANES), lambda b, s: (b, 0, 0, s, 0)),
                   pl.BlockSpec((1, n_heads, tm, LANES), lambda b, s: (b, 0, s, 0)),
                   pl.BlockSpec((1, n_heads, tm, 2 * LANES), lambda b, s: (b, 0, s, 0))],
        out_shape=[jax.ShapeDtypeStruct((batch, n_heads, 2, seq, LANES), BF16),
                   jax.ShapeDtypeStruct((batch, n_heads, seq, LANES), BF16),
                   jax.ShapeDtypeStruct((batch, n_heads, seq, 2 * LANES), BF16)],
        scratch_shapes=[pltpu.VMEM((tm, LANES), F32), pltpu.VMEM((tm, LANES), F32)],
        compiler_params=pltpu.CompilerParams(
            dimension_semantics=("parallel", "parallel"), vmem_limit_bytes=VMEM_LIMIT),
    )(proj, proj, proj, pos, invf, gq, gk, grp, sel)


def _diff_attn_kernel(lambda_init, qt_ref, kt_ref, q_ref, k_ref, v_ref, lam_ref, sub_ref,
                      o_ref, m_ref, acc_ref):
    p_idx = pl.program_id(2)
    qi = qt_ref[p_idx]
    ki = kt_ref[p_idx]
    t = k_ref.shape[2]
    dv = o_ref.shape[2]

    @pl.when(ki == 0)
    def _():
        m_ref[...] = jnp.full_like(m_ref, -jnp.inf)
        acc_ref[...] = jnp.zeros_like(acc_ref)

    def step(masked):
        for m in range(2):
            s = _dot_nt(q_ref[0, 0, m], k_ref[0, 0])
            if masked:
                row = lax.broadcasted_iota(jnp.int32, (t, t), 0) // CHUNK
                colm = lax.broadcasted_iota(jnp.int32, (t, t), 1) // CHUNK
                s = jnp.where(colm <= row, s, -jnp.inf)
            m_old = m_ref[m]
            m_new = jnp.maximum(m_old, jnp.max(s, axis=-1, keepdims=True))
            p = jnp.exp2(s - m_new)
            acc_ref[m] = (jnp.exp2(m_old - m_new) * acc_ref[m]
                          + jnp.dot(p.astype(BF16), v_ref[0, 0], preferred_element_type=F32))
            m_ref[m] = m_new

    @pl.when(ki < qi)
    def _():
        step(False)

    @pl.when(ki == qi)
    def _():
        step(True)
        _diff_finalize(lambda_init, acc_ref, lam_ref, sub_ref, o_ref)


def _diff_finalize(lambda_init, acc_ref, lam_ref, sub_ref, o_ref):
    dv = o_ref.shape[2]
    lp = lam_ref[...]
    lam = (jnp.exp(jnp.sum(lp[0:1] * lp[1:2], axis=-1, keepdims=True))
           - jnp.exp(jnp.sum(lp[2:3] * lp[3:4], axis=-1, keepdims=True)) + lambda_init)
    a0 = acc_ref[0]
    a1 = acc_ref[1]
    o = a0[:, :dv] / a0[:, dv:dv + 1] - lam * (a1[:, :dv] / a1[:, dv:dv + 1])
    ms = jnp.mean(o * o, axis=-1, keepdims=True)
    o_ref[0] = ((o * lax.rsqrt(ms + EPS) * sub_ref[...]) * (1.0 - lambda_init)).astype(o_ref.dtype)


def _diff_fast_kernel(lambda_init, qt_ref, kt_ref, q_ref, k_ref, v_ref, lam_ref, sub_ref, o_ref, acc_ref):
    p_idx = pl.program_id(2)
    qi = qt_ref[p_idx]
    ki = kt_ref[p_idx]
    sb = ATTN_FAST_SUB
    n_sub = k_ref.shape[2] // sb

    @pl.when(ki == 0)
    def _():
        acc_ref[...] = jnp.zeros_like(acc_ref)

    def tile(diagonal):
        for m in range(2):
            for qb, k0, k1, masked in _tile_plan(n_sub, diagonal):
                rows = pl.ds(qb * sb, sb)
                cols = pl.ds(k0 * sb, (k1 - k0) * sb)
                s = _dot_nt(q_ref[0, 0, m, rows, :], k_ref[0, 0, cols, :])
                if masked:
                    row = lax.broadcasted_iota(jnp.int32, (sb, sb), 0) // CHUNK
                    colm = lax.broadcasted_iota(jnp.int32, (sb, sb), 1) // CHUNK
                    s = jnp.where(colm <= row, s, -jnp.inf)
                p = jnp.exp2(s).astype(BF16)
                acc_ref[m, rows, :] += jnp.dot(p, v_ref[0, 0, cols, :], preferred_element_type=F32)

    @pl.when(ki < qi)
    def _():
        tile(False)

    @pl.when(ki == qi)
    def _():
        tile(True)
        _diff_finalize(lambda_init, acc_ref, lam_ref, sub_ref, o_ref)


def _diff_attention(qm, k2, va, lam_params, subln, lambda_init, fast):
    batch, n_heads, seq, _ = k2.shape
    dv = va.shape[3] // 2
    t = ATTN_FAST_TILE if fast else ATTN_TILE
    nq = seq // t
    qt, kt = _tri_tables(nq)
    lamp = jnp.zeros((8, LANES), F32).at[:4, :HEAD_DIM].set(lam_params)
    sub = subln.reshape(1, dv)
    scratch = [pltpu.VMEM((2, t, 2 * dv), F32)]
    if not fast:
        scratch = [pltpu.VMEM((2, t, 1), F32)] + scratch
    return pl.pallas_call(
        functools.partial(_diff_fast_kernel if fast else _diff_attn_kernel, lambda_init),
        grid_spec=pltpu.PrefetchScalarGridSpec(
            num_scalar_prefetch=2,
            grid=(batch, n_heads, int(qt.shape[0])),
            in_specs=[pl.BlockSpec((1, 1, 2, t, LANES), lambda b, h, p, qt, kt: (b, h, 0, qt[p], 0)),
                      pl.BlockSpec((1, 1, t, LANES), lambda b, h, p, qt, kt: (b, h, kt[p], 0)),
                      pl.BlockSpec((1, 1, t, 2 * dv), lambda b, h, p, qt, kt: (b, h, kt[p], 0)),
                      pl.BlockSpec((8, LANES), lambda b, h, p, qt, kt: (0, 0)),
                      pl.BlockSpec((1, dv), lambda b, h, p, qt, kt: (0, 0))],
            out_specs=pl.BlockSpec((1, t, dv), lambda b, h, p, qt, kt: (b, qt[p], h)),
            scratch_shapes=scratch),
        out_shape=jax.ShapeDtypeStruct((batch, seq, n_heads * dv), BF16),
        compiler_params=pltpu.CompilerParams(
            dimension_semantics=("parallel", "parallel", "arbitrary"), vmem_limit_bytes=VMEM_LIMIT),
    )(qt, kt, qm, k2, va, lamp, sub)


def _store_token_tiles(ref, value):
    t, width = value.shape
    s = width // LANES
    for j in range(s):
        ref[pl.ds(j, t, stride=s), :] = value[:, j * LANES:(j + 1) * LANES]


def _load_token_tiles(ref, first_row, t, s):
    return jnp.concatenate([ref[pl.ds(first_row + j, t, stride=s), :] for j in range(s)], axis=-1)


def _router_kernel(h_ref, g_ref, whi_ref, wlo_ref, tri_ref, xn_ref, route_ref, cnt_ref, run_ref):
    @pl.when(pl.program_id(0) == 0)
    def _():
        run_ref[...] = jnp.zeros_like(run_ref)

    x = h_ref[...]
    tm = x.shape[0]
    xn = x * lax.rsqrt(jnp.mean(x * x, axis=-1, keepdims=True) + EPS) * g_ref[...]
    _store_token_tiles(xn_ref, xn)
    xh = xn.astype(BF16)
    xl = (xn - xh.astype(F32)).astype(BF16)
    d = lambda a, b: jnp.dot(a, b[...], preferred_element_type=F32)
    logits = d(xh, whi_ref) + (d(xl, whi_ref) + d(xh, wlo_ref))
    lane = lax.broadcasted_iota(jnp.int32, (tm, LANES), 1)
    neg = jnp.full((tm, LANES), -jnp.inf, F32)
    big = jnp.full((tm, LANES), LANES, jnp.int32)

    def top1(vals):
        m = jnp.max(vals, axis=-1, keepdims=True)
        idx = jnp.min(jnp.where(vals == m, lane, big), axis=-1, keepdims=True)
        return m, idx

    grp_logits = jnp.where(lane < N_GROUPS, logits, neg)
    mg, gidx = top1(grp_logits)
    p_g = 1.0 / jnp.sum(jnp.exp(grp_logits - mg), axis=-1, keepdims=True)
    e_lane = lane - N_GROUPS
    in_grp = (e_lane >= gidx * EXPERTS_PER_GROUP) & (e_lane < (gidx + 1) * EXPERTS_PER_GROUP)
    sel = jnp.where(in_grp, logits, neg)
    m1, i1 = top1(sel)
    m2, i2 = top1(jnp.where(lane == i1, neg, sel))
    r = jnp.exp(m2 - m1)
    w1 = p_g / (1.0 + r)
    w2 = p_g * r / (1.0 + r)
    zero = jnp.zeros((tm, LANES), F32)
    chosen = jnp.where((lane == i1) | (lane == i2), 1.0, 0.0)
    before = jnp.dot(tri_ref[...], chosen.astype(BF16), preferred_element_type=F32) + run_ref[...]
    rank1 = jnp.sum(jnp.where(lane == i1, before, zero), axis=-1, keepdims=True)
    rank2 = jnp.sum(jnp.where(lane == i2, before, zero), axis=-1, keepdims=True)
    run = run_ref[...] + jnp.sum(chosen, axis=0, keepdims=True)
    run_ref[...] = run
    cnt_ref[...] = jnp.broadcast_to(run, cnt_ref.shape)
    route_ref[...] = jnp.where(lane == 0, (i1 - N_GROUPS).astype(F32),
                     jnp.where(lane == 1, (i2 - N_GROUPS).astype(F32),
                     jnp.where(lane == 2, w1, jnp.where(lane == 3, w2,
                     jnp.where(lane == 4, rank1, jnp.where(lane == 5, rank2, zero))))))


def _router(h, gain, w_group, w_expert):
    n, d = h.shape
    tm = ROUTER_ROWS
    wr = jnp.zeros((d, LANES), F32)
    wr = wr.at[:, :N_GROUPS].set(w_group)
    wr = wr.at[:, N_GROUPS:N_GROUPS + N_EXPERTS].set(
        jnp.transpose(w_expert, (1, 0, 2)).reshape(d, N_EXPERTS))
    w_hi = wr.astype(BF16)
    w_lo = (wr - w_hi.astype(F32)).astype(BF16)
    tri = jnp.asarray(np.tril(np.ones((tm, tm), np.float32), -1), BF16)
    xn, route, counts = pl.pallas_call(
        _router_kernel,
        grid=(n // tm,),
        in_specs=[pl.BlockSpec((tm, d), lambda i: (i, 0)),
                  pl.BlockSpec((1, d), lambda i: (0, 0)),
                  pl.BlockSpec((d, LANES), lambda i: (0, 0)),
                  pl.BlockSpec((d, LANES), lambda i: (0, 0)),
                  pl.BlockSpec((tm, tm), lambda i: (0, 0))],
        out_specs=[pl.BlockSpec((tm * d // LANES, LANES), lambda i: (i, 0)),
                   pl.BlockSpec((tm, LANES), lambda i: (i, 0)),
                   pl.BlockSpec((TILE_ROWS, LANES), lambda i: (0, 0))],
        out_shape=[jax.ShapeDtypeStruct((n * d // LANES, LANES), F32),
                   jax.ShapeDtypeStruct((n, LANES), F32),
                   jax.ShapeDtypeStruct((TILE_ROWS, LANES), F32)],
        scratch_shapes=[pltpu.VMEM((1, LANES), F32)],
        compiler_params=pltpu.CompilerParams(
            dimension_semantics=("arbitrary",), vmem_limit_bytes=VMEM_LIMIT),
    )(h, gain.reshape(1, d), w_hi, w_lo, tri)
    return xn, route, counts[0, N_GROUPS:N_GROUPS + N_EXPERTS].astype(jnp.int32)


def _dispatch_tables(expert_ids, rank, counts, tm):
    n = expert_ids.shape[0]
    n_tiles = 2 * n // tm + N_EXPERTS
    padded = ((counts + tm - 1) // tm) * tm
    ends = jnp.cumsum(padded)
    starts = ends - padded
    assert 2 * n <= 1 << PAIR_RANK_BITS
    code = expert_ids * (1 << PAIR_RANK_BITS) + rank
    tile_start = jnp.arange(n_tiles, dtype=jnp.int32) * tm
    tile_expert = jnp.minimum(jnp.sum(tile_start[:, None] >= ends[None, :], axis=1),
                              N_EXPERTS - 1).astype(jnp.int32)
    n_valid = (ends[-1] // tm).astype(jnp.int32).reshape(1)
    pad_start = (starts + counts).astype(jnp.int32)
    pad_count = (padded - counts).astype(jnp.int32)
    return (code.astype(jnp.int32), starts.astype(jnp.int32), tile_expert, n_valid, n_tiles,
            pad_start, pad_count)


PAIR_RANK_BITS = 16


def _pair_slot(starts_ref, code):
    return starts_ref[code >> PAIR_RANK_BITS] + (code & ((1 << PAIR_RANK_BITS) - 1))


def _token_copy(src_hbm, tok, dst_ref, r, sem):
    src = src_hbm.at[pl.ds(pl.multiple_of(tok * TILE_ROWS, TILE_ROWS), TILE_ROWS)]
    first = r * TILE_ROWS if isinstance(r, int) else pl.multiple_of(r * TILE_ROWS, TILE_ROWS)
    return pltpu.make_async_copy(src, dst_ref.at[pl.ds(first, TILE_ROWS)], sem)


def _gather_tokens(src_hbm, starts_ref, code_ref, dst_ref, sem, n_tokens):
    def body(r, carry):
        _token_copy(src_hbm, _pair_slot(starts_ref, code_ref[0, 0, r]), dst_ref, r, sem).start()
        return carry
    lax.fori_loop(0, n_tokens, body, 0, unroll=8)


def _wait_tokens(src_hbm, dst_ref, sem):
    pltpu.make_async_copy(src_hbm.at[pl.ds(0, dst_ref.shape[0])], dst_ref, sem).wait()


def _dispatch_kernel(st_ref, ps_ref, pc_ref, nv_ref, pos_ref, x_ref, xs_hbm, zero_blk, sem, pad_sem):
    i = pl.program_id(0)
    tokens = pos_ref.shape[2] // 2

    def slot_tile(slot):
        return xs_hbm.at[pl.ds(pl.multiple_of(slot * TILE_ROWS, TILE_ROWS), TILE_ROWS)]

    for r in range(2 * tokens):
        src = x_ref.at[pl.ds((r % tokens) * TILE_ROWS, TILE_ROWS)]
        pltpu.make_async_copy(src, slot_tile(_pair_slot(st_ref, pos_ref[0, 0, r])), sem).start(priority=r % 2)
    rows = 2 * tokens * TILE_ROWS
    pltpu.make_async_copy(xs_hbm.at[pl.ds(0, rows)], xs_hbm.at[pl.ds(0, rows)], sem).wait()

    @pl.when(i == pl.num_programs(0) - 1)
    def _():
        zero_blk[...] = jnp.zeros_like(zero_blk)
        tile_slots = zero_blk.shape[0] // TILE_ROWS
        for e in range(ps_ref.shape[0]):
            first = ps_ref[e]
            count = pc_ref[e]
            for wait in (False, True):
                for bit in range(tile_slots.bit_length() - 1):
                    chunk = (1 << bit) * TILE_ROWS
                    start = (first + (count & ((1 << bit) - 1))) * TILE_ROWS
                    copy = pltpu.make_async_copy(
                        zero_blk.at[pl.ds(0, chunk)],
                        xs_hbm.at[pl.ds(pl.multiple_of(start, TILE_ROWS), chunk)], pad_sem)

                    @pl.when(((count >> bit) & 1) == 1)
                    def _(copy=copy, wait=wait):
                        copy.wait() if wait else copy.start()

        block_rows = zero_blk.shape[0]
        n_blocks = xs_hbm.shape[0] // block_rows

        def block(t):
            return xs_hbm.at[pl.ds(pl.multiple_of(t * block_rows, block_rows), block_rows)]

        def fill_block(t, carry):
            pltpu.make_async_copy(zero_blk, block(t), pad_sem).start()
            return carry

        def drain_block(t, carry):
            pltpu.make_async_copy(zero_blk, block(t), pad_sem).wait()
            return carry

        lax.fori_loop(nv_ref[0], n_blocks, fill_block, 0)
        lax.fori_loop(nv_ref[0], n_blocks, drain_block, 0)


def _pair_table(code, rows):
    steps = code.shape[0] // rows
    return jnp.transpose(code.reshape(steps, rows, 2), (0, 2, 1)).reshape(steps, 1, 2 * rows)


def _moe_dispatch(xn, pair_tab, starts, pad_start, pad_count, n_valid, n_tiles, tm):
    steps = pair_tab.shape[0]
    tb = pair_tab.shape[2] // 2
    return pl.pallas_call(
        _dispatch_kernel,
        grid_spec=pltpu.PrefetchScalarGridSpec(
            num_scalar_prefetch=4,
            grid=(steps,),
            in_specs=[pl.BlockSpec((1, 1, 2 * tb), lambda i, *_: (i, 0, 0), memory_space=pltpu.SMEM),
                      pl.BlockSpec((tb * TILE_ROWS, LANES), lambda i, *_: (i, 0))],
            out_specs=pl.BlockSpec(memory_space=pl.ANY),
            scratch_shapes=[pltpu.VMEM((tm * TILE_ROWS, LANES), F32),
                            pltpu.SemaphoreType.DMA, pltpu.SemaphoreType.DMA]),
        out_shape=jax.ShapeDtypeStruct((n_tiles * tm * TILE_ROWS, LANES), F32),
        compiler_params=pltpu.CompilerParams(
            dimension_semantics=("arbitrary",), vmem_limit_bytes=VMEM_LIMIT),
    )(starts, pad_start, pad_count, n_valid, pair_tab, xn)


def _moe_kernel(te_ref, nv_ref, x_ref, wg_ref, wu_ref, wd_ref, o_ref, wg_b, wu_b, wd_b):
    i = pl.program_id(0)
    tm = x_ref.shape[0] // TILE_ROWS
    n_valid = nv_ref[0]
    new_expert = jnp.logical_or(i == 0, te_ref[i] != te_ref[jnp.maximum(i - 1, 0)])

    @pl.when(jnp.logical_and(i < n_valid, new_expert))
    def _():
        wg_b[...] = wg_ref[0].astype(BF16)
        wu_b[...] = wu_ref[0].astype(BF16)
        wd_b[...] = wd_ref[0].astype(BF16)

    @pl.when(i < n_valid)
    def _():
        x = _load_token_tiles(x_ref, 0, tm, TILE_ROWS).astype(BF16)
        g = jnp.dot(x, wg_b[...], preferred_element_type=F32)
        u = jnp.dot(x, wu_b[...], preferred_element_type=F32)
        hid = (g * _sigmoid(g) * u).astype(BF16)
        _store_token_tiles(o_ref, jnp.dot(hid, wd_b[...], preferred_element_type=F32))

    @pl.when(i >= n_valid)
    def _():
        o_ref[...] = jnp.zeros_like(o_ref)


def _moe_experts(xs, tile_expert, n_valid, tm, w_gate, w_up, w_down):
    n_exp, d, f = w_gate.shape
    assert d == TILE_ROWS * LANES
    n_tiles = xs.shape[0] // (tm * TILE_ROWS)
    x_index = lambda i, te, nv: (jnp.minimum(i, nv[0] - 1), 0)
    return pl.pallas_call(
        _moe_kernel,
        grid_spec=pltpu.PrefetchScalarGridSpec(
            num_scalar_prefetch=2,
            grid=(n_tiles,),
            in_specs=[
                pl.BlockSpec((tm * TILE_ROWS, LANES), x_index),
                pl.BlockSpec((1, d, f), lambda i, te, nv: (te[i], 0, 0)),
                pl.BlockSpec((1, d, f), lambda i, te, nv: (te[i], 0, 0)),
                pl.BlockSpec((1, f, d), lambda i, te, nv: (te[i], 0, 0))],
            out_specs=pl.BlockSpec((tm * TILE_ROWS, LANES), lambda i, te, nv: (i, 0)),
            scratch_shapes=[pltpu.VMEM((d, f), BF16), pltpu.VMEM((d, f), BF16), pltpu.VMEM((f, d), BF16)]),
        out_shape=jax.ShapeDtypeStruct((n_tiles * tm * TILE_ROWS, LANES), F32),
        compiler_params=pltpu.CompilerParams(
            dimension_semantics=("arbitrary",), vmem_limit_bytes=VMEM_LIMIT),
    )(tile_expert, n_valid, xs, w_gate, w_up, w_down)


def _combine_kernel(st_ref, pos_ref, pos_next_ref, h_ref, route_ref, y_hbm, o_ref, ybuf, sems):
    i = pl.program_id(0)
    n_steps = pl.num_programs(0)
    tokens = ybuf.shape[1] // TILE_ROWS
    slot = i % 2

    @pl.when(i == 0)
    def _():
        _gather_tokens(y_hbm, st_ref, pos_ref, ybuf.at[0], sems.at[0], tokens)

    @pl.when(i + 1 < n_steps)
    def _():
        for r in range(tokens):
            _token_copy(y_hbm, _pair_slot(st_ref, pos_next_ref[0, 0, r]), ybuf.at[1 - slot], r,
                        sems.at[1 - slot]).start(priority=r % 2)

    _wait_tokens(y_hbm, ybuf.at[slot], sems.at[slot])
    tc = tokens // 2
    w = route_ref[...]
    first = _load_token_tiles(ybuf.at[slot], 0, tc, TILE_ROWS)
    second = _load_token_tiles(ybuf.at[slot], tc * TILE_ROWS, tc, TILE_ROWS)
    o_ref[...] = h_ref[...] + w[:, 2:3] * first + w[:, 3:4] * second


def _moe_combine(h, route, pair_tab, starts, y_sorted):
    n, d = h.shape
    steps = pair_tab.shape[0]
    tc = pair_tab.shape[2] // 2
    return pl.pallas_call(
        _combine_kernel,
        grid_spec=pltpu.PrefetchScalarGridSpec(
            num_scalar_prefetch=1,
            grid=(steps,),
            in_specs=[pl.BlockSpec((1, 1, 2 * tc), lambda i, st: (i, 0, 0), memory_space=pltpu.SMEM),
                      pl.BlockSpec((1, 1, 2 * tc), lambda i, st: (jnp.minimum(i + 1, steps - 1), 0, 0),
                                   memory_space=pltpu.SMEM),
                      pl.BlockSpec((tc, d), lambda i, st: (i, 0)),
                      pl.BlockSpec((tc, LANES), lambda i, st: (i, 0)),
                      pl.BlockSpec(memory_space=pl.ANY)],
            out_specs=pl.BlockSpec((tc, d), lambda i, st: (i, 0)),
            scratch_shapes=[pltpu.VMEM((2, 2 * tc * TILE_ROWS, LANES), F32), pltpu.SemaphoreType.DMA((2,))]),
        out_shape=jax.ShapeDtypeStruct((n, d), F32),
        compiler_params=pltpu.CompilerParams(
            dimension_semantics=("arbitrary",), vmem_limit_bytes=VMEM_LIMIT),
    )(starts, pair_tab, pair_tab, h, route, y_sorted)


def _moe_layer(h, layer, gain, w_group, w_expert, w_gate, w_up, w_down):
    d = h.shape[1]
    f = w_gate.shape[-1]
    xn, route, counts = _router(h, gain, w_group, w_expert)
    expert_ids = route[:, :2].astype(jnp.int32)
    rank = route[:, 4:6].astype(jnp.int32)
    code, starts, tile_expert, n_valid, n_tiles, pad_start, pad_count = _dispatch_tables(
        expert_ids, rank, counts, MOE_TM)
    pair_tab = _pair_table(code, COMBINE_ROWS)
    xs = _moe_dispatch(xn, pair_tab, starts, pad_start, pad_count, n_valid, n_tiles, MOE_TM)
    y_sorted = _moe_experts(xs, tile_expert + layer * N_EXPERTS, n_valid, MOE_TM,
                            w_gate.reshape(-1, d, f), w_up.reshape(-1, d, f), w_down.reshape(-1, f, d))
    return _moe_combine(h, route, pair_tab, starts, y_sorted)


def _even_layer(h, batch, seq, gain, w_in, w_out, lb, f_bias, out_norm, q_norm, k_norm):
    d = h.shape[1]
    n_main = w_in.shape[1] - f_bias.shape[0]
    w_main = w_in[:, :n_main].astype(BF16)
    w_gate = jnp.zeros((d, LANES), F32).at[:, :f_bias.shape[0]].set(w_in[:, n_main:])
    proj, gates = _norm_proj(h, gain, w_main, w_gate)
    o_a = _hgrn2(proj, lb, out_norm, batch, seq)
    qa, ka, va = _fox_prep(proj, gates, f_bias, q_norm, k_norm, batch, seq, col0=4)
    o_b = lax.cond(_logit_bound(q_norm, k_norm) <= LOGIT_BOUND_MAX,
                   functools.partial(_fox_attention, fast=True),
                   functools.partial(_fox_attention, fast=False), qa, ka, va).reshape(batch * seq, -1)
    wo = w_out.astype(BF16)
    ka_dim = o_a.shape[1]
    return _proj_residual([(o_a, wo[:ka_dim]), (o_b, wo[ka_dim:])], h)


def _odd_layer(h, positions, batch, seq, gain, w_in, w_out, q_norm, k_norm, lam_params, subln, lambda_init):
    proj = _norm_proj(h, gain, w_in.astype(BF16))
    qm, k2, va = _diff_prep(proj, positions, q_norm, k_norm, batch, seq)
    attn = lambda fast: functools.partial(_diff_attention, lam_params=lam_params, subln=subln,
                                          lambda_init=lambda_init, fast=fast)
    o = lax.cond(_logit_bound(q_norm, k_norm) <= LOGIT_BOUND_MAX,
                 attn(True), attn(False), qm, k2, va).reshape(batch * seq, -1)
    return _proj_residual([(o, w_out.astype(BF16))], h)


def kernel(x, positions, hgrn_lb_logits, norm_mix, norm_ffn, even_w_in, even_w_out, fox_f_bias,
           hgrn_out_norm, fox_q_norm, fox_k_norm, odd_w_in, odd_w_out, diff_q_norm, diff_k_norm,
           diff_lambda_q1, diff_lambda_k1, diff_lambda_q2, diff_lambda_k2, diff_subln,
           moe_router_group, moe_router_expert, moe_w_gate, moe_w_up, moe_w_down):
    batch, seq, d = x.shape
    depth = norm_mix.shape[0]
    lower_bounds = jnp.cumsum(jax.nn.softmax(hgrn_lb_logits.astype(F32), axis=0), axis=0)
    h = x.reshape(batch * seq, d)
    for layer in range(depth):
        j = layer // 2
        if layer % 2 == 0:
            h = _even_layer(h, batch, seq, norm_mix[layer], even_w_in[j], even_w_out[j], lower_bounds[j],
                            fox_f_bias[j], hgrn_out_norm[j], fox_q_norm[j], fox_k_norm[j])
        else:
            lambda_init = 0.8 - 0.6 * math.exp(-0.3 * layer)
            lam_params = jnp.stack([diff_lambda_q1[j], diff_lambda_k1[j],
                                    diff_lambda_q2[j], diff_lambda_k2[j]]).astype(F32)
            h = _odd_layer(h, positions, batch, seq, norm_mix[layer], odd_w_in[j], odd_w_out[j],
                           diff_q_norm[j], diff_k_norm[j], lam_params, diff_subln[j], lambda_init)
        h = _moe_layer(h, layer, norm_ffn[layer], moe_router_group[layer], moe_router_expert[layer],
                       moe_w_gate, moe_w_up, moe_w_down)
    return h.reshape(batch, seq, d)
```

```python
import functools
import math

import numpy as np
import jax
import jax.numpy as jnp
from jax import lax
from jax.experimental import pallas as pl
from jax.experimental.pallas import tpu as pltpu

F32 = jnp.float32
BF16 = jnp.bfloat16

EPS = 1e-6
ROPE_THETA = 10000.0
CHUNK = 64
HEAD_DIM = 64
N_GROUPS = 4
EXPERTS_PER_GROUP = 8
N_EXPERTS = N_GROUPS * EXPERTS_PER_GROUP
LANES = 128
TILE_ROWS = 8

HGRN_CHUNK = 64
HGRN_ROWS = 256
ATTN_TILE = 512
ATTN_FAST_TILE = 2048
ATTN_FAST_SUB = 512
LOGIT_BOUND_MAX = 60.0
LOG2E = math.log2(math.e)
PREP_ROWS = 256
PROJ_TM = 512
ROUTER_ROWS = 256
MOE_TM = 512
COMBINE_ROWS = 256
VMEM_LIMIT = 56 * 1024 * 1024


def _split3(x):
    hi = x.astype(BF16)
    r1 = x - hi.astype(F32)
    mid = r1.astype(BF16)
    lo = (r1 - mid.astype(F32)).astype(BF16)
    return hi, mid, lo


def _dot3(const_bf16, x):
    hi, mid, lo = _split3(x)
    d = lambda b: jnp.dot(const_bf16, b, preferred_element_type=F32)
    return d(hi) + d(mid) + d(lo)


def _dot3_stacked(const3_bf16, x):
    return jnp.dot(const3_bf16, jnp.concatenate(_split3(x), axis=0), preferred_element_type=F32)


def _dot3_rhs(x, const_bf16):
    hi, mid, lo = _split3(x)
    d = lambda a: jnp.dot(a, const_bf16, preferred_element_type=F32)
    return d(hi) + d(mid) + d(lo)


def _dot2_rhs(x, const_bf16):
    hi = x.astype(BF16)
    lo = (x - hi.astype(F32)).astype(BF16)
    d = lambda a: jnp.dot(a, const_bf16, preferred_element_type=F32)
    return d(hi) + d(lo)


def _dot_nt(a, b):
    return lax.dot_general(a, b, (((1,), (1,)), ((), ())), preferred_element_type=F32)


def _dot_tn(a, b):
    return lax.dot_general(a, b, (((0,), (0,)), ((), ())), preferred_element_type=F32)


def _sigmoid(x):
    return 1.0 / (1.0 + jnp.exp(-x))


def _norm_proj_kernel(has_aux, x_ref, g_ref, w_ref, *rest):
    x = x_ref[...]
    ms = jnp.mean(x * x, axis=-1, keepdims=True)
    xn = x * lax.rsqrt(ms + EPS) * g_ref[...]
    xb = xn.astype(BF16)
    rest[-2 if has_aux else -1][...] = jnp.dot(xb, w_ref[...], preferred_element_type=F32).astype(BF16)
    if has_aux:
        whi_ref, wlo_ref, _, oaux_ref = rest
        xl = (xn - xb.astype(F32)).astype(BF16)
        d = lambda a, b: jnp.dot(a, b[...], preferred_element_type=F32)
        oaux_ref[...] = d(xb, whi_ref) + (d(xl, whi_ref) + d(xb, wlo_ref))


def _norm_proj(x, gain, w, w_aux=None):
    n, d = x.shape
    m = w.shape[1]
    tm = PROJ_TM
    has_aux = w_aux is not None
    in_specs = [pl.BlockSpec((tm, d), lambda i: (i, 0)),
                pl.BlockSpec((1, d), lambda i: (0, 0)),
                pl.BlockSpec((d, m), lambda i: (0, 0))]
    out_specs = [pl.BlockSpec((tm, m), lambda i: (i, 0))]
    out_shape = [jax.ShapeDtypeStruct((n, m), BF16)]
    args = [x, gain.reshape(1, d), w]
    if has_aux:
        w_hi = w_aux.astype(BF16)
        w_lo = (w_aux - w_hi.astype(F32)).astype(BF16)
        in_specs += [pl.BlockSpec((d, LANES), lambda i: (0, 0))] * 2
        out_specs.append(pl.BlockSpec((tm, LANES), lambda i: (i, 0)))
        out_shape.append(jax.ShapeDtypeStruct((n, LANES), F32))
        args += [w_hi, w_lo]
    res = pl.pallas_call(
        functools.partial(_norm_proj_kernel, has_aux),
        grid=(n // tm,),
        in_specs=in_specs, out_specs=out_specs, out_shape=out_shape,
        compiler_params=pltpu.CompilerParams(
            dimension_semantics=("parallel",), vmem_limit_bytes=VMEM_LIMIT),
    )(*args)
    return res if has_aux else res[0]


def _proj_res_kernel(n_in, *refs):
    h_ref = refs[2 * n_in]
    o_ref = refs[2 * n_in + 1]
    acc = h_ref[...]
    for t in range(n_in):
        acc = acc + jnp.dot(refs[2 * t][...], refs[2 * t + 1][...], preferred_element_type=F32)
    o_ref[...] = acc


def _proj_residual(pairs, h):
    n, d = h.shape
    tm = PROJ_TM
    in_specs, args = [], []
    for a, w in pairs:
        k = a.shape[1]
        in_specs += [pl.BlockSpec((tm, k), lambda i: (i, 0)),
                     pl.BlockSpec((k, d), lambda i: (0, 0))]
        args += [a, w]
    in_specs.append(pl.BlockSpec((tm, d), lambda i: (i, 0)))
    args.append(h)
    return pl.pallas_call(
        functools.partial(_proj_res_kernel, len(pairs)),
        grid=(n // tm,),
        in_specs=in_specs,
        out_specs=pl.BlockSpec((tm, d), lambda i: (i, 0)),
        out_shape=jax.ShapeDtypeStruct((n, d), F32),
        compiler_params=pltpu.CompilerParams(
            dimension_semantics=("parallel",), vmem_limit_bytes=VMEM_LIMIT),
    )(*args)


_HGRN_LEVELS = (64, 32, 16)
_HGRN_DIAG = 8


def _hgrn_constants():
    c = HGRN_CHUNK
    idx = np.arange(c)
    low = (idx[None, :] <= idx[:, None]).astype(np.float64)

    def ref_rows(r):
        return (idx[None, :] <= r[:, None]).astype(np.float64)

    blocks = [low, ref_rows(np.full(c, c - 1)) - low]
    masks = []
    for b in _HGRN_LEVELS:
        start = (idx // b) * b
        upper = (idx - start) >= b // 2
        ref = start + b // 2 - 1
        blocks.append(low - ref_rows(np.where(upper, ref, idx)))
        blocks.append(ref_rows(np.where(upper, idx, ref)) - low)
        same = (idx[:, None] // b) == (idx[None, :] // b)
        masks.append(same & upper[:, None] & ~upper[None, :])
    ref = (idx // _HGRN_DIAG) * _HGRN_DIAG + _HGRN_DIAG // 2 - 1
    blocks.append(low - ref_rows(ref))
    blocks.append(ref_rows(ref) - low)
    same = (idx[:, None] // _HGRN_DIAG) == (idx[None, :] // _HGRN_DIAG)
    masks.append(same & (idx[None, :] <= idx[:, None]))
    dst = np.concatenate(blocks, axis=0)
    return dst.astype(np.float32), np.stack(masks).astype(np.float32)


def _hgrn_kernel(q_ref, f_ref, i_ref, g_ref, lb_ref, gn_ref, dst_ref, mask_ref, bd_ref, grp_ref,
                 o_ref, st_ref):
    c = HGRN_CHUNK
    n_batch, n_pairs = st_ref.shape[:2]
    n_lvl = mask_ref.shape[0]

    @pl.when(pl.program_id(0) == 0)
    def _():
        st_ref[...] = jnp.zeros_like(st_ref)

    lb = lb_ref[...]
    gn = gn_ref[...]
    dst = dst_ref[...]
    bd = bd_ref[...]
    grp = grp_ref[...]
    low = lax.broadcasted_iota(jnp.int32, (c, LANES), 1) < HEAD_DIM

    def stack(x):
        return jnp.concatenate([jnp.where(low, x, jnp.zeros_like(x)), jnp.where(low, jnp.zeros_like(x), x)],
                               axis=0)

    for ch, b in [(ch, b) for ch in range(q_ref.shape[1] // c) for b in range(n_batch)]:
        rows = pl.ds(ch * c, c)
        q = q_ref[b, rows, :].astype(F32)
        qf = q * _sigmoid(q)
        f = lb + (1.0 - lb) * _sigmoid(f_ref[b, rows, :].astype(F32))
        kk = 1.0 - f
        ex = jnp.exp(_dot3_stacked(dst, jnp.log(f)))
        v = i_ref[b, rows, :].astype(BF16)
        g = g_ref[b, rows, :].astype(F32)
        gate = g * _sigmoid(g)
        q_in = (qf * ex[0:c]).astype(BF16)
        k_st = (kk * ex[c:2 * c]).astype(BF16)
        dec = ex[c - 1:c]
        q_l = [(qf * ex[(2 + 2 * l) * c:(3 + 2 * l) * c]).astype(BF16) for l in range(n_lvl)]
        k_l = [(kk * ex[(3 + 2 * l) * c:(4 + 2 * l) * c]).astype(BF16) for l in range(n_lvl)]
        outs = []
        for p in range(n_pairs):
            ps = slice(p * LANES, (p + 1) * LANES)
            scores = mask_ref[0] * _dot_nt(stack(q_l[0][:, ps]), k_l[0][:, ps])
            for l in range(1, n_lvl):
                scores = scores + mask_ref[l] * _dot_nt(stack(q_l[l][:, ps]), k_l[l][:, ps])
            pv = jnp.dot(scores.astype(BF16), v[:, ps], preferred_element_type=F32)
            st = st_ref[b, p]
            o = jnp.where(low, pv[:c], pv[c:]) + _dot_nt(q_in[:, ps], st.astype(BF16))
            st_ref[b, p] = st * dec[:, ps] + bd * _dot_tn(v[:, ps], k_st[:, ps])
            outs.append(o * lax.rsqrt(_dot2_rhs(o * o, grp) + EPS) * gn)
        o_ref[b, rows, :] = (jnp.concatenate(outs, axis=-1) * gate).astype(o_ref.dtype)


def _hgrn2(proj, lb, out_norm, batch, seq):
    n = proj.shape[0]
    width = lb.shape[0]
    n_heads = width // HEAD_DIM
    rb = HGRN_ROWS
    spb = seq // rb
    dst, masks = _hgrn_constants()
    col = lambda j: pl.BlockSpec((batch, rb, width), lambda s, j=j: (0, s, j))
    full = lambda a: pl.BlockSpec(a.shape, lambda s: (0,) * a.ndim)
    lb2 = lb.reshape(1, width)
    gn = jnp.tile(out_norm, LANES // HEAD_DIM).reshape(1, LANES)
    dst = jnp.asarray(np.concatenate([dst, dst, dst], axis=1), BF16)
    masks = jnp.asarray(np.concatenate([masks, masks], axis=1), F32)
    lane = np.arange(LANES)
    bd = jnp.asarray((lane[:, None] // HEAD_DIM) == (lane[None, :] // HEAD_DIM), F32)
    grp = _group_mean_matrix()
    proj3 = proj.reshape(batch, seq, proj.shape[1])
    out = pl.pallas_call(
        _hgrn_kernel,
        grid=(spb,),
        in_specs=[col(0), col(1), col(2), col(3), full(lb2), full(gn), full(dst), full(masks),
                  full(bd), full(grp)],
        out_specs=pl.BlockSpec((batch, rb, width), lambda s: (0, s, 0)),
        out_shape=jax.ShapeDtypeStruct((batch, seq, width), BF16),
        scratch_shapes=[pltpu.VMEM((batch, n_heads // 2, LANES, LANES), F32)],
        compiler_params=pltpu.CompilerParams(
            dimension_semantics=("arbitrary",), vmem_limit_bytes=VMEM_LIMIT),
    )(proj3, proj3, proj3, proj3, lb2, gn, dst, masks, bd, grp)
    return out.reshape(n, width)


def _fox_prep_kernel(q_ref, k_ref, v_ref, gate_ref, bias_ref, gq_ref, gk_ref, tril_ref,
                     sq_ref, sk_ref, cq_ref, ck_ref, cv_ref, grp_ref,
                     qa_ref, ka_ref, va_ref, carry_ref):
    n_heads = qa_ref.shape[1]
    tm = q_ref.shape[0]

    @pl.when(pl.program_id(1) == 0)
    def _():
        carry_ref[...] = jnp.zeros_like(carry_ref)

    z = gate_ref[...] + bias_ref[...]
    ls = -(jnp.maximum(-z, 0.0) + jnp.log(1.0 + jnp.exp(-jnp.abs(z))))
    cum = _dot3(tril_ref[...], ls) + carry_ref[...]
    carry_ref[...] = cum[tm - 1:tm]
    cum = cum * LOG2E

    c3 = jnp.concatenate(_split3(cum), axis=-1)
    ext_q = jnp.dot(c3, sq_ref[...], preferred_element_type=F32) + cq_ref[...]
    ext_k = jnp.dot(c3, sk_ref[...], preferred_element_type=F32) + ck_ref[...]

    lane = lax.broadcasted_iota(jnp.int32, (tm, LANES), 1)
    low_half = lane < HEAD_DIM
    grp = grp_ref[...]
    scale = HEAD_DIM ** -0.5 * LOG2E
    for c in range(n_heads // 2):
        cols = slice(c * LANES, (c + 1) * LANES)
        q = q_ref[:, cols].astype(F32)
        k = k_ref[:, cols].astype(F32)
        v = v_ref[:, cols]
        qn = q * lax.rsqrt(_dot2_rhs(q * q, grp) + EPS) * gq_ref[...] * scale
        kn = k * lax.rsqrt(_dot2_rhs(k * k, grp) + EPS) * gk_ref[...]
        for par in range(2):
            h = 2 * c + par
            data = low_half if par == 0 else jnp.logical_not(low_half)
            ext = slice(h * LANES, (h + 1) * LANES)
            qa_ref[0, h] = jnp.where(data, qn, ext_q[:, ext]).astype(BF16)
            ka_ref[0, h] = jnp.where(data, kn, ext_k[:, ext]).astype(BF16)
            va_ref[0, h] = jnp.where(data, v, cv_ref[par:par + 1, :].astype(BF16))


def _fox_layout_constants(n_heads):
    sq = np.zeros((3 * LANES, n_heads * LANES), np.float32)
    sk = np.zeros((3 * LANES, n_heads * LANES), np.float32)
    cq = np.zeros((1, n_heads * LANES), np.float32)
    ck = np.zeros((1, n_heads * LANES), np.float32)
    cv = np.zeros((2, LANES), np.float32)
    for h in range(n_heads):
        x0 = h * LANES + (HEAD_DIM if h % 2 == 0 else 0)
        for t in range(3):
            sq[t * LANES + h, x0 + t] = 1.0
            sk[t * LANES + h, x0 + 3 + t] = -1.0
        cq[0, x0 + 3:x0 + 6] = 1.0
        ck[0, x0:x0 + 3] = 1.0
    cv[0, HEAD_DIM] = 1.0
    cv[1, 0] = 1.0
    return (jnp.asarray(sq, BF16), jnp.asarray(sk, BF16), jnp.asarray(cq), jnp.asarray(ck), jnp.asarray(cv))


def _group_mean_matrix():
    lane = np.arange(LANES)
    return jnp.asarray(((lane[:, None] // HEAD_DIM) == (lane[None, :] // HEAD_DIM)) / HEAD_DIM, BF16)


def _fox_prep(proj, gates, f_bias, q_norm, k_norm, batch, seq, col0):
    width = 512
    n_heads = width // HEAD_DIM
    tm = PREP_ROWS
    spb = seq // tm
    col = lambda j: pl.BlockSpec((tm, width), lambda b, s, j=j: (b * spb + s, col0 + j))
    full = lambda a: pl.BlockSpec(a.shape, lambda b, s: (0,) * a.ndim)
    bias = jnp.zeros((1, LANES), F32).at[0, :n_heads].set(f_bias)
    gq = jnp.tile(q_norm, LANES // HEAD_DIM).reshape(1, LANES)
    gk = jnp.tile(k_norm, LANES // HEAD_DIM).reshape(1, LANES)
    tril = jnp.asarray(np.tril(np.ones((tm, tm), np.float32)), BF16)
    consts = _fox_layout_constants(n_heads) + (_group_mean_matrix(),)
    out = jax.ShapeDtypeStruct((batch, n_heads, seq, LANES), BF16)
    ospec = pl.BlockSpec((1, n_heads, tm, LANES), lambda b, s: (b, 0, s, 0))
    return pl.pallas_call(
        _fox_prep_kernel,
        grid=(batch, spb),
        in_specs=[col(0), col(1), col(2),
                  pl.BlockSpec((tm, LANES), lambda b, s: (b * spb + s, 0)),
                  full(bias), full(gq), full(gk), full(tril)] + [full(a) for a in consts],
        out_specs=[ospec, ospec, ospec],
        out_shape=[out, out, out],
        scratch_shapes=[pltpu.VMEM((1, LANES), F32)],
        compiler_params=pltpu.CompilerParams(
            dimension_semantics=("parallel", "arbitrary"), vmem_limit_bytes=VMEM_LIMIT),
    )(proj, proj, proj, gates, bias, gq, gk, tril, *consts)


def _tri_tables(nq):
    qi = [q for q in range(nq) for _ in range(q + 1)]
    ki = [k for q in range(nq) for k in range(q + 1)]
    return jnp.asarray(qi, jnp.int32), jnp.asarray(ki, jnp.int32)


def _fox_attn_kernel(qt_ref, kt_ref, q_ref, k_ref, v_ref, o_ref, m_ref, acc_ref):
    p_idx = pl.program_id(2)
    qi = qt_ref[p_idx]
    ki = kt_ref[p_idx]
    hp = q_ref.shape[1]
    t = q_ref.shape[2]

    @pl.when(ki == 0)
    def _():
        m_ref[...] = jnp.full_like(m_ref, -jnp.inf)
        acc_ref[...] = jnp.zeros_like(acc_ref)

    def step(masked):
        for h in range(hp):
            s = _dot_nt(q_ref[0, h], k_ref[0, h])
            if masked:
                row = lax.broadcasted_iota(jnp.int32, (t, t), 0)
                colm = lax.broadcasted_iota(jnp.int32, (t, t), 1)
                s = jnp.where(colm <= row, s, -jnp.inf)
            m_old = m_ref[h]
            m_new = jnp.maximum(m_old, jnp.max(s, axis=-1, keepdims=True))
            p = jnp.exp2(s - m_new)
            acc_ref[h] = (jnp.exp2(m_old - m_new) * acc_ref[h]
                          + jnp.dot(p.astype(BF16), v_ref[0, h], preferred_element_type=F32))
            m_ref[h] = m_new

    @pl.when(ki < qi)
    def _():
        step(False)

    @pl.when(ki == qi)
    def _():
        step(True)
        _fox_finalize(acc_ref, o_ref)


def _fox_finalize(acc_ref, o_ref):
    a0 = acc_ref[0]
    a1 = acc_ref[1]
    lane = lax.broadcasted_iota(jnp.int32, a0.shape, 1)
    o_ref[0] = jnp.where(lane < HEAD_DIM, a0 / a0[:, HEAD_DIM:HEAD_DIM + 1], a1 / a1[:, 0:1]).astype(o_ref.dtype)


def _tile_plan(n_sub, diagonal):
    plan = []
    for qb in range(n_sub):
        if not diagonal:
            plan.append((qb, 0, n_sub, False))
        else:
            if qb > 0:
                plan.append((qb, 0, qb, False))
            plan.append((qb, qb, qb + 1, True))
    return plan


def _fox_fast_kernel(qt_ref, kt_ref, q_ref, k_ref, v_ref, o_ref, acc_ref):
    p_idx = pl.program_id(2)
    qi = qt_ref[p_idx]
    ki = kt_ref[p_idx]
    hp = q_ref.shape[1]
    sb = ATTN_FAST_SUB
    n_sub = q_ref.shape[2] // sb

    @pl.when(ki == 0)
    def _():
        acc_ref[...] = jnp.zeros_like(acc_ref)

    def tile(diagonal):
        for h in range(hp):
            for qb, k0, k1, masked in _tile_plan(n_sub, diagonal):
                rows = pl.ds(qb * sb, sb)
                cols = pl.ds(k0 * sb, (k1 - k0) * sb)
                s = _dot_nt(q_ref[0, h, rows, :], k_ref[0, h, cols, :])
                if masked:
                    row = lax.broadcasted_iota(jnp.int32, (sb, sb), 0)
                    colm = lax.broadcasted_iota(jnp.int32, (sb, sb), 1)
                    s = jnp.where(colm <= row, s, -jnp.inf)
                p = jnp.exp2(s).astype(BF16)
                acc_ref[h, rows, :] += jnp.dot(p, v_ref[0, h, cols, :], preferred_element_type=F32)

    @pl.when(ki < qi)
    def _():
        tile(False)

    @pl.when(ki == qi)
    def _():
        tile(True)
        _fox_finalize(acc_ref, o_ref)


def _fox_attention(qa, ka, va, fast):
    batch, n_heads, seq, _ = qa.shape
    t = ATTN_FAST_TILE if fast else ATTN_TILE
    hp = 2
    nq = seq // t
    qt, kt = _tri_tables(nq)
    qspec = pl.BlockSpec((1, hp, t, LANES), lambda b, g, p, qt, kt: (b, g, qt[p], 0))
    kspec = pl.BlockSpec((1, hp, t, LANES), lambda b, g, p, qt, kt: (b, g, kt[p], 0))
    scratch = [pltpu.VMEM((hp, t, LANES), F32)]
    if not fast:
        scratch = [pltpu.VMEM((hp, t, 1), F32)] + scratch
    return pl.pallas_call(
        _fox_fast_kernel if fast else _fox_attn_kernel,
        grid_spec=pltpu.PrefetchScalarGridSpec(
            num_scalar_prefetch=2,
            grid=(batch, n_heads // hp, int(qt.shape[0])),
            in_specs=[qspec, kspec, kspec],
            out_specs=pl.BlockSpec((1, t, hp * HEAD_DIM), lambda b, g, p, qt, kt: (b, qt[p], g)),
            scratch_shapes=scratch),
        out_shape=jax.ShapeDtypeStruct((batch, seq, n_heads * HEAD_DIM), BF16),
        compiler_params=pltpu.CompilerParams(
            dimension_semantics=("parallel", "parallel", "arbitrary"), vmem_limit_bytes=VMEM_LIMIT),
    )(qt, kt, qa, ka, va)


def _logit_bound(q_gain, k_gain):
    return HEAD_DIM ** 0.5 * jnp.max(jnp.abs(q_gain)) * jnp.max(jnp.abs(k_gain))


def _diff_prep_kernel(q_ref, k_ref, v_ref, pos_ref, invf_ref, gq_ref, gk_ref, grp_ref, sel_ref,
                      qm_ref, k2_ref, va_ref, cs_ref, sn_ref):
    n_heads = k2_ref.shape[1]
    tm = q_ref.shape[0]
    ang = pos_ref[...].astype(F32) * invf_ref[...]
    cs_c = jnp.cos(ang)
    sn_c = jnp.sin(ang)
    per_row = LANES // (HEAD_DIM // 2)
    for j in range(per_row):
        cs_ref[pl.ds(j, tm // per_row, stride=per_row), :] = _dot3_rhs(cs_c, sel_ref[j])
        sn_ref[pl.ds(j, tm // per_row, stride=per_row), :] = _dot3_rhs(sn_c, sel_ref[j])
    lane = lax.broadcasted_iota(jnp.int32, (tm, LANES), 1)
    first = (lane % HEAD_DIM) < (HEAD_DIM // 2)
    cs = cs_ref[...]
    sn = sn_ref[...]
    sn = jnp.where(first, -sn, sn)
    grp = grp_ref[...]
    scale = HEAD_DIM ** -0.5 * LOG2E
    zero = jnp.zeros((tm, LANES), F32)
    onecol = jnp.where(lane == 0, 1.0, 0.0).astype(BF16)

    def norm_rope(x, gain):
        ms = _dot2_rhs(x * x, grp)
        y = x * lax.rsqrt(ms + EPS) * gain
        yr = jnp.where(first, pltpu.roll(y, LANES - HEAD_DIM // 2, 1), pltpu.roll(y, HEAD_DIM // 2, 1))
        return y * cs + yr * sn

    for h in range(n_heads):
        cols = slice(h * LANES, (h + 1) * LANES)
        qr = norm_rope(q_ref[:, cols].astype(F32), gq_ref[...]) * scale
        kr = norm_rope(k_ref[:, cols].astype(F32), gk_ref[...])
        qm_ref[0, h, 0] = jnp.where(lane < HEAD_DIM, qr, zero).astype(BF16)
        qm_ref[0, h, 1] = jnp.where(lane < HEAD_DIM, zero, qr).astype(BF16)
        k2_ref[0, h] = kr.astype(BF16)
        va_ref[0, h] = jnp.concatenate([v_ref[:, cols].astype(BF16), onecol], axis=-1)


def _diff_prep(proj, positions, q_norm, k_norm, batch, seq):
    n = proj.shape[0]
    width = proj.shape[1] // 3
    n_heads = width // LANES
    tm = PREP_ROWS
    spb = seq // tm
    col = lambda j: pl.BlockSpec((tm, width), lambda b, s, j=j: (b * spb + s, j))
    full = lambda a: pl.BlockSpec(a.shape, lambda b, s: (0,) * a.ndim)
    half = HEAD_DIM // 2
    inv_freq = ROPE_THETA ** (-jnp.arange(half, dtype=F32) / half)
    invf = jnp.tile(inv_freq, LANES // half).reshape(1, LANES)
    gq = jnp.tile(q_norm, LANES // HEAD_DIM).reshape(1, LANES)
    gk = jnp.tile(k_norm, LANES // HEAD_DIM).reshape(1, LANES)
    grp = _group_mean_matrix()
    per_row = LANES // half
    pos = jnp.repeat(positions.reshape(n // per_row, per_row).astype(jnp.int32), half, axis=1)
    lane = np.arange(LANES)
    sel = jnp.asarray(np.stack([(lane[:, None] == j * half + lane[None, :] % half) for j in range(per_row)]),
                      BF16)
    return pl.pallas_call(
        _diff_prep_kernel,
        grid=(batch, spb),
        in_specs=[col(0), col(1), col(2),
                  pl.BlockSpec((tm // per_row, LANES), lambda b, s: (b * spb + s, 0)),
                  full(invf), full(gq), full(gk), full(grp), full(sel)],
        out_specs=[pl.BlockSpec((1, n_heads, 2, tm, LANES), lambda b, s: (b, 0, 0, s, 0)),
                   pl.BlockSpec((1, n_heads, tm, LANES), lambda b, s: (b, 0, s, 0)),
                   pl.BlockSpec((1, n_heads, tm, 2 * LANES), lambda b, s: (b, 0, s, 0))],
        out_shape=[jax.ShapeDtypeStruct((batch, n_heads, 2, seq, LANES), BF16),
                   jax.ShapeDtypeStruct((batch, n_heads, seq, LANES), BF16),
                   jax.ShapeDtypeStruct((batch, n_heads, seq, 2 * LANES), BF16)],
        scratch_shapes=[pltpu.VMEM((tm, LANES), F32), pltpu.VMEM((tm, LANES), F32)],
        compiler_params=pltpu.CompilerParams(
            dimension_semantics=("parallel", "parallel"), vmem_limit_bytes=VMEM_LIMIT),
    )(proj, proj, proj, pos, invf, gq, gk, grp, sel)


def _diff_attn_kernel(lambda_init, qt_ref, kt_ref, q_ref, k_ref, v_ref, lam_ref, sub_ref,
                      o_ref, m_ref, acc_ref):
    p_idx = pl.program_id(2)
    qi = qt_ref[p_idx]
    ki = kt_ref[p_idx]
    t = k_ref.shape[2]
    dv = o_ref.shape[2]

    @pl.when(ki == 0)
    def _():
        m_ref[...] = jnp.full_like(m_ref, -jnp.inf)
        acc_ref[...] = jnp.zeros_like(acc_ref)

    def step(masked):
        for m in range(2):
            s = _dot_nt(q_ref[0, 0, m], k_ref[0, 0])
            if masked:
                row = lax.broadcasted_iota(jnp.int32, (t, t), 0) // CHUNK
                colm = lax.broadcasted_iota(jnp.int32, (t, t), 1) // CHUNK
                s = jnp.where(colm <= row, s, -jnp.inf)
            m_old = m_ref[m]
            m_new = jnp.maximum(m_old, jnp.max(s, axis=-1, keepdims=True))
            p = jnp.exp2(s - m_new)
            acc_ref[m] = (jnp.exp2(m_old - m_new) * acc_ref[m]
                          + jnp.dot(p.astype(BF16), v_ref[0, 0], preferred_element_type=F32))
            m_ref[m] = m_new

    @pl.when(ki < qi)
    def _():
        step(False)

    @pl.when(ki == qi)
    def _():
        step(True)
        _diff_finalize(lambda_init, acc_ref, lam_ref, sub_ref, o_ref)


def _diff_finalize(lambda_init, acc_ref, lam_ref, sub_ref, o_ref):
    dv = o_ref.shape[2]
    lp = lam_ref[...]
    lam = (jnp.exp(jnp.sum(lp[0:1] * lp[1:2], axis=-1, keepdims=True))
           - jnp.exp(jnp.sum(lp[2:3] * lp[3:4], axis=-1, keepdims=True)) + lambda_init)
    a0 = acc_ref[0]
    a1 = acc_ref[1]
    o = a0[:, :dv] / a0[:, dv:dv + 1] - lam * (a1[:, :dv] / a1[:, dv:dv + 1])
    ms = jnp.mean(o * o, axis=-1, keepdims=True)
    o_ref[0] = ((o * lax.rsqrt(ms + EPS) * sub_ref[...]) * (1.0 - lambda_init)).astype(o_ref.dtype)


def _diff_fast_kernel(lambda_init, qt_ref, kt_ref, q_ref, k_ref, v_ref, lam_ref, sub_ref, o_ref, acc_ref):
    p_idx = pl.program_id(2)
    qi = qt_ref[p_idx]
    ki = kt_ref[p_idx]
    sb = ATTN_FAST_SUB
    n_sub = k_ref.shape[2] // sb

    @pl.when(ki == 0)
    def _():
        acc_ref[...] = jnp.zeros_like(acc_ref)

    def tile(diagonal):
        for m in range(2):
            for qb, k0, k1, masked in _tile_plan(n_sub, diagonal):
                rows = pl.ds(qb * sb, sb)
                cols = pl.ds(k0 * sb, (k1 - k0) * sb)
                s = _dot_nt(q_ref[0, 0, m, rows, :], k_ref[0, 0, cols, :])
                if masked:
                    row = lax.broadcasted_iota(jnp.int32, (sb, sb), 0) // CHUNK
                    colm = lax.broadcasted_iota(jnp.int32, (sb, sb), 1) // CHUNK
                    s = jnp.where(colm <= row, s, -jnp.inf)
                p = jnp.exp2(s).astype(BF16)
                acc_ref[m, rows, :] += jnp.dot(p, v_ref[0, 0, cols, :], preferred_element_type=F32)

    @pl.when(ki < qi)
    def _():
        tile(False)

    @pl.when(ki == qi)
    def _():
        tile(True)
        _diff_finalize(lambda_init, acc_ref, lam_ref, sub_ref, o_ref)


def _diff_attention(qm, k2, va, lam_params, subln, lambda_init, fast):
    batch, n_heads, seq, _ = k2.shape
    dv = va.shape[3] // 2
    t = ATTN_FAST_TILE if fast else ATTN_TILE
    nq = seq // t
    qt, kt = _tri_tables(nq)
    lamp = jnp.zeros((8, LANES), F32).at[:4, :HEAD_DIM].set(lam_params)
    sub = subln.reshape(1, dv)
    scratch = [pltpu.VMEM((2, t, 2 * dv), F32)]
    if not fast:
        scratch = [pltpu.VMEM((2, t, 1), F32)] + scratch
    return pl.pallas_call(
        functools.partial(_diff_fast_kernel if fast else _diff_attn_kernel, lambda_init),
        grid_spec=pltpu.PrefetchScalarGridSpec(
            num_scalar_prefetch=2,
            grid=(batch, n_heads, int(qt.shape[0])),
            in_specs=[pl.BlockSpec((1, 1, 2, t, LANES), lambda b, h, p, qt, kt: (b, h, 0, qt[p], 0)),
                      pl.BlockSpec((1, 1, t, LANES), lambda b, h, p, qt, kt: (b, h, kt[p], 0)),
                      pl.BlockSpec((1, 1, t, 2 * dv), lambda b, h, p, qt, kt: (b, h, kt[p], 0)),
                      pl.BlockSpec((8, LANES), lambda b, h, p, qt, kt: (0, 0)),
                      pl.BlockSpec((1, dv), lambda b, h, p, qt, kt: (0, 0))],
            out_specs=pl.BlockSpec((1, t, dv), lambda b, h, p, qt, kt: (b, qt[p], h)),
            scratch_shapes=scratch),
        out_shape=jax.ShapeDtypeStruct((batch, seq, n_heads * dv), BF16),
        compiler_params=pltpu.CompilerParams(
            dimension_semantics=("parallel", "parallel", "arbitrary"), vmem_limit_bytes=VMEM_LIMIT),
    )(qt, kt, qm, k2, va, lamp, sub)


def _store_token_tiles(ref, value):
    t, width = value.shape
    s = width // LANES
    for j in range(s):
        ref[pl.ds(j, t, stride=s), :] = value[:, j * LANES:(j + 1) * LANES]


def _load_token_tiles(ref, first_row, t, s):
    return jnp.concatenate([ref[pl.ds(first_row + j, t, stride=s), :] for j in range(s)], axis=-1)


def _router_kernel(h_ref, g_ref, whi_ref, wlo_ref, tri_ref, xn_ref, route_ref, route_t_ref, cnt_ref, run_ref):
    @pl.when(pl.program_id(0) == 0)
    def _():
        run_ref[...] = jnp.zeros_like(run_ref)

    x = h_ref[...]
    tm = x.shape[0]
    xn = x * lax.rsqrt(jnp.mean(x * x, axis=-1, keepdims=True) + EPS) * g_ref[...]
    _store_token_tiles(xn_ref, xn)
    xh = xn.astype(BF16)
    xl = (xn - xh.astype(F32)).astype(BF16)
    d = lambda a, b: jnp.dot(a, b[...], preferred_element_type=F32)
    logits = d(xh, whi_ref) + (d(xl, whi_ref) + d(xh, wlo_ref))
    lane = lax.broadcasted_iota(jnp.int32, (tm, LANES), 1)
    neg = jnp.full((tm, LANES), -jnp.inf, F32)
    big = jnp.full((tm, LANES), LANES, jnp.int32)

    def top1(vals):
        m = jnp.max(vals, axis=-1, keepdims=True)
        idx = jnp.min(jnp.where(vals == m, lane, big), axis=-1, keepdims=True)
        return m, idx

    grp_logits = jnp.where(lane < N_GROUPS, logits, neg)
    mg, gidx = top1(grp_logits)
    p_g = 1.0 / jnp.sum(jnp.exp(grp_logits - mg), axis=-1, keepdims=True)
    e_lane = lane - N_GROUPS
    in_grp = (e_lane >= gidx * EXPERTS_PER_GROUP) & (e_lane < (gidx + 1) * EXPERTS_PER_GROUP)
    sel = jnp.where(in_grp, logits, neg)
    m1, i1 = top1(sel)
    m2, i2 = top1(jnp.where(lane == i1, neg, sel))
    r = jnp.exp(m2 - m1)
    w1 = p_g / (1.0 + r)
    w2 = p_g * r / (1.0 + r)
    zero = jnp.zeros((tm, LANES), F32)
    chosen = jnp.where((lane == i1) | (lane == i2), 1.0, 0.0)
    before = jnp.dot(tri_ref[...], chosen.astype(BF16), preferred_element_type=F32) + run_ref[...]
    rank1 = jnp.sum(jnp.where(lane == i1, before, zero), axis=-1, keepdims=True)
    rank2 = jnp.sum(jnp.where(lane == i2, before, zero), axis=-1, keepdims=True)
    run = run_ref[...] + jnp.sum(chosen, axis=0, keepdims=True)
    run_ref[...] = run
    cnt_ref[...] = jnp.broadcast_to(run, cnt_ref.shape)
    route = jnp.where(lane == 0, (i1 - N_GROUPS).astype(F32),
            jnp.where(lane == 1, (i2 - N_GROUPS).astype(F32),
            jnp.where(lane == 2, w1, jnp.where(lane == 3, w2,
            jnp.where(lane == 4, rank1, jnp.where(lane == 5, rank2, zero))))))
    route_ref[...] = route
    route_t_ref[...] = route.T[:TILE_ROWS]


def _router(h, gain, w_group, w_expert):
    n, d = h.shape
    tm = ROUTER_ROWS
    wr = jnp.zeros((d, LANES), F32)
    wr = wr.at[:, :N_GROUPS].set(w_group)
    wr = wr.at[:, N_GROUPS:N_GROUPS + N_EXPERTS].set(
        jnp.transpose(w_expert, (1, 0, 2)).reshape(d, N_EXPERTS))
    w_hi = wr.astype(BF16)
    w_lo = (wr - w_hi.astype(F32)).astype(BF16)
    tri = jnp.asarray(np.tril(np.ones((tm, tm), np.float32), -1), BF16)
    xn, route, route_t, counts = pl.pallas_call(
        _router_kernel,
        grid=(n // tm,),
        in_specs=[pl.BlockSpec((tm, d), lambda i: (i, 0)),
                  pl.BlockSpec((1, d), lambda i: (0, 0)),
                  pl.BlockSpec((d, LANES), lambda i: (0, 0)),
                  pl.BlockSpec((d, LANES), lambda i: (0, 0)),
                  pl.BlockSpec((tm, tm), lambda i: (0, 0))],
        out_specs=[pl.BlockSpec((tm * d // LANES, LANES), lambda i: (i, 0)),
                   pl.BlockSpec((tm, LANES), lambda i: (i, 0)),
                   pl.BlockSpec((TILE_ROWS, tm), lambda i: (0, i)),
                   pl.BlockSpec((TILE_ROWS, LANES), lambda i: (0, 0))],
        out_shape=[jax.ShapeDtypeStruct((n * d // LANES, LANES), F32),
                   jax.ShapeDtypeStruct((n, LANES), F32),
                   jax.ShapeDtypeStruct((TILE_ROWS, n), F32),
                   jax.ShapeDtypeStruct((TILE_ROWS, LANES), F32)],
        scratch_shapes=[pltpu.VMEM((1, LANES), F32)],
        compiler_params=pltpu.CompilerParams(
            dimension_semantics=("arbitrary",), vmem_limit_bytes=VMEM_LIMIT),
    )(h, gain.reshape(1, d), w_hi, w_lo, tri)
    return xn, route, route_t, counts[0, N_GROUPS:N_GROUPS + N_EXPERTS].astype(jnp.int32)


def _dispatch_tables(expert_ids, rank, counts, tm):
    n = expert_ids.shape[1]
    n_tiles = 2 * n // tm + N_EXPERTS
    padded = ((counts + tm - 1) // tm) * tm
    ends = jnp.cumsum(padded)
    starts = ends - padded
    pos = jnp.take(starts, expert_ids) + rank
    tile_start = jnp.arange(n_tiles, dtype=jnp.int32) * tm
    tile_expert = jnp.minimum(jnp.sum(tile_start[:, None] >= ends[None, :], axis=1),
                              N_EXPERTS - 1).astype(jnp.int32)
    n_valid = (ends[-1] // tm).astype(jnp.int32).reshape(1)
    pad_start = (starts + counts).astype(jnp.int32)
    pad_count = (padded - counts).astype(jnp.int32)
    return pos.astype(jnp.int32), tile_expert, n_valid, n_tiles, pad_start, pad_count


def _token_copy(src_hbm, tok, dst_ref, r, sem):
    src = src_hbm.at[pl.ds(pl.multiple_of(tok * TILE_ROWS, TILE_ROWS), TILE_ROWS)]
    first = r * TILE_ROWS if isinstance(r, int) else pl.multiple_of(r * TILE_ROWS, TILE_ROWS)
    return pltpu.make_async_copy(src, dst_ref.at[pl.ds(first, TILE_ROWS)], sem)


def _gather_tokens(src_hbm, idx_ref, dst_ref, sem, n_tokens):
    def body(r, carry):
        _token_copy(src_hbm, idx_ref[0, 0, r], dst_ref, r, sem).start()
        return carry
    lax.fori_loop(0, n_tokens, body, 0, unroll=8)


def _wait_tokens(src_hbm, dst_ref, sem):
    pltpu.make_async_copy(src_hbm.at[pl.ds(0, dst_ref.shape[0])], dst_ref, sem).wait()


def _dispatch_kernel(ps_ref, pc_ref, nv_ref, pos_ref, x_ref, xs_hbm, zero_blk, sem, pad_sem):
    i = pl.program_id(0)
    tokens = pos_ref.shape[2] // 2

    def slot_tile(slot):
        return xs_hbm.at[pl.ds(pl.multiple_of(slot * TILE_ROWS, TILE_ROWS), TILE_ROWS)]

    for r in range(2 * tokens):
        src = x_ref.at[pl.ds((r % tokens) * TILE_ROWS, TILE_ROWS)]
        pltpu.make_async_copy(src, slot_tile(pos_ref[0, 0, r]), sem).start(priority=r % 2)
    rows = 2 * tokens * TILE_ROWS
    pltpu.make_async_copy(xs_hbm.at[pl.ds(0, rows)], xs_hbm.at[pl.ds(0, rows)], sem).wait()

    @pl.when(i == pl.num_programs(0) - 1)
    def _():
        zero_blk[...] = jnp.zeros_like(zero_blk)
        tile_slots = zero_blk.shape[0] // TILE_ROWS
        for e in range(ps_ref.shape[0]):
            first = ps_ref[e]
            count = pc_ref[e]
            for wait in (False, True):
                for bit in range(tile_slots.bit_length() - 1):
                    chunk = (1 << bit) * TILE_ROWS
                    start = (first + (count & ((1 << bit) - 1))) * TILE_ROWS
                    copy = pltpu.make_async_copy(
                        zero_blk.at[pl.ds(0, chunk)],
                        xs_hbm.at[pl.ds(pl.multiple_of(start, TILE_ROWS), chunk)], pad_sem)

                    @pl.when(((count >> bit) & 1) == 1)
                    def _(copy=copy, wait=wait):
                        copy.wait() if wait else copy.start()

        block_rows = zero_blk.shape[0]
        n_blocks = xs_hbm.shape[0] // block_rows

        def block(t):
            return xs_hbm.at[pl.ds(pl.multiple_of(t * block_rows, block_rows), block_rows)]

        def fill_block(t, carry):
            pltpu.make_async_copy(zero_blk, block(t), pad_sem).start()
            return carry

        def drain_block(t, carry):
            pltpu.make_async_copy(zero_blk, block(t), pad_sem).wait()
            return carry

        lax.fori_loop(nv_ref[0], n_blocks, fill_block, 0)
        lax.fori_loop(nv_ref[0], n_blocks, drain_block, 0)


def _pair_table(pos, rows):
    steps = pos.shape[1] // rows
    return jnp.transpose(pos.reshape(2, steps, rows), (1, 0, 2)).reshape(steps, 1, 2 * rows)


def _moe_dispatch(xn, pair_tab, pad_start, pad_count, n_valid, n_tiles, tm):
    steps = pair_tab.shape[0]
    tb = pair_tab.shape[2] // 2
    return pl.pallas_call(
        _dispatch_kernel,
        grid_spec=pltpu.PrefetchScalarGridSpec(
            num_scalar_prefetch=3,
            grid=(steps,),
            in_specs=[pl.BlockSpec((1, 1, 2 * tb), lambda i, *_: (i, 0, 0), memory_space=pltpu.SMEM),
                      pl.BlockSpec((tb * TILE_ROWS, LANES), lambda i, *_: (i, 0))],
            out_specs=pl.BlockSpec(memory_space=pl.ANY),
            scratch_shapes=[pltpu.VMEM((tm * TILE_ROWS, LANES), F32),
                            pltpu.SemaphoreType.DMA, pltpu.SemaphoreType.DMA]),
        out_shape=jax.ShapeDtypeStruct((n_tiles * tm * TILE_ROWS, LANES), F32),
        compiler_params=pltpu.CompilerParams(
            dimension_semantics=("arbitrary",), vmem_limit_bytes=VMEM_LIMIT),
    )(pad_start, pad_count, n_valid, pair_tab, xn)


def _moe_kernel(te_ref, nv_ref, x_ref, wg_ref, wu_ref, wd_ref, o_ref, wg_b, wu_b, wd_b):
    i = pl.program_id(0)
    tm = x_ref.shape[0] // TILE_ROWS
    n_valid = nv_ref[0]
    new_expert = jnp.logical_or(i == 0, te_ref[i] != te_ref[jnp.maximum(i - 1, 0)])

    @pl.when(jnp.logical_and(i < n_valid, new_expert))
    def _():
        wg_b[...] = wg_ref[0].astype(BF16)
        wu_b[...] = wu_ref[0].astype(BF16)
        wd_b[...] = wd_ref[0].astype(BF16)

    @pl.when(i < n_valid)
    def _():
        x = _load_token_tiles(x_ref, 0, tm, TILE_ROWS).astype(BF16)
        g = jnp.dot(x, wg_b[...], preferred_element_type=F32)
        u = jnp.dot(x, wu_b[...], preferred_element_type=F32)
        hid = (g * _sigmoid(g) * u).astype(BF16)
        _store_token_tiles(o_ref, jnp.dot(hid, wd_b[...], preferred_element_type=F32))

    @pl.when(i >= n_valid)
    def _():
        o_ref[...] = jnp.zeros_like(o_ref)


def _moe_experts(xs, tile_expert, n_valid, tm, w_gate, w_up, w_down):
    n_exp, d, f = w_gate.shape
    assert d == TILE_ROWS * LANES
    n_tiles = xs.shape[0] // (tm * TILE_ROWS)
    x_index = lambda i, te, nv: (jnp.minimum(i, nv[0] - 1), 0)
    return pl.pallas_call(
        _moe_kernel,
        grid_spec=pltpu.PrefetchScalarGridSpec(
            num_scalar_prefetch=2,
            grid=(n_tiles,),
            in_specs=[
                pl.BlockSpec((tm * TILE_ROWS, LANES), x_index),
                pl.BlockSpec((1, d, f), lambda i, te, nv: (te[i], 0, 0)),
                pl.BlockSpec((1, d, f), lambda i, te, nv: (te[i], 0, 0)),
                pl.BlockSpec((1, f, d), lambda i, te, nv: (te[i], 0, 0))],
            out_specs=pl.BlockSpec((tm * TILE_ROWS, LANES), lambda i, te, nv: (i, 0)),
            scratch_shapes=[pltpu.VMEM((d, f), BF16), pltpu.VMEM((d, f), BF16), pltpu.VMEM((f, d), BF16)]),
        out_shape=jax.ShapeDtypeStruct((n_tiles * tm * TILE_ROWS, LANES), F32),
        compiler_params=pltpu.CompilerParams(
            dimension_semantics=("arbitrary",), vmem_limit_bytes=VMEM_LIMIT),
    )(tile_expert, n_valid, xs, w_gate, w_up, w_down)


def _combine_kernel(pos_ref, pos_next_ref, h_ref, route_ref, y_hbm, o_ref, ybuf, sems):
    i = pl.program_id(0)
    n_steps = pl.num_programs(0)
    tokens = ybuf.shape[1] // TILE_ROWS
    slot = i % 2

    @pl.when(i == 0)
    def _():
        _gather_tokens(y_hbm, pos_ref, ybuf.at[0], sems.at[0], tokens)

    @pl.when(i + 1 < n_steps)
    def _():
        for r in range(tokens):
            _token_copy(y_hbm, pos_next_ref[0, 0, r], ybuf.at[1 - slot], r,
                        sems.at[1 - slot]).start(priority=r % 2)

    _wait_tokens(y_hbm, ybuf.at[slot], sems.at[slot])
    tc = tokens // 2
    w = route_ref[...]
    first = _load_token_tiles(ybuf.at[slot], 0, tc, TILE_ROWS)
    second = _load_token_tiles(ybuf.at[slot], tc * TILE_ROWS, tc, TILE_ROWS)
    o_ref[...] = h_ref[...] + w[:, 2:3] * first + w[:, 3:4] * second


def _moe_combine(h, route, pair_tab, y_sorted):
    n, d = h.shape
    steps = pair_tab.shape[0]
    tc = pair_tab.shape[2] // 2
    return pl.pallas_call(
        _combine_kernel,
        grid=(steps,),
        in_specs=[pl.BlockSpec((1, 1, 2 * tc), lambda i: (i, 0, 0), memory_space=pltpu.SMEM),
                  pl.BlockSpec((1, 1, 2 * tc), lambda i: (jnp.minimum(i + 1, steps - 1), 0, 0),
                               memory_space=pltpu.SMEM),
                  pl.BlockSpec((tc, d), lambda i: (i, 0)),
                  pl.BlockSpec((tc, LANES), lambda i: (i, 0)),
                  pl.BlockSpec(memory_space=pl.ANY)],
        out_specs=pl.BlockSpec((tc, d), lambda i: (i, 0)),
        out_shape=jax.ShapeDtypeStruct((n, d), F32),
        scratch_shapes=[pltpu.VMEM((2, 2 * tc * TILE_ROWS, LANES), F32), pltpu.SemaphoreType.DMA((2,))],
        compiler_params=pltpu.CompilerParams(
            dimension_semantics=("arbitrary",), vmem_limit_bytes=VMEM_LIMIT),
    )(pair_tab, pair_tab, h, route, y_sorted)


def _moe_layer(h, layer, gain, w_group, w_expert, w_gate, w_up, w_down):
    d = h.shape[1]
    f = w_gate.shape[-1]
    xn, route, route_t, counts = _router(h, gain, w_group, w_expert)
    expert_ids = route_t[0:2].astype(jnp.int32)
    rank = route_t[4:6].astype(jnp.int32)
    pos, tile_expert, n_valid, n_tiles, pad_start, pad_count = _dispatch_tables(expert_ids, rank, counts,
                                                                                MOE_TM)
    pair_tab = _pair_table(pos, COMBINE_ROWS)
    xs = _moe_dispatch(xn, pair_tab, pad_start, pad_count, n_valid, n_tiles, MOE_TM)
    y_sorted = _moe_experts(xs, tile_expert + layer * N_EXPERTS, n_valid, MOE_TM,
                            w_gate.reshape(-1, d, f), w_up.reshape(-1, d, f), w_down.reshape(-1, f, d))
    return _moe_combine(h, route, pair_tab, y_sorted)


def _even_layer(h, batch, seq, gain, w_in, w_out, lb, f_bias, out_norm, q_norm, k_norm):
    d = h.shape[1]
    n_main = w_in.shape[1] - f_bias.shape[0]
    w_main = w_in[:, :n_main].astype(BF16)
    w_gate = jnp.zeros((d, LANES), F32).at[:, :f_bias.shape[0]].set(w_in[:, n_main:])
    proj, gates = _norm_proj(h, gain, w_main, w_gate)
    o_a = _hgrn2(proj, lb, out_norm, batch, seq)
    qa, ka, va = _fox_prep(proj, gates, f_bias, q_norm, k_norm, batch, seq, col0=4)
    o_b = lax.cond(_logit_bound(q_norm, k_norm) <= LOGIT_BOUND_MAX,
                   functools.partial(_fox_attention, fast=True),
                   functools.partial(_fox_attention, fast=False), qa, ka, va).reshape(batch * seq, -1)
    wo = w_out.astype(BF16)
    ka_dim = o_a.shape[1]
    return _proj_residual([(o_a, wo[:ka_dim]), (o_b, wo[ka_dim:])], h)


def _odd_layer(h, positions, batch, seq, gain, w_in, w_out, q_norm, k_norm, lam_params, subln, lambda_init):
    proj = _norm_proj(h, gain, w_in.astype(BF16))
    qm, k2, va = _diff_prep(proj, positions, q_norm, k_norm, batch, seq)
    attn = lambda fast: functools.partial(_diff_attention, lam_params=lam_params, subln=subln,
                                          lambda_init=lambda_init, fast=fast)
    o = lax.cond(_logit_bound(q_norm, k_norm) <= LOGIT_BOUND_MAX,
                 attn(True), attn(False), qm, k2, va).reshape(batch * seq, -1)
    return _proj_residual([(o, w_out.astype(BF16))], h)


def kernel(x, positions, hgrn_lb_logits, norm_mix, norm_ffn, even_w_in, even_w_out, fox_f_bias,
           hgrn_out_norm, fox_q_norm, fox_k_norm, odd_w_in, odd_w_out, diff_q_norm, diff_k_norm,
           diff_lambda_q1, diff_lambda_k1, diff_lambda_q2, diff_lambda_k2, diff_subln,
           moe_router_group, moe_router_expert, moe_w_gate, moe_w_up, moe_w_down):
    batch, seq, d = x.shape
    depth = norm_mix.shape[0]
    lower_bounds = jnp.cumsum(jax.nn.softmax(hgrn_lb_logits.astype(F32), axis=0), axis=0)
    h = x.reshape(batch * seq, d)
    for layer in range(depth):
        j = layer // 2
        if layer % 2 == 0:
            h = _even_layer(h, batch, seq, norm_mix[layer], even_w_in[j], even_w_out[j], lower_bounds[j],
                            fox_f_bias[j], hgrn_out_norm[j], fox_q_norm[j], fox_k_norm[j])
        else:
            lambda_init = 0.8 - 0.6 * math.exp(-0.3 * layer)
            lam_params = jnp.stack([diff_lambda_q1[j], diff_lambda_k1[j],
                                    diff_lambda_q2[j], diff_lambda_k2[j]]).astype(F32)
            h = _odd_layer(h, positions, batch, seq, norm_mix[layer], odd_w_in[j], odd_w_out[j],
                           diff_q_norm[j], diff_k_norm[j], lam_params, diff_subln[j], lambda_init)
        h = _moe_layer(h, layer, norm_ffn[layer], moe_router_group[layer], moe_router_expert[layer],
                       moe_w_gate, moe_w_up, moe_w_down)
    return h.reshape(batch, seq, d)
```

```python
import functools
import math

import numpy as np
import jax
import jax.numpy as jnp
from jax import lax
from jax.experimental import pallas as pl
from jax.experimental.pallas import tpu as pltpu

F32 = jnp.float32
BF16 = jnp.bfloat16

EPS = 1e-6
ROPE_THETA = 10000.0
CHUNK = 64
HEAD_DIM = 64
N_GROUPS = 4
EXPERTS_PER_GROUP = 8
N_EXPERTS = N_GROUPS * EXPERTS_PER_GROUP
LANES = 128
TILE_ROWS = 8

HGRN_CHUNK = 64
HGRN_ROWS = 256
ATTN_TILE = 512
ATTN_FAST_TILE = 2048
ATTN_FAST_SUB = 512
LOGIT_BOUND_MAX = 60.0
LOG2E = math.log2(math.e)
PREP_ROWS = 256
PROJ_TM = 512
ROUTER_ROWS = 256
MOE_TM = 512
COMBINE_ROWS = 256
VMEM_LIMIT = 56 * 1024 * 1024


def _split3(x):
    hi = x.astype(BF16)
    r1 = x - hi.astype(F32)
    mid = r1.astype(BF16)
    lo = (r1 - mid.astype(F32)).astype(BF16)
    return hi, mid, lo


def _dot3(const_bf16, x):
    hi, mid, lo = _split3(x)
    d = lambda b: jnp.dot(const_bf16, b, preferred_element_type=F32)
    return d(hi) + d(mid) + d(lo)


def _dot3_stacked(const3_bf16, x):
    return jnp.dot(const3_bf16, jnp.concatenate(_split3(x), axis=0), preferred_element_type=F32)


def _dot3_rhs(x, const_bf16):
    hi, mid, lo = _split3(x)
    d = lambda a: jnp.dot(a, const_bf16, preferred_element_type=F32)
    return d(hi) + d(mid) + d(lo)


def _dot2_rhs(x, const_bf16):
    hi = x.astype(BF16)
    lo = (x - hi.astype(F32)).astype(BF16)
    d = lambda a: jnp.dot(a, const_bf16, preferred_element_type=F32)
    return d(hi) + d(lo)


def _dot_nt(a, b):
    return lax.dot_general(a, b, (((1,), (1,)), ((), ())), preferred_element_type=F32)


def _dot_tn(a, b):
    return lax.dot_general(a, b, (((0,), (0,)), ((), ())), preferred_element_type=F32)


def _sigmoid(x):
    return 1.0 / (1.0 + jnp.exp(-x))


def _norm_proj_kernel(has_aux, x_ref, g_ref, w_ref, *rest):
    x = x_ref[...]
    ms = jnp.mean(x * x, axis=-1, keepdims=True)
    xn = x * lax.rsqrt(ms + EPS) * g_ref[...]
    xb = xn.astype(BF16)
    rest[-2 if has_aux else -1][...] = jnp.dot(xb, w_ref[...], preferred_element_type=F32).astype(BF16)
    if has_aux:
        whi_ref, wlo_ref, _, oaux_ref = rest
        xl = (xn - xb.astype(F32)).astype(BF16)
        d = lambda a, b: jnp.dot(a, b[...], preferred_element_type=F32)
        oaux_ref[...] = d(xb, whi_ref) + (d(xl, whi_ref) + d(xb, wlo_ref))


def _norm_proj(x, gain, w, w_aux=None):
    n, d = x.shape
    m = w.shape[1]
    tm = PROJ_TM
    has_aux = w_aux is not None
    in_specs = [pl.BlockSpec((tm, d), lambda i: (i, 0)),
                pl.BlockSpec((1, d), lambda i: (0, 0)),
                pl.BlockSpec((d, m), lambda i: (0, 0))]
    out_specs = [pl.BlockSpec((tm, m), lambda i: (i, 0))]
    out_shape = [jax.ShapeDtypeStruct((n, m), BF16)]
    args = [x, gain.reshape(1, d), w]
    if has_aux:
        w_hi = w_aux.astype(BF16)
        w_lo = (w_aux - w_hi.astype(F32)).astype(BF16)
        in_specs += [pl.BlockSpec((d, LANES), lambda i: (0, 0))] * 2
        out_specs.append(pl.BlockSpec((tm, LANES), lambda i: (i, 0)))
        out_shape.append(jax.ShapeDtypeStruct((n, LANES), F32))
        args += [w_hi, w_lo]
    res = pl.pallas_call(
        functools.partial(_norm_proj_kernel, has_aux),
        grid=(n // tm,),
        in_specs=in_specs, out_specs=out_specs, out_shape=out_shape,
        compiler_params=pltpu.CompilerParams(
            dimension_semantics=("parallel",), vmem_limit_bytes=VMEM_LIMIT),
    )(*args)
    return res if has_aux else res[0]


def _proj_res_kernel(n_in, *refs):
    h_ref = refs[2 * n_in]
    o_ref = refs[2 * n_in + 1]
    acc = h_ref[...]
    for t in range(n_in):
        acc = acc + jnp.dot(refs[2 * t][...], refs[2 * t + 1][...], preferred_element_type=F32)
    o_ref[...] = acc


def _proj_residual(pairs, h):
    n, d = h.shape
    tm = PROJ_TM
    in_specs, args = [], []
    for a, w in pairs:
        k = a.shape[1]
        in_specs += [pl.BlockSpec((tm, k), lambda i: (i, 0)),
                     pl.BlockSpec((k, d), lambda i: (0, 0))]
        args += [a, w]
    in_specs.append(pl.BlockSpec((tm, d), lambda i: (i, 0)))
    args.append(h)
    return pl.pallas_call(
        functools.partial(_proj_res_kernel, len(pairs)),
        grid=(n // tm,),
        in_specs=in_specs,
        out_specs=pl.BlockSpec((tm, d), lambda i: (i, 0)),
        out_shape=jax.ShapeDtypeStruct((n, d), F32),
        compiler_params=pltpu.CompilerParams(
            dimension_semantics=("parallel",), vmem_limit_bytes=VMEM_LIMIT),
    )(*args)


_HGRN_LEVELS = (64, 32, 16)
_HGRN_DIAG = 8


def _hgrn_constants():
    c = HGRN_CHUNK
    idx = np.arange(c)
    low = (idx[None, :] <= idx[:, None]).astype(np.float64)

    def ref_rows(r):
        return (idx[None, :] <= r[:, None]).astype(np.float64)

    blocks = [low, ref_rows(np.full(c, c - 1)) - low]
    masks = []
    for b in _HGRN_LEVELS:
        start = (idx // b) * b
        upper = (idx - start) >= b // 2
        ref = start + b // 2 - 1
        blocks.append(low - ref_rows(np.where(upper, ref, idx)))
        blocks.append(ref_rows(np.where(upper, idx, ref)) - low)
        same = (idx[:, None] // b) == (idx[None, :] // b)
        masks.append(same & upper[:, None] & ~upper[None, :])
    ref = (idx // _HGRN_DIAG) * _HGRN_DIAG + _HGRN_DIAG // 2 - 1
    blocks.append(low - ref_rows(ref))
    blocks.append(ref_rows(ref) - low)
    same = (idx[:, None] // _HGRN_DIAG) == (idx[None, :] // _HGRN_DIAG)
    masks.append(same & (idx[None, :] <= idx[:, None]))
    dst = np.concatenate(blocks, axis=0)
    return dst.astype(np.float32), np.stack(masks).astype(np.float32)


def _hgrn_kernel(q_ref, f_ref, i_ref, g_ref, lb_ref, gn_ref, dst_ref, mask_ref, bd_ref, grp_ref,
                 o_ref, st_ref):
    c = HGRN_CHUNK
    n_batch, n_pairs = st_ref.shape[:2]
    n_lvl = mask_ref.shape[0]

    @pl.when(pl.program_id(0) == 0)
    def _():
        st_ref[...] = jnp.zeros_like(st_ref)

    lb = lb_ref[...]
    gn = gn_ref[...]
    dst = dst_ref[...]
    bd = bd_ref[...]
    grp = grp_ref[...]
    low = lax.broadcasted_iota(jnp.int32, (c, LANES), 1) < HEAD_DIM

    def stack(x):
        return jnp.concatenate([jnp.where(low, x, jnp.zeros_like(x)), jnp.where(low, jnp.zeros_like(x), x)],
                               axis=0)

    for ch, b in [(ch, b) for ch in range(q_ref.shape[1] // c) for b in range(n_batch)]:
        rows = pl.ds(ch * c, c)
        q = q_ref[b, rows, :].astype(F32)
        qf = q * _sigmoid(q)
        f = lb + (1.0 - lb) * _sigmoid(f_ref[b, rows, :].astype(F32))
        kk = 1.0 - f
        ex = jnp.exp(_dot3_stacked(dst, jnp.log(f)))
        v = i_ref[b, rows, :].astype(BF16)
        g = g_ref[b, rows, :].astype(F32)
        gate = g * _sigmoid(g)
        q_in = (qf * ex[0:c]).astype(BF16)
        k_st = (kk * ex[c:2 * c]).astype(BF16)
        dec = ex[c - 1:c]
        q_l = [(qf * ex[(2 + 2 * l) * c:(3 + 2 * l) * c]).astype(BF16) for l in range(n_lvl)]
        k_l = [(kk * ex[(3 + 2 * l) * c:(4 + 2 * l) * c]).astype(BF16) for l in range(n_lvl)]
        outs = []
        for p in range(n_pairs):
            ps = slice(p * LANES, (p + 1) * LANES)
            scores = mask_ref[0] * _dot_nt(stack(q_l[0][:, ps]), k_l[0][:, ps])
            for l in range(1, n_lvl):
                scores = scores + mask_ref[l] * _dot_nt(stack(q_l[l][:, ps]), k_l[l][:, ps])
            pv = jnp.dot(scores.astype(BF16), v[:, ps], preferred_element_type=F32)
            st = st_ref[b, p]
            o = jnp.where(low, pv[:c], pv[c:]) + _dot_nt(q_in[:, ps], st.astype(BF16))
            st_ref[b, p] = st * dec[:, ps] + bd * _dot_tn(v[:, ps], k_st[:, ps])
            outs.append(o * lax.rsqrt(_dot2_rhs(o * o, grp) + EPS) * gn)
        o_ref[b, rows, :] = (jnp.concatenate(outs, axis=-1) * gate).astype(o_ref.dtype)


def _hgrn2(proj, lb, out_norm, batch, seq):
    n = proj.shape[0]
    width = lb.shape[0]
    n_heads = width // HEAD_DIM
    rb = HGRN_ROWS
    spb = seq // rb
    dst, masks = _hgrn_constants()
    col = lambda j: pl.BlockSpec((batch, rb, width), lambda s, j=j: (0, s, j))
    full = lambda a: pl.BlockSpec(a.shape, lambda s: (0,) * a.ndim)
    lb2 = lb.reshape(1, width)
    gn = jnp.tile(out_norm, LANES // HEAD_DIM).reshape(1, LANES)
    dst = jnp.asarray(np.concatenate([dst, dst, dst], axis=1), BF16)
    masks = jnp.asarray(np.concatenate([masks, masks], axis=1), F32)
    lane = np.arange(LANES)
    bd = jnp.asarray((lane[:, None] // HEAD_DIM) == (lane[None, :] // HEAD_DIM), F32)
    grp = _group_mean_matrix()
    proj3 = proj.reshape(batch, seq, proj.shape[1])
    out = pl.pallas_call(
        _hgrn_kernel,
        grid=(spb,),
        in_specs=[col(0), col(1), col(2), col(3), full(lb2), full(gn), full(dst), full(masks),
                  full(bd), full(grp)],
        out_specs=pl.BlockSpec((batch, rb, width), lambda s: (0, s, 0)),
        out_shape=jax.ShapeDtypeStruct((batch, seq, width), BF16),
        scratch_shapes=[pltpu.VMEM((batch, n_heads // 2, LANES, LANES), F32)],
        compiler_params=pltpu.CompilerParams(
            dimension_semantics=("arbitrary",), vmem_limit_bytes=VMEM_LIMIT),
    )(proj3, proj3, proj3, proj3, lb2, gn, dst, masks, bd, grp)
    return out.reshape(n, width)


def _fox_prep_kernel(q_ref, k_ref, v_ref, gate_ref, bias_ref, gq_ref, gk_ref, tril_ref,
                     sq_ref, sk_ref, cq_ref, ck_ref, cv_ref, grp_ref,
                     qa_ref, ka_ref, va_ref, carry_ref):
    n_heads = qa_ref.shape[1]
    tm = q_ref.shape[0]

    @pl.when(pl.program_id(1) == 0)
    def _():
        carry_ref[...] = jnp.zeros_like(carry_ref)

    z = gate_ref[...] + bias_ref[...]
    ls = -(jnp.maximum(-z, 0.0) + jnp.log(1.0 + jnp.exp(-jnp.abs(z))))
    cum = _dot3(tril_ref[...], ls) + carry_ref[...]
    carry_ref[...] = cum[tm - 1:tm]
    cum = cum * LOG2E

    c3 = jnp.concatenate(_split3(cum), axis=-1)
    ext_q = jnp.dot(c3, sq_ref[...], preferred_element_type=F32) + cq_ref[...]
    ext_k = jnp.dot(c3, sk_ref[...], preferred_element_type=F32) + ck_ref[...]

    lane = lax.broadcasted_iota(jnp.int32, (tm, LANES), 1)
    low_half = lane < HEAD_DIM
    grp = grp_ref[...]
    scale = HEAD_DIM ** -0.5 * LOG2E
    for c in range(n_heads // 2):
        cols = slice(c * LANES, (c + 1) * LANES)
        q = q_ref[:, cols].astype(F32)
        k = k_ref[:, cols].astype(F32)
        v = v_ref[:, cols]
        qn = q * lax.rsqrt(_dot2_rhs(q * q, grp) + EPS) * gq_ref[...] * scale
        kn = k * lax.rsqrt(_dot2_rhs(k * k, grp) + EPS) * gk_ref[...]
        for par in range(2):
            h = 2 * c + par
            data = low_half if par == 0 else jnp.logical_not(low_half)
            ext = slice(h * LANES, (h + 1) * LANES)
            qa_ref[0, h] = jnp.where(data, qn, ext_q[:, ext]).astype(BF16)
            ka_ref[0, h] = jnp.where(data, kn, ext_k[:, ext]).astype(BF16)
            va_ref[0, h] = jnp.where(data, v, cv_ref[par:par + 1, :].astype(BF16))


def _fox_layout_constants(n_heads):
    sq = np.zeros((3 * LANES, n_heads * LANES), np.float32)
    sk = np.zeros((3 * LANES, n_heads * LANES), np.float32)
    cq = np.zeros((1, n_heads * LANES), np.float32)
    ck = np.zeros((1, n_heads * LANES), np.float32)
    cv = np.zeros((2, LANES), np.float32)
    for h in range(n_heads):
        x0 = h * LANES + (HEAD_DIM if h % 2 == 0 else 0)
        for t in range(3):
            sq[t * LANES + h, x0 + t] = 1.0
            sk[t * LANES + h, x0 + 3 + t] = -1.0
        cq[0, x0 + 3:x0 + 6] = 1.0
        ck[0, x0:x0 + 3] = 1.0
    cv[0, HEAD_DIM] = 1.0
    cv[1, 0] = 1.0
    return (jnp.asarray(sq, BF16), jnp.asarray(sk, BF16), jnp.asarray(cq), jnp.asarray(ck), jnp.asarray(cv))


def _group_mean_matrix():
    lane = np.arange(LANES)
    return jnp.asarray(((lane[:, None] // HEAD_DIM) == (lane[None, :] // HEAD_DIM)) / HEAD_DIM, BF16)


def _fox_prep(proj, gates, f_bias, q_norm, k_norm, batch, seq, col0):
    width = 512
    n_heads = width // HEAD_DIM
    tm = PREP_ROWS
    spb = seq // tm
    col = lambda j: pl.BlockSpec((tm, width), lambda b, s, j=j: (b * spb + s, col0 + j))
    full = lambda a: pl.BlockSpec(a.shape, lambda b, s: (0,) * a.ndim)
    bias = jnp.zeros((1, LANES), F32).at[0, :n_heads].set(f_bias)
    gq = jnp.tile(q_norm, LANES // HEAD_DIM).reshape(1, LANES)
    gk = jnp.tile(k_norm, LANES // HEAD_DIM).reshape(1, LANES)
    tril = jnp.asarray(np.tril(np.ones((tm, tm), np.float32)), BF16)
    consts = _fox_layout_constants(n_heads) + (_group_mean_matrix(),)
    out = jax.ShapeDtypeStruct((batch, n_heads, seq, LANES), BF16)
    ospec = pl.BlockSpec((1, n_heads, tm, LANES), lambda b, s: (b, 0, s, 0))
    return pl.pallas_call(
        _fox_prep_kernel,
        grid=(batch, spb),
        in_specs=[col(0), col(1), col(2),
                  pl.BlockSpec((tm, LANES), lambda b, s: (b * spb + s, 0)),
                  full(bias), full(gq), full(gk), full(tril)] + [full(a) for a in consts],
        out_specs=[ospec, ospec, ospec],
        out_shape=[out, out, out],
        scratch_shapes=[pltpu.VMEM((1, LANES), F32)],
        compiler_params=pltpu.CompilerParams(
            dimension_semantics=("parallel", "arbitrary"), vmem_limit_bytes=VMEM_LIMIT),
    )(proj, proj, proj, gates, bias, gq, gk, tril, *consts)


def _tri_tables(nq):
    qi = [q for q in range(nq) for _ in range(q + 1)]
    ki = [k for q in range(nq) for k in range(q + 1)]
    return jnp.asarray(qi, jnp.int32), jnp.asarray(ki, jnp.int32)


def _fox_attn_kernel(qt_ref, kt_ref, q_ref, k_ref, v_ref, o_ref, m_ref, acc_ref):
    p_idx = pl.program_id(2)
    qi = qt_ref[p_idx]
    ki = kt_ref[p_idx]
    hp = q_ref.shape[1]
    t = q_ref.shape[2]

    @pl.when(ki == 0)
    def _():
        m_ref[...] = jnp.full_like(m_ref, -jnp.inf)
        acc_ref[...] = jnp.zeros_like(acc_ref)

    def step(masked):
        for h in range(hp):
            s = _dot_nt(q_ref[0, h], k_ref[0, h])
            if masked:
                row = lax.broadcasted_iota(jnp.int32, (t, t), 0)
                colm = lax.broadcasted_iota(jnp.int32, (t, t), 1)
                s = jnp.where(colm <= row, s, -jnp.inf)
            m_old = m_ref[h]
            m_new = jnp.maximum(m_old, jnp.max(s, axis=-1, keepdims=True))
            p = jnp.exp2(s - m_new)
            acc_ref[h] = (jnp.exp2(m_old - m_new) * acc_ref[h]
                          + jnp.dot(p.astype(BF16), v_ref[0, h], preferred_element_type=F32))
            m_ref[h] = m_new

    @pl.when(ki < qi)
    def _():
        step(False)

    @pl.when(ki == qi)
    def _():
        step(True)
        _fox_finalize(acc_ref, o_ref)


def _fox_finalize(acc_ref, o_ref):
    a0 = acc_ref[0]
    a1 = acc_ref[1]
    lane = lax.broadcasted_iota(jnp.int32, a0.shape, 1)
    o_ref[0] = jnp.where(lane < HEAD_DIM, a0 / a0[:, HEAD_DIM:HEAD_DIM + 1], a1 / a1[:, 0:1]).astype(o_ref.dtype)


def _tile_plan(n_sub, diagonal):
    plan = []
    for qb in range(n_sub):
        if not diagonal:
            plan.append((qb, 0, n_sub, False))
        else:
            if qb > 0:
                plan.append((qb, 0, qb, False))
            plan.append((qb, qb, qb + 1, True))
    return plan


def _fox_fast_kernel(qt_ref, kt_ref, q_ref, k_ref, v_ref, o_ref, acc_ref):
    p_idx = pl.program_id(2)
    qi = qt_ref[p_idx]
    ki = kt_ref[p_idx]
    hp = q_ref.shape[1]
    sb = ATTN_FAST_SUB
    n_sub = q_ref.shape[2] // sb

    @pl.when(ki == 0)
    def _():
        acc_ref[...] = jnp.zeros_like(acc_ref)

    def tile(diagonal):
        for h in range(hp):
            for qb, k0, k1, masked in _tile_plan(n_sub, diagonal):
                rows = pl.ds(qb * sb, sb)
                cols = pl.ds(k0 * sb, (k1 - k0) * sb)
                s = _dot_nt(q_ref[0, h, rows, :], k_ref[0, h, cols, :])
                if masked:
                    row = lax.broadcasted_iota(jnp.int32, (sb, sb), 0)
                    colm = lax.broadcasted_iota(jnp.int32, (sb, sb), 1)
                    s = jnp.where(colm <= row, s, -jnp.inf)
                p = jnp.exp2(s).astype(BF16)
                acc_ref[h, rows, :] += jnp.dot(p, v_ref[0, h, cols, :], preferred_element_type=F32)

    @pl.when(ki < qi)
    def _():
        tile(False)

    @pl.when(ki == qi)
    def _():
        tile(True)
        _fox_finalize(acc_ref, o_ref)


def _fox_attention(qa, ka, va, fast):
    batch, n_heads, seq, _ = qa.shape
    t = ATTN_FAST_TILE if fast else ATTN_TILE
    hp = 2
    nq = seq // t
    qt, kt = _tri_tables(nq)
    qspec = pl.BlockSpec((1, hp, t, LANES), lambda b, g, p, qt, kt: (b, g, qt[p], 0))
    kspec = pl.BlockSpec((1, hp, t, LANES), lambda b, g, p, qt, kt: (b, g, kt[p], 0))
    scratch = [pltpu.VMEM((hp, t, LANES), F32)]
    if not fast:
        scratch = [pltpu.VMEM((hp, t, 1), F32)] + scratch
    return pl.pallas_call(
        _fox_fast_kernel if fast else _fox_attn_kernel,
        grid_spec=pltpu.PrefetchScalarGridSpec(
            num_scalar_prefetch=2,
            grid=(batch, n_heads // hp, int(qt.shape[0])),
            in_specs=[qspec, kspec, kspec],
            out_specs=pl.BlockSpec((1, t, hp * HEAD_DIM), lambda b, g, p, qt, kt: (b, qt[p], g)),
            scratch_shapes=scratch),
        out_shape=jax.ShapeDtypeStruct((batch, seq, n_heads * HEAD_DIM), BF16),
        compiler_params=pltpu.CompilerParams(
            dimension_semantics=("parallel", "parallel", "arbitrary"), vmem_limit_bytes=VMEM_LIMIT),
    )(qt, kt, qa, ka, va)


def _logit_bound(q_gain, k_gain):
    return HEAD_DIM ** 0.5 * jnp.max(jnp.abs(q_gain)) * jnp.max(jnp.abs(k_gain))


def _diff_prep_kernel(q_ref, k_ref, v_ref, pos_ref, invf_ref, gq_ref, gk_ref, grp_ref, sel_ref,
                      qm_ref, k2_ref, va_ref, cs_ref, sn_ref):
    n_heads = k2_ref.shape[1]
    tm = q_ref.shape[0]
    ang = pos_ref[...].astype(F32) * invf_ref[...]
    cs_c = jnp.cos(ang)
    sn_c = jnp.sin(ang)
    per_row = LANES // (HEAD_DIM // 2)
    for j in range(per_row):
        cs_ref[pl.ds(j, tm // per_row, stride=per_row), :] = _dot3_rhs(cs_c, sel_ref[j])
        sn_ref[pl.ds(j, tm // per_row, stride=per_row), :] = _dot3_rhs(sn_c, sel_ref[j])
    lane = lax.broadcasted_iota(jnp.int32, (tm, LANES), 1)
    first = (lane % HEAD_DIM) < (HEAD_DIM // 2)
    cs = cs_ref[...]
    sn = sn_ref[...]
    sn = jnp.where(first, -sn, sn)
    grp = grp_ref[...]
    scale = HEAD_DIM ** -0.5 * LOG2E
    zero = jnp.zeros((tm, LANES), F32)
    onecol = jnp.where(lane == 0, 1.0, 0.0).astype(BF16)

    def norm_rope(x, gain):
        ms = _dot2_rhs(x * x, grp)
        y = x * lax.rsqrt(ms + EPS) * gain
        yr = jnp.where(first, pltpu.roll(y, LANES - HEAD_DIM // 2, 1), pltpu.roll(y, HEAD_DIM // 2, 1))
        return y * cs + yr * sn

    for h in range(n_heads):
        cols = slice(h * LANES, (h + 1) * LANES)
        qr = norm_rope(q_ref[:, cols].astype(F32), gq_ref[...]) * scale
        kr = norm_rope(k_ref[:, cols].astype(F32), gk_ref[...])
        qm_ref[0, h, 0] = jnp.where(lane < HEAD_DIM, qr, zero).astype(BF16)
        qm_ref[0, h, 1] = jnp.where(lane < HEAD_DIM, zero, qr).astype(BF16)
        k2_ref[0, h] = kr.astype(BF16)
        va_ref[0, h] = jnp.concatenate([v_ref[:, cols].astype(BF16), onecol], axis=-1)


def _diff_prep(proj, positions, q_norm, k_norm, batch, seq):
    n = proj.shape[0]
    width = proj.shape[1] // 3
    n_heads = width // LANES
    tm = PREP_ROWS
    spb = seq // tm
    col = lambda j: pl.BlockSpec((tm, width), lambda b, s, j=j: (b * spb + s, j))
    full = lambda a: pl.BlockSpec(a.shape, lambda b, s: (0,) * a.ndim)
    half = HEAD_DIM // 2
    inv_freq = ROPE_THETA ** (-jnp.arange(half, dtype=F32) / half)
    invf = jnp.tile(inv_freq, LANES // half).reshape(1, LANES)
    gq = jnp.tile(q_norm, LANES // HEAD_DIM).reshape(1, LANES)
    gk = jnp.tile(k_norm, LANES // HEAD_DIM).reshape(1, LANES)
    grp = _group_mean_matrix()
    per_row = LANES // half
    pos = jnp.repeat(positions.reshape(n // per_row, per_row).astype(jnp.int32), half, axis=1)
    lane = np.arange(LANES)
    sel = jnp.asarray(np.stack([(lane[:, None] == j * half + lane[None, :] % half) for j in range(per_row)]),
                      BF16)
    return pl.pallas_call(
        _diff_prep_kernel,
        grid=(batch, spb),
        in_specs=[col(0), col(1), col(2),
                  pl.BlockSpec((tm // per_row, LANES), lambda b, s: (b * spb + s, 0)),
                  full(invf), full(gq), full(gk), full(grp), full(sel)],
        out_specs=[pl.BlockSpec((1, n_heads, 2, tm, LANES), lambda b, s: (b, 0, 0, s, 0)),
                   pl.BlockSpec((1, n_heads, tm, LANES), lambda b, s: (b, 0, s, 0)),
                   pl.BlockSpec((1, n_heads, tm, 2 * LANES), lambda b, s: (b, 0, s, 0))],
        out_shape=[jax.ShapeDtypeStruct((batch, n_heads, 2, seq, LANES), BF16),
                   jax.ShapeDtypeStruct((batch, n_heads, seq, LANES), BF16),
                   jax.ShapeDtypeStruct((batch, n_heads, seq, 2 * LANES), BF16)],
        scratch_shapes=[pltpu.VMEM((tm, LANES), F32), pltpu.VMEM((tm, LANES), F32)],
        compiler_params=pltpu.CompilerParams(
            dimension_semantics=("parallel", "parallel"), vmem_limit_bytes=VMEM_LIMIT),
    )(proj, proj, proj, pos, invf, gq, gk, grp, sel)


def _diff_attn_kernel(lambda_init, qt_ref, kt_ref, q_ref, k_ref, v_ref, lam_ref, sub_ref,
                      o_ref, m_ref, acc_ref):
    p_idx = pl.program_id(2)
    qi = qt_ref[p_idx]
    ki = kt_ref[p_idx]
    t = k_ref.shape[2]
    dv = o_ref.shape[2]

    @pl.when(ki == 0)
    def _():
        m_ref[...] = jnp.full_like(m_ref, -jnp.inf)
        acc_ref[...] = jnp.zeros_like(acc_ref)

    def step(masked):
        for m in range(2):
            s = _dot_nt(q_ref[0, 0, m], k_ref[0, 0])
            if masked:
                row = lax.broadcasted_iota(jnp.int32, (t, t), 0) // CHUNK
                colm = lax.broadcasted_iota(jnp.int32, (t, t), 1) // CHUNK
                s = jnp.where(colm <= row, s, -jnp.inf)
            m_old = m_ref[m]
            m_new = jnp.maximum(m_old, jnp.max(s, axis=-1, keepdims=True))
            p = jnp.exp2(s - m_new)
            acc_ref[m] = (jnp.exp2(m_old - m_new) * acc_ref[m]
                          + jnp.dot(p.astype(BF16), v_ref[0, 0], preferred_element_type=F32))
            m_ref[m] = m_new

    @pl.when(ki < qi)
    def _():
        step(False)

    @pl.when(ki == qi)
    def _():
        step(True)
        _diff_finalize(lambda_init, acc_ref, lam_ref, sub_ref, o_ref)


def _diff_finalize(lambda_init, acc_ref, lam_ref, sub_ref, o_ref):
    dv = o_ref.shape[2]
    lp = lam_ref[...]
    lam = (jnp.exp(jnp.sum(lp[0:1] * lp[1:2], axis=-1, keepdims=True))
           - jnp.exp(jnp.sum(lp[2:3] * lp[3:4], axis=-1, keepdims=True)) + lambda_init)
    a0 = acc_ref[0]
    a1 = acc_ref[1]
    o = a0[:, :dv] / a0[:, dv:dv + 1] - lam * (a1[:, :dv] / a1[:, dv:dv + 1])
    ms = jnp.mean(o * o, axis=-1, keepdims=True)
    o_ref[0] = ((o * lax.rsqrt(ms + EPS) * sub_ref[...]) * (1.0 - lambda_init)).astype(o_ref.dtype)


def _diff_fast_kernel(lambda_init, qt_ref, kt_ref, q_ref, k_ref, v_ref, lam_ref, sub_ref, o_ref, acc_ref):
    p_idx = pl.program_id(2)
    qi = qt_ref[p_idx]
    ki = kt_ref[p_idx]
    sb = ATTN_FAST_SUB
    n_sub = k_ref.shape[2] // sb

    @pl.when(ki == 0)
    def _():
        acc_ref[...] = jnp.zeros_like(acc_ref)

    def tile(diagonal):
        for m in range(2):
            for qb, k0, k1, masked in _tile_plan(n_sub, diagonal):
                rows = pl.ds(qb * sb, sb)
                cols = pl.ds(k0 * sb, (k1 - k0) * sb)
                s = _dot_nt(q_ref[0, 0, m, rows, :], k_ref[0, 0, cols, :])
                if masked:
                    row = lax.broadcasted_iota(jnp.int32, (sb, sb), 0) // CHUNK
                    colm = lax.broadcasted_iota(jnp.int32, (sb, sb), 1) // CHUNK
                    s = jnp.where(colm <= row, s, -jnp.inf)
                p = jnp.exp2(s).astype(BF16)
                acc_ref[m, rows, :] += jnp.dot(p, v_ref[0, 0, cols, :], preferred_element_type=F32)

    @pl.when(ki < qi)
    def _():
        tile(False)

    @pl.when(ki == qi)
    def _():
        tile(True)
        _diff_finalize(lambda_init, acc_ref, lam_ref, sub_ref, o_ref)


def _diff_attention(qm, k2, va, lam_params, subln, lambda_init, fast):
    batch, n_heads, seq, _ = k2.shape
    dv = va.shape[3] // 2
    t = ATTN_FAST_TILE if fast else ATTN_TILE
    nq = seq // t
    qt, kt = _tri_tables(nq)
    lamp = jnp.zeros((8, LANES), F32).at[:4, :HEAD_DIM].set(lam_params)
    sub = subln.reshape(1, dv)
    scratch = [pltpu.VMEM((2, t, 2 * dv), F32)]
    if not fast:
        scratch = [pltpu.VMEM((2, t, 1), F32)] + scratch
    return pl.pallas_call(
        functools.partial(_diff_fast_kernel if fast else _diff_attn_kernel, lambda_init),
        grid_spec=pltpu.PrefetchScalarGridSpec(
            num_scalar_prefetch=2,
            grid=(batch, n_heads, int(qt.shape[0])),
            in_specs=[pl.BlockSpec((1, 1, 2, t, LANES), lambda b, h, p, qt, kt: (b, h, 0, qt[p], 0)),
                      pl.BlockSpec((1, 1, t, LANES), lambda b, h, p, qt, kt: (b, h, kt[p], 0)),
                      pl.BlockSpec((1, 1, t, 2 * dv), lambda b, h, p, qt, kt: (b, h, kt[p], 0)),
                      pl.BlockSpec((8, LANES), lambda b, h, p, qt, kt: (0, 0)),
                      pl.BlockSpec((1, dv), lambda b, h, p, qt, kt: (0, 0))],
            out_specs=pl.BlockSpec((1, t, dv), lambda b, h, p, qt, kt: (b, qt[p], h)),
            scratch_shapes=scratch),
        out_shape=jax.ShapeDtypeStruct((batch, seq, n_heads * dv), BF16),
        compiler_params=pltpu.CompilerParams(
            dimension_semantics=("parallel", "parallel", "arbitrary"), vmem_limit_bytes=VMEM_LIMIT),
    )(qt, kt, qm, k2, va, lamp, sub)


def _store_token_tiles(ref, value):
    t, width = value.shape
    s = width // LANES
    for j in range(s):
        ref[pl.ds(j, t, stride=s), :] = value[:, j * LANES:(j + 1) * LANES]


def _load_token_tiles(ref, first_row, t, s):
    return jnp.concatenate([ref[pl.ds(first_row + j, t, stride=s), :] for j in range(s)], axis=-1)


def _router_kernel(h_ref, g_ref, whi_ref, wlo_ref, tri_ref, xn_ref, route_ref, route_t_ref, cnt_ref, run_ref):
    @pl.when(pl.program_id(0) == 0)
    def _():
        run_ref[...] = jnp.zeros_like(run_ref)

    x = h_ref[...]
    tm = x.shape[0]
    xn = x * lax.rsqrt(jnp.mean(x * x, axis=-1, keepdims=True) + EPS) * g_ref[...]
    _store_token_tiles(xn_ref, xn)
    xh = xn.astype(BF16)
    xl = (xn - xh.astype(F32)).astype(BF16)
    d = lambda a, b: jnp.dot(a, b[...], preferred_element_type=F32)
    logits = d(xh, whi_ref) + (d(xl, whi_ref) + d(xh, wlo_ref))
    lane = lax.broadcasted_iota(jnp.int32, (tm, LANES), 1)
    neg = jnp.full((tm, LANES), -jnp.inf, F32)
    big = jnp.full((tm, LANES), LANES, jnp.int32)

    def top1(vals):
        m = jnp.max(vals, axis=-1, keepdims=True)
        idx = jnp.min(jnp.where(vals == m, lane, big), axis=-1, keepdims=True)
        return m, idx

    grp_logits = jnp.where(lane < N_GROUPS, logits, neg)
    mg, gidx = top1(grp_logits)
    p_g = 1.0 / jnp.sum(jnp.exp(grp_logits - mg), axis=-1, keepdims=True)
    e_lane = lane - N_GROUPS
    in_grp = (e_lane >= gidx * EXPERTS_PER_GROUP) & (e_lane < (gidx + 1) * EXPERTS_PER_GROUP)
    sel = jnp.where(in_grp, logits, neg)
    m1, i1 = top1(sel)
    m2, i2 = top1(jnp.where(lane == i1, neg, sel))
    r = jnp.exp(m2 - m1)
    w1 = p_g / (1.0 + r)
    w2 = p_g * r / (1.0 + r)
    zero = jnp.zeros((tm, LANES), F32)
    chosen = jnp.where((lane == i1) | (lane == i2), 1.0, 0.0)
    before = jnp.dot(tri_ref[...], chosen.astype(BF16), preferred_element_type=F32) + run_ref[...]
    rank1 = jnp.sum(jnp.where(lane == i1, before, zero), axis=-1, keepdims=True)
    rank2 = jnp.sum(jnp.where(lane == i2, before, zero), axis=-1, keepdims=True)
    run = run_ref[...] + jnp.sum(chosen, axis=0, keepdims=True)
    run_ref[...] = run
    cnt_ref[...] = jnp.broadcast_to(run, cnt_ref.shape)
    route = jnp.where(lane == 0, (i1 - N_GROUPS).astype(F32),
            jnp.where(lane == 1, (i2 - N_GROUPS).astype(F32),
            jnp.where(lane == 2, w1, jnp.where(lane == 3, w2,
            jnp.where(lane == 4, rank1, jnp.where(lane == 5, rank2, zero))))))
    route_ref[...] = route
    route_t_ref[...] = route.T[:TILE_ROWS]


def _router(h, gain, w_group, w_expert):
    n, d = h.shape
    tm = ROUTER_ROWS
    wr = jnp.zeros((d, LANES), F32)
    wr = wr.at[:, :N_GROUPS].set(w_group)
    wr = wr.at[:, N_GROUPS:N_GROUPS + N_EXPERTS].set(
        jnp.transpose(w_expert, (1, 0, 2)).reshape(d, N_EXPERTS))
    w_hi = wr.astype(BF16)
    w_lo = (wr - w_hi.astype(F32)).astype(BF16)
    tri = jnp.asarray(np.tril(np.ones((tm, tm), np.float32), -1), BF16)
    xn, route, route_t, counts = pl.pallas_call(
        _router_kernel,
        grid=(n // tm,),
        in_specs=[pl.BlockSpec((tm, d), lambda i: (i, 0)),
                  pl.BlockSpec((1, d), lambda i: (0, 0)),
                  pl.BlockSpec((d, LANES), lambda i: (0, 0)),
                  pl.BlockSpec((d, LANES), lambda i: (0, 0)),
                  pl.BlockSpec((tm, tm), lambda i: (0, 0))],
        out_specs=[pl.BlockSpec((tm * d // LANES, LANES), lambda i: (i, 0)),
                   pl.BlockSpec((tm, LANES), lambda i: (i, 0)),
                   pl.BlockSpec((TILE_ROWS, tm), lambda i: (0, i)),
                   pl.BlockSpec((TILE_ROWS, LANES), lambda i: (0, 0))],
        out_shape=[jax.ShapeDtypeStruct((n * d // LANES, LANES), F32),
                   jax.ShapeDtypeStruct((n, LANES), F32),
                   jax.ShapeDtypeStruct((TILE_ROWS, n), F32),
                   jax.ShapeDtypeStruct((TILE_ROWS, LANES), F32)],
        scratch_shapes=[pltpu.VMEM((1, LANES), F32)],
        compiler_params=pltpu.CompilerParams(
            dimension_semantics=("arbitrary",), vmem_limit_bytes=VMEM_LIMIT),
    )(h, gain.reshape(1, d), w_hi, w_lo, tri)
    return xn, route, route_t, counts[0, N_GROUPS:N_GROUPS + N_EXPERTS].astype(jnp.int32)


def _dispatch_tables(expert_ids, rank, counts, tm):
    n = expert_ids.shape[1]
    n_tiles = 2 * n // tm + N_EXPERTS
    padded = ((counts + tm - 1) // tm) * tm
    ends = jnp.cumsum(padded)
    starts = ends - padded
    pos = rank
    for e in range(N_EXPERTS):
        pos = pos + jnp.where(expert_ids == e, starts[e], 0)
    tile_start = jnp.arange(n_tiles, dtype=jnp.int32) * tm
    tile_expert = jnp.minimum(jnp.sum(tile_start[:, None] >= ends[None, :], axis=1),
                              N_EXPERTS - 1).astype(jnp.int32)
    n_valid = (ends[-1] // tm).astype(jnp.int32).reshape(1)
    pad_start = (starts + counts).astype(jnp.int32)
    pad_count = (padded - counts).astype(jnp.int32)
    return pos.astype(jnp.int32), tile_expert, n_valid, n_tiles, pad_start, pad_count


def _token_copy(src_hbm, tok, dst_ref, r, sem):
    src = src_hbm.at[pl.ds(pl.multiple_of(tok * TILE_ROWS, TILE_ROWS), TILE_ROWS)]
    first = r * TILE_ROWS if isinstance(r, int) else pl.multiple_of(r * TILE_ROWS, TILE_ROWS)
    return pltpu.make_async_copy(src, dst_ref.at[pl.ds(first, TILE_ROWS)], sem)


def _gather_tokens(src_hbm, idx_ref, dst_ref, sem, n_tokens):
    def body(r, carry):
        _token_copy(src_hbm, idx_ref[0, 0, r], dst_ref, r, sem).start()
        return carry
    lax.fori_loop(0, n_tokens, body, 0, unroll=8)


def _wait_tokens(src_hbm, dst_ref, sem):
    pltpu.make_async_copy(src_hbm.at[pl.ds(0, dst_ref.shape[0])], dst_ref, sem).wait()


def _dispatch_kernel(ps_ref, pc_ref, nv_ref, pos_ref, x_ref, xs_hbm, zero_blk, sem, pad_sem):
    i = pl.program_id(0)
    tokens = pos_ref.shape[2] // 2

    def slot_tile(slot):
        return xs_hbm.at[pl.ds(pl.multiple_of(slot * TILE_ROWS, TILE_ROWS), TILE_ROWS)]

    for r in range(2 * tokens):
        src = x_ref.at[pl.ds((r % tokens) * TILE_ROWS, TILE_ROWS)]
        pltpu.make_async_copy(src, slot_tile(pos_ref[0, 0, r]), sem).start(priority=r % 2)
    rows = 2 * tokens * TILE_ROWS
    pltpu.make_async_copy(xs_hbm.at[pl.ds(0, rows)], xs_hbm.at[pl.ds(0, rows)], sem).wait()

    @pl.when(i == pl.num_programs(0) - 1)
    def _():
        zero_blk[...] = jnp.zeros_like(zero_blk)
        tile_slots = zero_blk.shape[0] // TILE_ROWS
        for e in range(ps_ref.shape[0]):
            first = ps_ref[e]
            count = pc_ref[e]
            for wait in (False, True):
                for bit in range(tile_slots.bit_length() - 1):
                    chunk = (1 << bit) * TILE_ROWS
                    start = (first + (count & ((1 << bit) - 1))) * TILE_ROWS
                    copy = pltpu.make_async_copy(
                        zero_blk.at[pl.ds(0, chunk)],
                        xs_hbm.at[pl.ds(pl.multiple_of(start, TILE_ROWS), chunk)], pad_sem)

                    @pl.when(((count >> bit) & 1) == 1)
                    def _(copy=copy, wait=wait):
                        copy.wait() if wait else copy.start()

        block_rows = zero_blk.shape[0]
        n_blocks = xs_hbm.shape[0] // block_rows

        def block(t):
            return xs_hbm.at[pl.ds(pl.multiple_of(t * block_rows, block_rows), block_rows)]

        def fill_block(t, carry):
            pltpu.make_async_copy(zero_blk, block(t), pad_sem).start()
            return carry

        def drain_block(t, carry):
            pltpu.make_async_copy(zero_blk, block(t), pad_sem).wait()
            return carry

        lax.fori_loop(nv_ref[0], n_blocks, fill_block, 0)
        lax.fori_loop(nv_ref[0], n_blocks, drain_block, 0)


def _pair_table(pos, rows):
    steps = pos.shape[1] // rows
    return jnp.transpose(pos.reshape(2, steps, rows), (1, 0, 2)).reshape(steps, 1, 2 * rows)


def _moe_dispatch(xn, pair_tab, pad_start, pad_count, n_valid, n_tiles, tm):
    steps = pair_tab.shape[0]
    tb = pair_tab.shape[2] // 2
    return pl.pallas_call(
        _dispatch_kernel,
        grid_spec=pltpu.PrefetchScalarGridSpec(
            num_scalar_prefetch=3,
            grid=(steps,),
            in_specs=[pl.BlockSpec((1, 1, 2 * tb), lambda i, *_: (i, 0, 0), memory_space=pltpu.SMEM),
                      pl.BlockSpec((tb * TILE_ROWS, LANES), lambda i, *_: (i, 0))],
            out_specs=pl.BlockSpec(memory_space=pl.ANY),
            scratch_shapes=[pltpu.VMEM((tm * TILE_ROWS, LANES), F32),
                            pltpu.SemaphoreType.DMA, pltpu.SemaphoreType.DMA]),
        out_shape=jax.ShapeDtypeStruct((n_tiles * tm * TILE_ROWS, LANES), F32),
        compiler_params=pltpu.CompilerParams(
            dimension_semantics=("arbitrary",), vmem_limit_bytes=VMEM_LIMIT),
    )(pad_start, pad_count, n_valid, pair_tab, xn)


def _moe_kernel(te_ref, nv_ref, x_ref, wg_ref, wu_ref, wd_ref, o_ref, wg_b, wu_b, wd_b):
    i = pl.program_id(0)
    tm = x_ref.shape[0] // TILE_ROWS
    n_valid = nv_ref[0]
    new_expert = jnp.logical_or(i == 0, te_ref[i] != te_ref[jnp.maximum(i - 1, 0)])

    @pl.when(jnp.logical_and(i < n_valid, new_expert))
    def _():
        wg_b[...] = wg_ref[0].astype(BF16)
        wu_b[...] = wu_ref[0].astype(BF16)
        wd_b[...] = wd_ref[0].astype(BF16)

    @pl.when(i < n_valid)
    def _():
        x = _load_token_tiles(x_ref, 0, tm, TILE_ROWS).astype(BF16)
        g = jnp.dot(x, wg_b[...], preferred_element_type=F32)
        u = jnp.dot(x, wu_b[...], preferred_element_type=F32)
        hid = (g * _sigmoid(g) * u).astype(BF16)
        _store_token_tiles(o_ref, jnp.dot(hid, wd_b[...], preferred_element_type=F32))

    @pl.when(i >= n_valid)
    def _():
        o_ref[...] = jnp.zeros_like(o_ref)


def _moe_experts(xs, tile_expert, n_valid, tm, w_gate, w_up, w_down):
    n_exp, d, f = w_gate.shape
    assert d == TILE_ROWS * LANES
    n_tiles = xs.shape[0] // (tm * TILE_ROWS)
    x_index = lambda i, te, nv: (jnp.minimum(i, nv[0] - 1), 0)
    return pl.pallas_call(
        _moe_kernel,
        grid_spec=pltpu.PrefetchScalarGridSpec(
            num_scalar_prefetch=2,
            grid=(n_tiles,),
            in_specs=[
                pl.BlockSpec((tm * TILE_ROWS, LANES), x_index),
                pl.BlockSpec((1, d, f), lambda i, te, nv: (te[i], 0, 0)),
                pl.BlockSpec((1, d, f), lambda i, te, nv: (te[i], 0, 0)),
                pl.BlockSpec((1, f, d), lambda i, te, nv: (te[i], 0, 0))],
            out_specs=pl.BlockSpec((tm * TILE_ROWS, LANES), lambda i, te, nv: (i, 0)),
            scratch_shapes=[pltpu.VMEM((d, f), BF16), pltpu.VMEM((d, f), BF16), pltpu.VMEM((f, d), BF16)]),
        out_shape=jax.ShapeDtypeStruct((n_tiles * tm * TILE_ROWS, LANES), F32),
        compiler_params=pltpu.CompilerParams(
            dimension_semantics=("arbitrary",), vmem_limit_bytes=VMEM_LIMIT),
    )(tile_expert, n_valid, xs, w_gate, w_up, w_down)


def _combine_kernel(pos_ref, pos_next_ref, h_ref, route_ref, y_hbm, o_ref, ybuf, sems):
    i = pl.program_id(0)
    n_steps = pl.num_programs(0)
    tokens = ybuf.shape[1] // TILE_ROWS
    slot = i % 2

    @pl.when(i == 0)
    def _():
        _gather_tokens(y_hbm, pos_ref, ybuf.at[0], sems.at[0], tokens)

    @pl.when(i + 1 < n_steps)
    def _():
        for r in range(tokens):
            _token_copy(y_hbm, pos_next_ref[0, 0, r], ybuf.at[1 - slot], r,
                        sems.at[1 - slot]).start(priority=r % 2)

    _wait_tokens(y_hbm, ybuf.at[slot], sems.at[slot])
    tc = tokens // 2
    w = route_ref[...]
    first = _load_token_tiles(ybuf.at[slot], 0, tc, TILE_ROWS)
    second = _load_token_tiles(ybuf.at[slot], tc * TILE_ROWS, tc, TILE_ROWS)
    o_ref[...] = h_ref[...] + w[:, 2:3] * first + w[:, 3:4] * second


def _moe_combine(h, route, pair_tab, y_sorted):
    n, d = h.shape
    steps = pair_tab.shape[0]
    tc = pair_tab.shape[2] // 2
    return pl.pallas_call(
        _combine_kernel,
        grid=(steps,),
        in_specs=[pl.BlockSpec((1, 1, 2 * tc), lambda i: (i, 0, 0), memory_space=pltpu.SMEM),
                  pl.BlockSpec((1, 1, 2 * tc), lambda i: (jnp.minimum(i + 1, steps - 1), 0, 0),
                               memory_space=pltpu.SMEM),
                  pl.BlockSpec((tc, d), lambda i: (i, 0)),
                  pl.BlockSpec((tc, LANES), lambda i: (i, 0)),
                  pl.BlockSpec(memory_space=pl.ANY)],
        out_specs=pl.BlockSpec((tc, d), lambda i: (i, 0)),
        out_shape=jax.ShapeDtypeStruct((n, d), F32),
        scratch_shapes=[pltpu.VMEM((2, 2 * tc * TILE_ROWS, LANES), F32), pltpu.SemaphoreType.DMA((2,))],
        compiler_params=pltpu.CompilerParams(
            dimension_semantics=("arbitrary",), vmem_limit_bytes=VMEM_LIMIT),
    )(pair_tab, pair_tab, h, route, y_sorted)


def _moe_layer(h, layer, gain, w_group, w_expert, w_gate, w_up, w_down):
    d = h.shape[1]
    f = w_gate.shape[-1]
    xn, route, route_t, counts = _router(h, gain, w_group, w_expert)
    expert_ids = route_t[0:2].astype(jnp.int32)
    rank = route_t[4:6].astype(jnp.int32)
    pos, tile_expert, n_valid, n_tiles, pad_start, pad_count = _dispatch_tables(expert_ids, rank, counts,
                                                                                MOE_TM)
    pair_tab = _pair_table(pos, COMBINE_ROWS)
    xs = _moe_dispatch(xn, pair_tab, pad_start, pad_count, n_valid, n_tiles, MOE_TM)
    y_sorted = _moe_experts(xs, tile_expert + layer * N_EXPERTS, n_valid, MOE_TM,
                            w_gate.reshape(-1, d, f), w_up.reshape(-1, d, f), w_down.reshape(-1, f, d))
    return _moe_combine(h, route, pair_tab, y_sorted)


def _even_layer(h, batch, seq, gain, w_in, w_out, lb, f_bias, out_norm, q_norm, k_norm):
    d = h.shape[1]
    n_main = w_in.shape[1] - f_bias.shape[0]
    w_main = w_in[:, :n_main].astype(BF16)
    w_gate = jnp.zeros((d, LANES), F32).at[:, :f_bias.shape[0]].set(w_in[:, n_main:])
    proj, gates = _norm_proj(h, gain, w_main, w_gate)
    o_a = _hgrn2(proj, lb, out_norm, batch, seq)
    qa, ka, va = _fox_prep(proj, gates, f_bias, q_norm, k_norm, batch, seq, col0=4)
    o_b = lax.cond(_logit_bound(q_norm, k_norm) <= LOGIT_BOUND_MAX,
                   functools.partial(_fox_attention, fast=True),
                   functools.partial(_fox_attention, fast=False), qa, ka, va).reshape(batch * seq, -1)
    wo = w_out.astype(BF16)
    ka_dim = o_a.shape[1]
    return _proj_residual([(o_a, wo[:ka_dim]), (o_b, wo[ka_dim:])], h)


def _odd_layer(h, positions, batch, seq, gain, w_in, w_out, q_norm, k_norm, lam_params, subln, lambda_init):
    proj = _norm_proj(h, gain, w_in.astype(BF16))
    qm, k2, va = _diff_prep(proj, positions, q_norm, k_norm, batch, seq)
    attn = lambda fast: functools.partial(_diff_attention, lam_params=lam_params, subln=subln,
                                          lambda_init=lambda_init, fast=fast)
    o = lax.cond(_logit_bound(q_norm, k_norm) <= LOGIT_BOUND_MAX,
                 attn(True), attn(False), qm, k2, va).reshape(batch * seq, -1)
    return _proj_residual([(o, w_out.astype(BF16))], h)


def kernel(x, positions, hgrn_lb_logits, norm_mix, norm_ffn, even_w_in, even_w_out, fox_f_bias,
           hgrn_out_norm, fox_q_norm, fox_k_norm, odd_w_in, odd_w_out, diff_q_norm, diff_k_norm,
           diff_lambda_q1, diff_lambda_k1, diff_lambda_q2, diff_lambda_k2, diff_subln,
           moe_router_group, moe_router_expert, moe_w_gate, moe_w_up, moe_w_down):
    batch, seq, d = x.shape
    depth = norm_mix.shape[0]
    lower_bounds = jnp.cumsum(jax.nn.softmax(hgrn_lb_logits.astype(F32), axis=0), axis=0)
    h = x.reshape(batch * seq, d)
    for layer in range(depth):
        j = layer // 2
        if layer % 2 == 0:
            h = _even_layer(h, batch, seq, norm_mix[layer], even_w_in[j], even_w_out[j], lower_bounds[j],
                            fox_f_bias[j], hgrn_out_norm[j], fox_q_norm[j], fox_k_norm[j])
        else:
            lambda_init = 0.8 - 0.6 * math.exp(-0.3 * layer)
            lam_params = jnp.stack([diff_lambda_q1[j], diff_lambda_k1[j],
                                    diff_lambda_q2[j], diff_lambda_k2[j]]).astype(F32)
            h = _odd_layer(h, positions, batch, seq, norm_mix[layer], odd_w_in[j], odd_w_out[j],
                           diff_q_norm[j], diff_k_norm[j], lam_params, diff_subln[j], lambda_init)
        h = _moe_layer(h, layer, norm_ffn[layer], moe_router_group[layer], moe_router_expert[layer],
                       moe_w_gate, moe_w_up, moe_w_down)
    return h.reshape(batch, seq, d)
```
